```python
import math
import jax, jax.numpy as jnp
from jax import lax
import numpy as np

D_MODEL = 2048
BATCH = 8
SEQ = 2048
DEPTH = 2
DEC_BATCH = 32
DEC_SEQ = 1
PAST_LEN = 8192
PAGE_SIZE = 128

N_ATTN_LAYERS = (DEPTH + 1) // 2
N_SSD_LAYERS = DEPTH // 2
MIX_W = D_MODEL
A_HEADS = 8
HEAD_DIM = 128
A_W = A_HEADS * HEAD_DIM
ROT_DIM = HEAD_DIM // 4
ROPE_THETA = 500000.0
MOBA_BLOCK = 256
MOBA_TOPK = 3
N_SEL = MOBA_TOPK + 1
Q_CHUNK = 64
ATTN_SCALE = HEAD_DIM ** -0.5
LRU_W = MIX_W - A_W
LRU_HEADS = 8
LRU_BW = LRU_W // LRU_HEADS
LRU_C = 8.0
CONV_K = 4
IN_A_W = 3 * A_W + 2 * LRU_W
D_INNER = 2 * D_MODEL
SSD_HEAD_DIM = 64
SSD_HEADS = D_INNER // SSD_HEAD_DIM
SSD_STATE = 128
SSD_GROUPS = 8
SSD_CHUNK = 128
XBC_W = D_INNER + 2 * SSD_GROUPS * SSD_STATE
IN_S_W = D_INNER + XBC_W + SSD_HEADS
D_FF = 4 * D_MODEL
EPS = 1e-6

kernel_name = 'moba_rglru_ssd_hybrid_step'


def rmsnorm(x, g):
    x32 = x.astype(jnp.float32)
    y = x32 * lax.rsqrt(jnp.mean(jnp.square(x32), axis=-1, keepdims=True) + EPS)
    return (y * g.astype(jnp.float32)).astype(x.dtype)


def partial_rope(x, pos):
    half = ROT_DIM // 2
    inv = ROPE_THETA ** (-jnp.arange(half, dtype=jnp.float32) / half)
    ang = pos.astype(jnp.float32)[:, None] * inv
    cos = jnp.cos(ang)[:, None, :]
    sin = jnp.sin(ang)[:, None, :]
    xf = x.astype(jnp.float32)
    x1 = xf[..., :half]
    x2 = xf[..., half:ROT_DIM]
    out = jnp.concatenate([x1 * cos - x2 * sin, x2 * cos + x1 * sin, xf[..., ROT_DIM:]], axis=-1)
    return out.astype(x.dtype)


def causal_conv(x, buf, w, b):
    T = x.shape[1]
    xx = jnp.concatenate([buf.astype(x.dtype), x], axis=1)
    out = b
    for j in range(CONV_K):
        out = out + xx[:, j:j + T] * w[j]
    return out, xx[:, -(CONV_K - 1):]


def pad_blocks(x):
    extra = (-x.shape[-3]) % MOBA_BLOCK
    return jnp.pad(x, [(0, 0)] * (x.ndim - 3) + [(0, extra), (0, 0), (0, 0)])


def block_means(k):
    nb = k.shape[-3] // MOBA_BLOCK
    m = k.astype(jnp.float32).reshape(k.shape[:-3] + (nb, MOBA_BLOCK) + k.shape[-2:]).mean(axis=-3)
    extra = max(MOBA_TOPK - nb, 0)
    return jnp.pad(m, [(0, 0)] * (m.ndim - 3) + [(0, extra), (0, 0), (0, 0)])


def moba_queries(q, qpos, k_all, v_all, kmean):
    T, H, _ = q.shape
    nb = kmean.shape[0]
    qblk = qpos // MOBA_BLOCK
    q32 = q.astype(jnp.float32)
    gate = jnp.einsum('thd,nhd->thn', q32, kmean)
    past = jnp.arange(nb)[None, None, :] < qblk[:, None, None]
    gate = jnp.where(past, gate, -jnp.inf)
    _, top = lax.top_k(gate, MOBA_TOPK)
    own = jnp.broadcast_to(qblk[:, None, None], (T, H, 1))
    top_ok = top < own
    blocks = jnp.concatenate([jnp.where(top_ok, top, own), own], axis=-1)
    sel_ok = jnp.concatenate([top_ok, jnp.ones((T, H, 1), bool)], axis=-1)
    kpos = (blocks[..., None] * MOBA_BLOCK + jnp.arange(MOBA_BLOCK)).reshape(T, H, N_SEL * MOBA_BLOCK)
    visible = jnp.repeat(sel_ok, MOBA_BLOCK, axis=-1) & (kpos <= qpos[:, None, None])
    hidx = jnp.arange(H)[None, :, None]
    kg = k_all[kpos, hidx]
    vg = v_all[kpos, hidx]
    s = jnp.einsum('thd,thkd->thk', q32, kg.astype(jnp.float32)) * ATTN_SCALE
    p = jax.nn.softmax(jnp.where(visible, s, -jnp.inf), axis=-1)
    return jnp.einsum('thk,thkd->thd', p.astype(vg.dtype), vg)


def moba_prompt(q, k, v):
    B, T, H, Dh = q.shape
    nc = T // Q_CHUNK
    k_all = pad_blocks(k)
    v_all = pad_blocks(v)
    kmean = block_means(k_all)
    qc = q.reshape(B * nc, Q_CHUNK, H, Dh)
    posc = jnp.broadcast_to(jnp.arange(T, dtype=jnp.int32).reshape(1, nc, Q_CHUNK), (B, nc, Q_CHUNK)).reshape(B * nc, Q_CHUNK)
    bidx = jnp.repeat(jnp.arange(B), nc)

    def one(args):
        qi, pi, bi = args
        return moba_queries(qi, pi, k_all[bi], v_all[bi], kmean[bi]).astype(qi.dtype)

    return lax.map(one, (qc, posc, bidx)).reshape(B, T, H, Dh)


def paged_moba(k_pool, v_pool, page_table):
    past_len = page_table.shape[1] * PAGE_SIZE

    def attend(q, k, v):
        T = q.shape[1]
        qpos = past_len + jnp.arange(T, dtype=jnp.int32)

        def one(args):
            pt, qi, ki, vi = args
            kp = k_pool[pt].reshape(past_len, A_HEADS, HEAD_DIM)
            vp = v_pool[pt].reshape(past_len, A_HEADS, HEAD_DIM)
            k_all = pad_blocks(jnp.concatenate([kp, ki.astype(kp.dtype)], axis=0))
            v_all = pad_blocks(jnp.concatenate([vp, vi.astype(vp.dtype)], axis=0))
            return moba_queries(qi, qpos, k_all, v_all, block_means(k_all)).astype(qi.dtype)

        return lax.map(one, (page_table, q, k, v))

    return attend


def rg_lru(xc, h0, wa, ba, wx, bx, lam):
    Bq, T, W = xc.shape
    x32 = xc.astype(jnp.float32)
    xb = x32.reshape(Bq, T, LRU_HEADS, LRU_BW)
    r = jax.nn.sigmoid(jnp.einsum('btnc,ncd->btnd', xb, wa.astype(jnp.float32)).reshape(Bq, T, W) + ba.astype(jnp.float32))
    i = jax.nn.sigmoid(jnp.einsum('btnc,ncd->btnd', xb, wx.astype(jnp.float32)).reshape(Bq, T, W) + bx.astype(jnp.float32))
    log_a = -LRU_C * r * jax.nn.softplus(-lam.astype(jnp.float32))
    a = jnp.exp(log_a)
    u = jnp.sqrt(-jnp.expm1(2.0 * log_a)) * (i * x32)

    def step(h, au):
        at, ut = au
        h = at * h + ut
        return h, h

    h_last, hs = lax.scan(step, h0.astype(jnp.float32), (a.swapaxes(0, 1), u.swapaxes(0, 1)))
    return hs.swapaxes(0, 1), h_last


def moba_rglru_mixer(hn, pos, conv0, h0, attend, w_in, q_g, k_g, cw, cb, wa, ba, wx, bx, lam, w_out):
    Bq, T, _ = hn.shape
    proj = hn @ w_in
    q, k, v, xr, yg = jnp.split(proj, [A_W, 2 * A_W, 3 * A_W, 3 * A_W + LRU_W], axis=-1)
    shp = (Bq, T, A_HEADS, HEAD_DIM)
    q = partial_rope(rmsnorm(q.reshape(shp), q_g), pos)
    k = partial_rope(rmsnorm(k.reshape(shp), k_g), pos)
    v = v.reshape(shp)
    att = attend(q, k, v).reshape(Bq, T, A_W)
    xc, conv_new = causal_conv(xr, conv0, cw, cb)
    hs, h_last = rg_lru(xc, h0, wa, ba, wx, bx, lam)
    rec = hs.astype(hn.dtype) * jax.nn.gelu(yg)
    out = jnp.concatenate([att, rec], axis=-1) @ w_out
    return out, k, v, conv_new, h_last


def ssd_scan(x, dt, A, Bm, Cm, s0):
    b, T, H, P = x.shape
    G, N = Bm.shape[-2], Bm.shape[-1]
    E = H // G
    Lc = min(SSD_CHUNK, T)
    extra = (-T) % Lc
    nc = (T + extra) // Lc
    padt = lambda z: jnp.pad(z, [(0, 0), (0, extra)] + [(0, 0)] * (z.ndim - 2))
    x, dt, Bm, Cm = padt(x.astype(jnp.float32)), padt(dt), padt(Bm.astype(jnp.float32)), padt(Cm.astype(jnp.float32))
    xdt = (x * dt[..., None]).reshape(b, nc, Lc, G, E, P)
    cs = jnp.cumsum((dt * A).reshape(b, nc, Lc, G, E), axis=2)
    Bc = Bm.reshape(b, nc, Lc, G, N)
    Cc = Cm.reshape(b, nc, Lc, G, N)
    causal = jnp.tril(jnp.ones((Lc, Lc), bool))[:, :, None, None]
    seg = cs[:, :, :, None] - cs[:, :, None, :]
    decay = jnp.exp(jnp.where(causal, seg, -jnp.inf))
    cb = jnp.einsum('bclgn,bcsgn->bclsg', Cc, Bc)
    y_diag = jnp.einsum('bclsg,bclsge,bcsgep->bclgep', cb, decay, xdt)
    to_end = jnp.exp(cs[:, :, -1:] - cs)
    states = jnp.einsum('bclgn,bclge,bclgep->bcgepn', Bc, to_end, xdt)
    chunk_decay = jnp.exp(cs[:, :, -1])

    def step(s, inp):
        st, cd = inp
        return s * cd[..., None, None] + st, s

    s_last, s_prev = lax.scan(step, s0.astype(jnp.float32).reshape(b, G, E, P, N),
                              (states.swapaxes(0, 1), chunk_decay.swapaxes(0, 1)))
    s_prev = s_prev.swapaxes(0, 1)
    y_off = jnp.einsum('bclgn,bcgepn,bclge->bclgep', Cc, s_prev, jnp.exp(cs))
    y = (y_diag + y_off).reshape(b, nc * Lc, H, P)[:, :T]
    return y, s_last.reshape(b, H, P, N)


def gated_group_rmsnorm(y, z, g):
    u = y.astype(jnp.float32) * jax.nn.silu(z.astype(jnp.float32))
    u = u.reshape(y.shape[:-1] + (SSD_GROUPS, D_INNER // SSD_GROUPS))
    u = u * lax.rsqrt(jnp.mean(jnp.square(u), axis=-1, keepdims=True) + EPS)
    return (u.reshape(y.shape) * g.astype(jnp.float32)).astype(z.dtype)


def ssd_mixer(hn, conv0, s0, w_in, cw, cb, dt_bias, a_log, d_skip, norm_g, w_out):
    Bq, T, _ = hn.shape
    proj = hn @ w_in
    z, xbc, dt = jnp.split(proj, [D_INNER, D_INNER + XBC_W], axis=-1)
    xbc, conv_new = causal_conv(xbc, conv0, cw, cb)
    xbc = jax.nn.silu(xbc)
    xs, Bm, Cm = jnp.split(xbc, [D_INNER, D_INNER + SSD_GROUPS * SSD_STATE], axis=-1)
    dt = jax.nn.softplus(dt.astype(jnp.float32) + dt_bias.astype(jnp.float32))
    A = -jnp.exp(a_log.astype(jnp.float32))
    xh = xs.reshape(Bq, T, SSD_HEADS, SSD_HEAD_DIM)
    y, s_last = ssd_scan(xh, dt, A, Bm.reshape(Bq, T, SSD_GROUPS, SSD_STATE),
                         Cm.reshape(Bq, T, SSD_GROUPS, SSD_STATE), s0)
    y = y + xh.astype(jnp.float32) * d_skip.astype(jnp.float32)[:, None]
    y = gated_group_rmsnorm(y.reshape(Bq, T, D_INNER), z, norm_g)
    return y @ w_out, conv_new, s_last


def sq_relu_mlp(h, w_up, w_down):
    return jnp.square(jax.nn.relu(h @ w_up)) @ w_down


def setup_inputs(seed: int = 0) -> dict:
    key = jax.random.key(seed)
    k = jax.random.split(key, 40)
    f32 = jnp.float32
    nrm = lambda kk, shape, scale=1.0: jax.random.normal(kk, shape, f32) * scale
    n_pages = PAST_LEN // PAGE_SIZE
    n_phys = (5 * DEC_BATCH * n_pages) // 4
    page_table = jax.random.permutation(k[8], n_phys)[:DEC_BATCH * n_pages].reshape(DEC_BATCH, n_pages).astype(jnp.int32)
    u = jax.random.uniform(k[20], (N_ATTN_LAYERS, LRU_W), f32, 0.9, 0.999)
    a0 = u ** (1.0 / LRU_C)
    lam = jnp.log(a0) - jnp.log1p(-a0)
    dt0 = jnp.exp(jax.random.uniform(k[24], (N_SSD_LAYERS, SSD_HEADS), f32, math.log(1e-3), math.log(1e-1)))
    dt_bias = dt0 + jnp.log(-jnp.expm1(-dt0))
    return {
        'x_prompt': nrm(k[0], (BATCH, SEQ, D_MODEL)),
        'x_sample': nrm(k[1], (DEC_BATCH, DEC_SEQ, D_MODEL)),
        'cache_k': nrm(k[2], (N_ATTN_LAYERS, n_phys, PAGE_SIZE, A_HEADS, HEAD_DIM)),
        'cache_v': nrm(k[3], (N_ATTN_LAYERS, n_phys, PAGE_SIZE, A_HEADS, HEAD_DIM)),
        'state_lru_conv': nrm(k[4], (N_ATTN_LAYERS, DEC_BATCH, CONV_K - 1, LRU_W)),
        'state_lru_h': nrm(k[5], (N_ATTN_LAYERS, DEC_BATCH, LRU_W), 0.5),
        'state_ssd_conv': nrm(k[6], (N_SSD_LAYERS, DEC_BATCH, CONV_K - 1, XBC_W)),
        'state_ssd_h': nrm(k[7], (N_SSD_LAYERS, DEC_BATCH, SSD_HEADS, SSD_HEAD_DIM, SSD_STATE), 0.1),
        'page_table': page_table,
        'norm_mix': 1.0 + nrm(k[9], (DEPTH, D_MODEL), 0.02),
        'norm_mlp': 1.0 + nrm(k[10], (DEPTH, D_MODEL), 0.02),
        'w_in_a': nrm(k[11], (N_ATTN_LAYERS, D_MODEL, IN_A_W), D_MODEL ** -0.5),
        'q_norm': 1.0 + nrm(k[12], (N_ATTN_LAYERS, HEAD_DIM), 0.02),
        'k_norm': 1.0 + nrm(k[13], (N_ATTN_LAYERS, HEAD_DIM), 0.02),
        'lru_conv_w': nrm(k[14], (N_ATTN_LAYERS, CONV_K, LRU_W), CONV_K ** -0.5),
        'lru_conv_b': nrm(k[15], (N_ATTN_LAYERS, LRU_W), 0.01),
        'lru_wa': nrm(k[16], (N_ATTN_LAYERS, LRU_HEADS, LRU_BW, LRU_BW), LRU_BW ** -0.5),
        'lru_ba': nrm(k[17], (N_ATTN_LAYERS, LRU_W), 0.01),
        'lru_wx': nrm(k[18], (N_ATTN_LAYERS, LRU_HEADS, LRU_BW, LRU_BW), LRU_BW ** -0.5),
        'lru_bx': nrm(k[19], (N_ATTN_LAYERS, LRU_W), 0.01),
        'lru_lambda': lam,
        'w_out_a': nrm(k[21], (N_ATTN_LAYERS, MIX_W, D_MODEL), MIX_W ** -0.5),
        'w_in_s': nrm(k[22], (N_SSD_LAYERS, D_MODEL, IN_S_W), D_MODEL ** -0.5),
        'ssd_conv_w': nrm(k[23], (N_SSD_LAYERS, CONV_K, XBC_W), CONV_K ** -0.5),
        'ssd_conv_b': nrm(k[25], (N_SSD_LAYERS, XBC_W), 0.01),
        'ssd_dt_bias': dt_bias,
        'ssd_a_log': jnp.log(jax.random.uniform(k[26], (N_SSD_LAYERS, SSD_HEADS), f32, 1.0, 16.0)),
        'ssd_d': 1.0 + nrm(k[27], (N_SSD_LAYERS, SSD_HEADS), 0.1),
        'ssd_norm': 1.0 + nrm(k[28], (N_SSD_LAYERS, D_INNER), 0.02),
        'w_out_s': nrm(k[29], (N_SSD_LAYERS, D_INNER, D_MODEL), D_INNER ** -0.5),
        'w_up': nrm(k[30], (DEPTH, D_MODEL, D_FF), D_MODEL ** -0.5),
        'w_down': nrm(k[31], (DEPTH, D_FF, D_MODEL), D_FF ** -0.5),
    }


def reference(x_prompt, x_sample, cache_k, cache_v, state_lru_conv, state_lru_h, state_ssd_conv, state_ssd_h,
              page_table, norm_mix, norm_mlp, w_in_a, q_norm, k_norm, lru_conv_w, lru_conv_b, lru_wa, lru_ba,
              lru_wx, lru_bx, lru_lambda, w_out_a, w_in_s, ssd_conv_w, ssd_conv_b, ssd_dt_bias, ssd_a_log, ssd_d,
              ssd_norm, w_out_s, w_up, w_down):
    past_len = page_table.shape[1] * PAGE_SIZE
    Bp, Tp, _ = x_prompt.shape
    Ts = x_sample.shape[1]
    pos_p = jnp.arange(Tp, dtype=jnp.int32)
    pos_s = past_len + jnp.arange(Ts, dtype=jnp.int32)
    xp, xs = x_prompt, x_sample
    kp_l, vp_l, ks_l, vs_l = [], [], [], []
    lcp_l, lcs_l, lhp_l, lhs_l = [], [], [], []
    scp_l, scs_l, shp_l, shs_l = [], [], [], []
    for layer in range(DEPTH):
        i = layer // 2
        hp = rmsnorm(xp, norm_mix[layer])
        hs = rmsnorm(xs, norm_mix[layer])
        if layer % 2 == 0:
            wts = (w_in_a[i], q_norm[i], k_norm[i], lru_conv_w[i], lru_conv_b[i], lru_wa[i], lru_ba[i],
                   lru_wx[i], lru_bx[i], lru_lambda[i], w_out_a[i])
            mp, k_p, v_p, c_p, h_p = moba_rglru_mixer(
                hp, pos_p, jnp.zeros((Bp, CONV_K - 1, LRU_W), xp.dtype), jnp.zeros((Bp, LRU_W), jnp.float32),
                moba_prompt, *wts)
            ms, k_s, v_s, c_s, h_s = moba_rglru_mixer(
                hs, pos_s, state_lru_conv[i], state_lru_h[i], paged_moba(cache_k[i], cache_v[i], page_table), *wts)
            kp_l.append(k_p); vp_l.append(v_p); ks_l.append(k_s); vs_l.append(v_s)
            lcp_l.append(c_p); lcs_l.append(c_s); lhp_l.append(h_p); lhs_l.append(h_s)
        else:
            wts = (w_in_s[i], ssd_conv_w[i], ssd_conv_b[i], ssd_dt_bias[i], ssd_a_log[i], ssd_d[i], ssd_norm[i], w_out_s[i])
            mp, c_p, s_p = ssd_mixer(hp, jnp.zeros((Bp, CONV_K - 1, XBC_W), xp.dtype),
                                     jnp.zeros((Bp, SSD_HEADS, SSD_HEAD_DIM, SSD_STATE), jnp.float32), *wts)
            ms, c_s, s_s = ssd_mixer(hs, state_ssd_conv[i], state_ssd_h[i], *wts)
            scp_l.append(c_p); scs_l.append(c_s); shp_l.append(s_p); shs_l.append(s_s)
        xp = xp + mp
        xs = xs + ms
        xp = xp + sq_relu_mlp(rmsnorm(xp, norm_mlp[layer]), w_up[layer], w_down[layer])
        xs = xs + sq_relu_mlp(rmsnorm(xs, norm_mlp[layer]), w_up[layer], w_down[layer])
    return (xp, xs, jnp.stack(kp_l), jnp.stack(vp_l), jnp.stack(ks_l), jnp.stack(vs_l),
            jnp.stack(lcp_l), jnp.stack(lcs_l), jnp.stack(lhp_l), jnp.stack(lhs_l),
            jnp.stack(scp_l), jnp.stack(scs_l), jnp.stack(shp_l), jnp.stack(shs_l))
```

```python
import functools
import math

import jax
import jax.numpy as jnp
from jax import lax
from jax.experimental import pallas as pl
from jax.experimental.pallas import tpu as pltpu

F32 = jnp.float32
BF16 = jnp.bfloat16
HIGHEST = lax.Precision.HIGHEST

A_HEADS = 8
HEAD_DIM = 128
A_W = A_HEADS * HEAD_DIM
ROT_DIM = HEAD_DIM // 4
ROT_HALF = ROT_DIM // 2
ROPE_THETA = 500000.0
MOBA_BLOCK = 256
MOBA_TOPK = 3
ATTN_SCALE = HEAD_DIM ** -0.5
LRU_HEADS = 8
LRU_C = 8.0
CONV_K = 4
SSD_HEAD_DIM = 64
SSD_STATE = 128
SSD_GROUPS = 8
SSD_CHUNK = 128
PAGE_SIZE = 128
EPS = 1e-6

LANES = 128
SUBLANES = 8
NEG_BIG = -1e30
VMEM_LIMIT = 56 * 1024 * 1024

NT_DIMS = (((1,), (1,)), ((), ()))


def _cparams(sem):
    return pltpu.CompilerParams(dimension_semantics=sem, vmem_limit_bytes=VMEM_LIMIT)


def _rms_rows(x, g):
    return x * lax.rsqrt(jnp.mean(x * x, axis=-1, keepdims=True) + EPS) * g


def _head_norm_rope(acc, g, cos, sin):
    y = _rms_rows(acc, g)
    lane = lax.broadcasted_iota(jnp.int32, y.shape, 1)
    partner = jnp.where(lane < ROT_HALF, pltpu.roll(y, LANES - ROT_HALF, 1), pltpu.roll(y, ROT_HALF, 1))
    return y * cos + partner * sin


def _norm_proj_kernel(*refs, ranges, n_rope, has_extra, tn):
    it = iter(refs)
    x_ref, g_ref, w_ref = next(it), next(it), next(it)
    rope_g = [next(it) for _ in range(n_rope)]
    if n_rope:
        cos_ref, sin_ref = next(it), next(it)
    if has_extra:
        we_ref = next(it)
    outs = [next(it) for _ in ranges]
    if has_extra:
        oe_ref = next(it)
    xn_ref = next(it)
    j = pl.program_id(1)

    @pl.when(j == 0)
    def _():
        xn_ref[...] = _rms_rows(x_ref[...], g_ref[...]).astype(BF16)
        if has_extra:
            oe_ref[...] = jnp.dot(xn_ref[...], we_ref[...], preferred_element_type=F32)

    acc = jnp.dot(xn_ref[...], w_ref[...], preferred_element_type=F32)
    for s, (a, b) in enumerate(ranges):
        @pl.when((j >= a) & (j < b))
        def _(s=s):
            if s < n_rope:
                cos, sin = cos_ref[...], sin_ref[...]
                for hh in range(tn // HEAD_DIM):
                    sl = slice(hh * HEAD_DIM, (hh + 1) * HEAD_DIM)
                    outs[s][:, sl] = _head_norm_rope(acc[:, sl], rope_g[s][...], cos, sin)
            else:
                outs[s][...] = acc


def _norm_proj(x, g, w, splits, *, tm, tn=512, rope=None, extra_w=None):
    M, K = x.shape
    N = sum(splits)
    assert w.shape == (K, N) and M % tm == 0 and all(s % tn == 0 for s in splits)
    edges = [0]
    for s in splits:
        edges.append(edges[-1] + s // tn)
    ranges = tuple((edges[i], edges[i + 1]) for i in range(len(splits)))
    n_rope = 0 if rope is None else len(rope[0])
    in_specs = [pl.BlockSpec((tm, K), lambda i, j: (i, 0)),
                pl.BlockSpec((1, K), lambda i, j: (0, 0)),
                pl.BlockSpec((K, tn), lambda i, j: (0, j))]
    args = [x, g.reshape(1, K), w]
    if rope is not None:
        gains, cos, sin, period = rope
        for gg in gains:
            in_specs.append(pl.BlockSpec((1, HEAD_DIM), lambda i, j: (0, 0)))
            args.append(gg.reshape(1, HEAD_DIM))
        for tab in (cos, sin):
            in_specs.append(pl.BlockSpec((tm, HEAD_DIM), lambda i, j, period=period: (i % period, 0)))
            args.append(tab)
    if extra_w is not None:
        in_specs.append(pl.BlockSpec((K, LANES), lambda i, j: (0, 0)))
        args.append(extra_w)
    out_specs, out_shape = [], []
    for (a, b), s in zip(ranges, splits):
        out_specs.append(pl.BlockSpec((tm, tn), lambda i, j, a=a, b=b: (i, jnp.clip(j - a, 0, b - a - 1))))
        out_shape.append(jax.ShapeDtypeStruct((M, s), F32))
    if extra_w is not None:
        out_specs.append(pl.BlockSpec((tm, LANES), lambda i, j: (i, 0)))
        out_shape.append(jax.ShapeDtypeStruct((M, LANES), F32))
    kern = functools.partial(_norm_proj_kernel, ranges=ranges, n_rope=n_rope,
                             has_extra=extra_w is not None, tn=tn)
    return pl.pallas_call(
        kern, grid=(M // tm, N // tn), in_specs=in_specs, out_specs=out_specs, out_shape=out_shape,
        scratch_shapes=[pltpu.VMEM((tm, K), BF16)],
        compiler_params=_cparams(("parallel", "arbitrary")), name="norm_proj")(*args)


def _out_proj_kernel(*refs, n_parts):
    xs = refs[:n_parts]
    ws = refs[n_parts:2 * n_parts]
    res_ref, o_ref = refs[2 * n_parts], refs[2 * n_parts + 1]
    acc = res_ref[...]
    for x_ref, w_ref in zip(xs, ws):
        acc = acc + jnp.dot(x_ref[...], w_ref[...], preferred_element_type=F32)
    o_ref[...] = acc


def _out_proj(parts, w, res, *, tm, tn=512):
    M, N = res.shape
    kp = parts[0].shape[1]
    assert all(p.shape == (M, kp) for p in parts) and w.shape == (kp * len(parts), N)
    in_specs = [pl.BlockSpec((tm, kp), lambda i, j: (i, 0)) for _ in parts]
    in_specs += [pl.BlockSpec((kp, tn), lambda i, j, p=p: (p, j)) for p in range(len(parts))]
    in_specs.append(pl.BlockSpec((tm, tn), lambda i, j: (i, j)))
    return pl.pallas_call(
        functools.partial(_out_proj_kernel, n_parts=len(parts)),
        grid=(M // tm, N // tn), in_specs=in_specs,
        out_specs=pl.BlockSpec((tm, tn), lambda i, j: (i, j)),
        out_shape=jax.ShapeDtypeStruct((M, N), F32),
        compiler_params=_cparams(("parallel", "parallel")), name="out_proj")(*parts, *([w] * len(parts)), res)


def _mlp_kernel(x_ref, g_ref, wu_ref, wd_ref, o_ref, xn_ref):
    f = pl.program_id(1)

    @pl.when(f == 0)
    def _():
        x = x_ref[...]
        xn_ref[...] = _rms_rows(x, g_ref[...]).astype(BF16)
        o_ref[...] = x

    h = jnp.maximum(jnp.dot(xn_ref[...], wu_ref[...], preferred_element_type=F32), 0.0)
    o_ref[...] += jnp.dot((h * h).astype(BF16), wd_ref[...], preferred_element_type=F32)


def _mlp(x, g, w_up, w_down, *, tm, tf=512):
    M, D = x.shape
    FF = w_up.shape[1]
    return pl.pallas_call(
        _mlp_kernel, grid=(M // tm, FF // tf),
        in_specs=[pl.BlockSpec((tm, D), lambda i, f: (i, 0)),
                  pl.BlockSpec((1, D), lambda i, f: (0, 0)),
                  pl.BlockSpec((D, tf), lambda i, f: (0, f)),
                  pl.BlockSpec((tf, D), lambda i, f: (f, 0))],
        out_specs=pl.BlockSpec((tm, D), lambda i, f: (i, 0)),
        out_shape=jax.ShapeDtypeStruct((M, D), F32),
        scratch_shapes=[pltpu.VMEM((tm, D), BF16)],
        compiler_params=_cparams(("parallel", "arbitrary")), name="mlp")(x, g.reshape(1, D), w_up, w_down)


def _moba_prompt_kernel(q_ref, k_ref, v_ref, o_ref, kmean_ref, *, nb):
    qb = pl.program_id(2)
    blk = MOBA_BLOCK

    @pl.when(qb == 0)
    def _():
        kmean_ref[...] = jnp.zeros(kmean_ref.shape, F32)
        for n in range(nb):
            kmean_ref[n:n + 1, :] = jnp.mean(k_ref[n * blk:(n + 1) * blk, :], axis=0, keepdims=True)

    q = q_ref[...]
    gate = lax.dot_general(q, kmean_ref[...], NT_DIMS, precision=HIGHEST, preferred_element_type=F32)
    lane = lax.broadcasted_iota(jnp.int32, gate.shape, 1)
    gate = jnp.where(lane < qb, gate, -jnp.inf)
    sel = jnp.zeros(gate.shape, F32)
    for _ in range(MOBA_TOPK):
        m = jnp.max(gate, axis=-1, keepdims=True)
        idx = jnp.min(jnp.where(gate == m, lane, LANES), axis=-1, keepdims=True)
        hit = lane == idx
        sel = jnp.where(hit & (m > -jnp.inf), 1.0, sel)
        gate = jnp.where(hit, -jnp.inf, gate)

    qh = q.astype(BF16)

    def scores(n):
        start = pl.multiple_of(n * blk, blk)
        kn = k_ref[pl.ds(start, blk), :].astype(BF16)
        vn = v_ref[pl.ds(start, blk), :].astype(BF16)
        return lax.dot_general(qh, kn, NT_DIMS, preferred_element_type=F32) * ATTN_SCALE, vn

    row = lax.broadcasted_iota(jnp.int32, (blk, blk), 0)
    col = lax.broadcasted_iota(jnp.int32, (blk, blk), 1)
    s, vn = scores(qb)
    s = jnp.where(col <= row, s, NEG_BIG)
    m = jnp.max(s, axis=-1, keepdims=True)
    p = jnp.exp(s - m)
    l = jnp.sum(p, axis=-1, keepdims=True)
    acc = jnp.dot(p.astype(BF16), vn, preferred_element_type=F32)

    def body(n, carry):
        m, l, acc = carry
        seln = jnp.max(jnp.where(lane == n, sel, 0.0), axis=-1, keepdims=True) > 0.0
        s, vn = scores(n)
        s = jnp.where(seln, s, NEG_BIG)
        m_new = jnp.maximum(m, jnp.max(s, axis=-1, keepdims=True))
        alpha = jnp.exp(m - m_new)
        p = jnp.exp(s - m_new)
        l = alpha * l + jnp.sum(p, axis=-1, keepdims=True)
        acc = alpha * acc + jnp.dot(p.astype(BF16), vn, preferred_element_type=F32)
        return m_new, l, acc

    m, l, acc = lax.fori_loop(0, qb, body, (m, l, acc))
    o_ref[...] = (acc / l).astype(o_ref.dtype)


def _moba_prompt(q, k, v, *, batch, seq):
    M = q.shape[0]
    nb = seq // MOBA_BLOCK
    assert nb <= LANES and seq % MOBA_BLOCK == 0
    return pl.pallas_call(
        functools.partial(_moba_prompt_kernel, nb=nb),
        grid=(batch, A_HEADS, nb),
        in_specs=[pl.BlockSpec((MOBA_BLOCK, HEAD_DIM), lambda b, h, n: (b * nb + n, h)),
                  pl.BlockSpec((seq, HEAD_DIM), lambda b, h, n: (b, h)),
                  pl.BlockSpec((seq, HEAD_DIM), lambda b, h, n: (b, h))],
        out_specs=pl.BlockSpec((MOBA_BLOCK, HEAD_DIM), lambda b, h, n: (b * nb + n, h)),
        out_shape=jax.ShapeDtypeStruct((M, A_W), BF16),
        scratch_shapes=[pltpu.VMEM((LANES, HEAD_DIM), F32)],
        compiler_params=_cparams(("parallel", "parallel", "arbitrary")), name="moba_prompt")(q, k, v)


def _lru_gates(xc, wa_ref, ba, wx_ref, bx, lam):
    bw = xc.shape[1] // LRU_HEADS
    rs, is_ = [], []
    for n in range(LRU_HEADS):
        xb = xc[:, n * bw:(n + 1) * bw].astype(BF16)
        rs.append(jnp.dot(xb, wa_ref[n], preferred_element_type=F32))
        is_.append(jnp.dot(xb, wx_ref[n], preferred_element_type=F32))
    r = jax.nn.sigmoid(jnp.concatenate(rs, axis=-1) + ba)
    i = jax.nn.sigmoid(jnp.concatenate(is_, axis=-1) + bx)
    log_a = -LRU_C * r * jax.nn.softplus(-lam)
    a = jnp.exp(log_a)
    t = jnp.tanh(log_a)
    u = jnp.sqrt(-2.0 * t / (1.0 - t)) * (i * xc)
    return a, u


def _lru_prompt_kernel(xr_ref, yg_ref, cw_ref, cb_ref, wa_ref, ba_ref, wx_ref, bx_ref, lam_ref,
                       rec_ref, hl_ref, xx_ref, a_ref, u_ref, h_ref, *, tt):
    ti = pl.program_id(1)
    pad = SUBLANES

    @pl.when(ti == 0)
    def _():
        xx_ref[0:pad, :] = jnp.zeros((pad, xx_ref.shape[1]), F32)
        h_ref[...] = jnp.zeros(h_ref.shape, F32)

    @pl.when(ti > 0)
    def _():
        xx_ref[0:pad, :] = xx_ref[tt:tt + pad, :]

    xx_ref[pad:pad + tt, :] = xr_ref[...]
    xc = cb_ref[...]
    for j in range(CONV_K):
        off = pad - (CONV_K - 1) + j
        xc = xc + xx_ref[off:off + tt, :] * cw_ref[j:j + 1, :]
    a, u = _lru_gates(xc, wa_ref, ba_ref[...], wx_ref, bx_ref[...], lam_ref[...])
    a_ref[...] = a
    u_ref[...] = u

    sub = lax.broadcasted_iota(jnp.int32, (SUBLANES, a.shape[1]), 0)

    def group(gi, h):
        r0 = pl.multiple_of(gi * SUBLANES, SUBLANES)
        a8 = a_ref[pl.ds(r0, SUBLANES), :]
        u8 = u_ref[pl.ds(r0, SUBLANES), :]
        for s in (1, 2, 4):
            keep = sub >= s
            a_sh = pltpu.roll(a8, s, 0)
            u_sh = pltpu.roll(u8, s, 0)
            u8 = jnp.where(keep, a8 * u_sh + u8, u8)
            a8 = jnp.where(keep, a8 * a_sh, a8)
        hs = a8 * h + u8
        yg = yg_ref[pl.ds(r0, SUBLANES), :]
        rec_ref[pl.ds(r0, SUBLANES), :] = (hs * jax.nn.gelu(yg)).astype(rec_ref.dtype)
        return hs[SUBLANES - 1:SUBLANES, :]

    h = lax.fori_loop(0, tt // SUBLANES, group, h_ref[...])
    h_ref[...] = h

    @pl.when(ti == pl.num_programs(1) - 1)
    def _():
        hl_ref[0] = h


def _lru_prompt(xr, yg, cw, cb, wa, ba, wx, bx, lam, *, batch, seq, tt=256):
    M, W = xr.shape
    nt = seq // tt
    row = lambda a: a.reshape(1, W)
    full2 = lambda shp: pl.BlockSpec(shp, lambda b, t: (0, 0))
    full3 = lambda shp: pl.BlockSpec(shp, lambda b, t: (0, 0, 0))
    return pl.pallas_call(
        functools.partial(_lru_prompt_kernel, tt=tt),
        grid=(batch, nt),
        in_specs=[pl.BlockSpec((tt, W), lambda b, t: (b * nt + t, 0)),
                  pl.BlockSpec((tt, W), lambda b, t: (b * nt + t, 0)),
                  full2((CONV_K, W)), full2((1, W)), full3(wa.shape), full2((1, W)),
                  full3(wx.shape), full2((1, W)), full2((1, W))],
        out_specs=[pl.BlockSpec((tt, W), lambda b, t: (b * nt + t, 0)),
                   pl.BlockSpec((1, 1, W), lambda b, t: (b, 0, 0))],
        out_shape=[jax.ShapeDtypeStruct((M, W), BF16), jax.ShapeDtypeStruct((batch, 1, W), F32)],
        scratch_shapes=[pltpu.VMEM((tt + SUBLANES, W), F32), pltpu.VMEM((tt, W), F32),
                        pltpu.VMEM((tt, W), F32), pltpu.VMEM((1, W), F32)],
        compiler_params=_cparams(("parallel", "arbitrary")), name="lru_prompt")(
            xr, yg, cw, row(cb), wa.astype(BF16), row(ba), wx.astype(BF16), row(bx), row(lam))


def _lru_sample_kernel(xr_ref, yg_ref, buf_ref, h0_ref, cw_ref, cb_ref, wa_ref, ba_ref, wx_ref, bx_ref,
                       lam_ref, rec_ref, hn_ref):
    xc = cb_ref[...]
    for j in range(CONV_K - 1):
        xc = xc + buf_ref[j] * cw_ref[j:j + 1, :]
    xc = xc + xr_ref[...] * cw_ref[CONV_K - 1:CONV_K, :]
    a, u = _lru_gates(xc, wa_ref, ba_ref[...], wx_ref, bx_ref[...], lam_ref[...])
    h = a * h0_ref[...] + u
    hn_ref[...] = h
    rec_ref[...] = (h * jax.nn.gelu(yg_ref[...])).astype(rec_ref.dtype)


def _lru_sample(xr, yg, buf_t, h0, cw, cb, wa, ba, wx, bx, lam):
    Bs, W = xr.shape
    row = lambda a: a.reshape(1, W)
    return pl.pallas_call(
        _lru_sample_kernel,
        out_shape=[jax.ShapeDtypeStruct((Bs, W), BF16), jax.ShapeDtypeStruct((Bs, W), F32)],
        compiler_params=pltpu.CompilerParams(vmem_limit_bytes=VMEM_LIMIT), name="lru_sample")(
            xr, yg, buf_t, h0, cw, row(cb), wa.astype(BF16), row(ba), wx.astype(BF16), row(bx), row(lam))


def _ssd_prompt_kernel(xbc_ref, prev_ref, z_ref, dt_ref, cw_ref, cb_ref, dtb_ref, alog_ref, dsk_ref, ng_ref,
                       ex_ref, yn_ref, sl_ref, xx_ref, xcv_ref, y_ref, st_ref, *, d_inner):
    c = pl.program_id(1)
    L = SSD_CHUNK
    pad = SUBLANES
    gw = d_inner // SSD_GROUPS
    hpg = gw // SSD_HEAD_DIM
    nst = SSD_STATE

    @pl.when(c == 0)
    def _():
        xx_ref[0:pad, :] = jnp.zeros((pad, xx_ref.shape[1]), F32)
        st_ref[...] = jnp.zeros(st_ref.shape, F32)

    @pl.when(c > 0)
    def _():
        xx_ref[0:pad, :] = prev_ref[...]

    xx_ref[pad:pad + L, :] = xbc_ref[...]
    xc = cb_ref[...]
    for j in range(CONV_K):
        off = pad - (CONV_K - 1) + j
        xc = xc + xx_ref[off:off + L, :] * cw_ref[j:j + 1, :]
    xcv_ref[...] = xc * jax.nn.sigmoid(xc)

    dt = jax.nn.softplus(dt_ref[...] + dtb_ref[...])
    dta = dt * (-jnp.exp(alog_ref[...]))
    r_i = lax.broadcasted_iota(jnp.int32, (L, L), 0)
    c_i = lax.broadcasted_iota(jnp.int32, (L, L), 1)
    tril = c_i <= r_i
    cs = jnp.dot(tril.astype(F32), dta, precision=HIGHEST, preferred_element_type=F32)
    cs_t = cs.T
    last = cs[L - 1:L, :]
    stack = jnp.concatenate([dt, jnp.exp(cs), jnp.exp(last - cs)], axis=0)

    for g in range(SSD_GROUPS):
        lo = g * gw
        ex = jnp.dot(stack, ex_ref[:, lo:lo + gw], precision=HIGHEST, preferred_element_type=F32)
        dt_x, ecs_x, end_x = ex[0:L], ex[L:2 * L], ex[2 * L:3 * L]
        xg = xcv_ref[:, lo:lo + gw]
        bg = xcv_ref[:, d_inner + g * nst:d_inner + (g + 1) * nst]
        cg = xcv_ref[:, d_inner + SSD_GROUPS * nst + g * nst:d_inner + SSD_GROUPS * nst + (g + 1) * nst].astype(BF16)
        xdt = xg * dt_x
        bg_t = bg.T.astype(BF16)
        cb = jnp.dot(cg, bg_t, preferred_element_type=F32)
        st_g = st_ref[:, lo:lo + gw]
        y_off = jnp.dot(cg, st_g.astype(BF16), preferred_element_type=F32) * ecs_x
        st_ref[:, lo:lo + gw] = st_g * ecs_x[L - 1:L, :] + jnp.dot(
            bg_t, (xdt * end_x).astype(BF16), preferred_element_type=F32)
        xdt_h = xdt.astype(BF16)
        for e in range(hpg):
            h = g * hpg + e
            seg = cs[:, h:h + 1] - cs_t[h:h + 1, :]
            dec = jnp.exp(jnp.where(tril, seg, NEG_BIG))
            yd = jnp.dot((cb * dec).astype(BF16), xdt_h[:, e * SSD_HEAD_DIM:(e + 1) * SSD_HEAD_DIM],
                         preferred_element_type=F32)
            y_ref[:, h * SSD_HEAD_DIM:(h + 1) * SSD_HEAD_DIM] = yd + y_off[:, e * SSD_HEAD_DIM:(e + 1) * SSD_HEAD_DIM]

    for g in range(SSD_GROUPS):
        lo = g * gw
        y = y_ref[:, lo:lo + gw] + xcv_ref[:, lo:lo + gw] * dsk_ref[:, lo:lo + gw]
        zz = z_ref[:, lo:lo + gw]
        u = y * (zz * jax.nn.sigmoid(zz))
        yn_ref[:, lo:lo + gw] = _rms_rows(u, ng_ref[:, lo:lo + gw]).astype(yn_ref.dtype)

    @pl.when(c == pl.num_programs(1) - 1)
    def _():
        for k in range(d_inner // LANES):
            sl_ref[0, k * LANES:(k + 1) * LANES, :] = st_ref[:, k * LANES:(k + 1) * LANES].T


def _ssd_expand_matrix(n_heads, d_inner):
    h = lax.broadcasted_iota(jnp.int32, (LANES, d_inner), 0)
    col = lax.broadcasted_iota(jnp.int32, (LANES, d_inner), 1)
    return (col // SSD_HEAD_DIM == h).astype(F32) * (h < n_heads).astype(F32)


def _pad_lanes(v):
    return jnp.pad(v, (0, LANES - v.shape[0])).reshape(1, LANES)


def _ssd_prompt(xbc, z, dt, cw, cb, dt_bias, a_log, d_skip, norm_g, *, batch, seq):
    M, XW = xbc.shape
    d_inner = z.shape[1]
    n_heads = d_inner // SSD_HEAD_DIM
    L = SSD_CHUNK
    nc = seq // L
    rpb = L // SUBLANES
    const2 = lambda shp: pl.BlockSpec(shp, lambda b, c: (0, 0))
    return pl.pallas_call(
        functools.partial(_ssd_prompt_kernel, d_inner=d_inner),
        grid=(batch, nc),
        in_specs=[pl.BlockSpec((L, XW), lambda b, c: (b * nc + c, 0)),
                  pl.BlockSpec((SUBLANES, XW), lambda b, c: (jnp.maximum((b * nc + c) * rpb - 1, 0), 0)),
                  pl.BlockSpec((L, d_inner), lambda b, c: (b * nc + c, 0)),
                  pl.BlockSpec((L, LANES), lambda b, c: (b * nc + c, 0)),
                  const2((CONV_K, XW)), const2((1, XW)), const2((1, LANES)), const2((1, LANES)),
                  const2((1, d_inner)), const2((1, d_inner)), const2((LANES, d_inner))],
        out_specs=[pl.BlockSpec((L, d_inner), lambda b, c: (b * nc + c, 0)),
                   pl.BlockSpec((1, d_inner, SSD_STATE), lambda b, c: (b, 0, 0))],
        out_shape=[jax.ShapeDtypeStruct((M, d_inner), BF16),
                   jax.ShapeDtypeStruct((batch, d_inner, SSD_STATE), F32)],
        scratch_shapes=[pltpu.VMEM((L + SUBLANES, XW), F32), pltpu.VMEM((L, XW), F32),
                        pltpu.VMEM((L, d_inner), F32), pltpu.VMEM((SSD_STATE, d_inner), F32)],
        compiler_params=_cparams(("parallel", "arbitrary")), name="ssd_prompt")(
            xbc, xbc, z, dt, cw, cb.reshape(1, XW), _pad_lanes(dt_bias), _pad_lanes(a_log),
            jnp.repeat(d_skip, SSD_HEAD_DIM).reshape(1, d_inner), norm_g.reshape(1, d_inner),
            _ssd_expand_matrix(n_heads, d_inner))


def _ssd_sample_pre_kernel(x_ref, buf_ref, dt_ref, cw_ref, cb_ref, dtb_ref, alog_ref, ex_ref,
                           xcs_ref, dtx_ref, dax_ref):
    xc = cb_ref[...]
    for j in range(CONV_K - 1):
        xc = xc + buf_ref[j] * cw_ref[j:j + 1, :]
    xc = xc + x_ref[...] * cw_ref[CONV_K - 1:CONV_K, :]
    xcs_ref[...] = xc * jax.nn.sigmoid(xc)
    dt = jax.nn.softplus(dt_ref[...] + dtb_ref[...])
    da = jnp.exp(dt * (-jnp.exp(alog_ref[...])))
    ex = ex_ref[...]
    dtx_ref[...] = jnp.dot(dt, ex, precision=HIGHEST, preferred_element_type=F32)
    dax_ref[...] = jnp.dot(da, ex, precision=HIGHEST, preferred_element_type=F32)


def _ssd_sample_state_kernel(s0_ref, x_ref, b_ref, c_ref, dtx_ref, dax_ref, z_ref, dsk_ref, ng_ref,
                             sn_ref, yn_ref, yrow_ref):
    hpg = s0_ref.shape[1]
    P = SSD_HEAD_DIM
    eye = lax.broadcasted_iota(jnp.int32, (P, P), 0) == lax.broadcasted_iota(jnp.int32, (P, P), 1)
    x = x_ref[0]
    xdt = x * dtx_ref[0]
    b_row = b_ref[0]
    c_rows = jnp.broadcast_to(c_ref[0], (SUBLANES, SSD_STATE)).astype(BF16)
    for e in range(hpg):
        sl = slice(e * P, (e + 1) * P)
        xcol = jnp.sum(jnp.where(eye, jnp.broadcast_to(xdt[:, sl], (P, P)), 0.0), axis=1, keepdims=True)
        da = dax_ref[0][:, e * P:e * P + 1]
        s_new = s0_ref[0, e] * da + xcol * b_row
        sn_ref[0, e] = s_new
        yr = lax.dot_general(c_rows, s_new.astype(BF16), NT_DIMS, preferred_element_type=F32)
        yrow_ref[:, sl] = yr[0:1, :]
    y = yrow_ref[...] + x * dsk_ref[...]
    zz = z_ref[0]
    u = y * (zz * jax.nn.sigmoid(zz))
    yn_ref[0] = _rms_rows(u, ng_ref[...]).astype(yn_ref.dtype)


def _ssd_sample(xbc, buf_t, z, dt, s0, cw, cb, dt_bias, a_log, d_skip, norm_g):
    Bs, XW = xbc.shape
    d_inner = z.shape[1]
    n_heads = d_inner // SSD_HEAD_DIM
    gw = d_inner // SSD_GROUPS
    hpg = gw // SSD_HEAD_DIM
    ex = _ssd_expand_matrix(n_heads, d_inner)
    xcs, dtx, dax = pl.pallas_call(
        _ssd_sample_pre_kernel,
        out_shape=[jax.ShapeDtypeStruct((Bs, XW), F32), jax.ShapeDtypeStruct((Bs, d_inner), F32),
                   jax.ShapeDtypeStruct((Bs, d_inner), F32)],
        compiler_params=pltpu.CompilerParams(vmem_limit_bytes=VMEM_LIMIT), name="ssd_sample_pre")(
            xbc, buf_t, dt, cw, cb.reshape(1, XW), _pad_lanes(dt_bias), _pad_lanes(a_log), ex)
    r3 = lambda a: a.reshape(Bs, 1, a.shape[1])
    nbg = d_inner // SSD_STATE
    rowblk = lambda w, off=0: pl.BlockSpec((1, 1, w), lambda b, g, off=off: (b, 0, g + off))
    s_new, yn = pl.pallas_call(
        _ssd_sample_state_kernel,
        grid=(Bs, SSD_GROUPS),
        in_specs=[pl.BlockSpec((1, hpg, SSD_HEAD_DIM, SSD_STATE), lambda b, g: (b, g, 0, 0)),
                  rowblk(gw), rowblk(SSD_STATE, nbg), rowblk(SSD_STATE, nbg + SSD_GROUPS),
                  rowblk(gw), rowblk(gw), rowblk(gw),
                  pl.BlockSpec((1, gw), lambda b, g: (0, g)), pl.BlockSpec((1, gw), lambda b, g: (0, g))],
        out_specs=[pl.BlockSpec((1, hpg, SSD_HEAD_DIM, SSD_STATE), lambda b, g: (b, g, 0, 0)),
                   rowblk(gw)],
        out_shape=[jax.ShapeDtypeStruct(s0.shape, F32), jax.ShapeDtypeStruct((Bs, 1, d_inner), BF16)],
        scratch_shapes=[pltpu.VMEM((1, gw), F32)],
        compiler_params=_cparams(("parallel", "parallel")), name="ssd_sample_state")(
            s0, r3(xcs), r3(xcs), r3(xcs), r3(dtx), r3(dax), r3(z),
            jnp.repeat(d_skip, SSD_HEAD_DIM).reshape(1, d_inner), norm_g.reshape(1, d_inner))
    return yn.reshape(Bs, d_inner), s_new


def _paged_topk_kernel(pt_ref, ka_ref, kb_ref, q_ref, idx_ref, ksum_ref, *, nblk):
    n = pl.program_id(1)
    tot = jnp.sum(ka_ref[0], axis=0, keepdims=True) + jnp.sum(kb_ref[0], axis=0, keepdims=True)
    ksum_ref[pl.ds(n, 1), :] = tot / float(MOBA_BLOCK)

    @pl.when(n == nblk - 1)
    def _():
        q = q_ref[0]
        rows = lax.broadcasted_iota(jnp.int32, (nblk, 1), 0)
        lane = lax.broadcasted_iota(jnp.int32, (1, LANES), 1)
        for h in range(A_HEADS):
            sl = slice(h * HEAD_DIM, (h + 1) * HEAD_DIM)
            gate = jnp.sum(ksum_ref[:, sl] * q[:, sl], axis=-1, keepdims=True)
            out = jnp.zeros((1, LANES), jnp.int32)
            for t in range(MOBA_TOPK):
                m = jnp.max(gate, axis=0, keepdims=True)
                idx = jnp.min(jnp.where(gate == m, rows, nblk), axis=0, keepdims=True)
                out = jnp.where(lane == t, idx, out)
                gate = jnp.where(rows == idx, -jnp.inf, gate)
            idx_ref[0, h:h + 1, :] = out


def _paged_attn_kernel(pt_ref, top_ref, q_ref, kn_ref, vn_ref, *refs, n_pages):
    k_refs = refs[:n_pages]
    v_refs = refs[n_pages:2 * n_pages]
    o_ref = refs[2 * n_pages]
    q = q_ref[0]
    qh = jnp.broadcast_to(q, (SUBLANES, HEAD_DIM)).astype(BF16)
    s_own = jnp.sum(q * kn_ref[0], axis=-1, keepdims=True) * ATTN_SCALE
    ss = [lax.dot_general(qh, kr[0].astype(BF16), NT_DIMS, preferred_element_type=F32) * ATTN_SCALE
          for kr in k_refs]
    m = s_own
    for s in ss:
        m = jnp.maximum(m, jnp.max(s[0:1], axis=-1, keepdims=True))
    p_own = jnp.exp(s_own - m)
    l = p_own
    acc = p_own * vn_ref[0]
    for s, vr in zip(ss, v_refs):
        p = jnp.exp(s - m)
        l = l + jnp.sum(p[0:1], axis=-1, keepdims=True)
        acc = acc + jnp.dot(p.astype(BF16), vr[0].astype(BF16), preferred_element_type=F32)[0:1]
    o_ref[0] = (acc / l).astype(o_ref.dtype)


def _paged_moba(q, k_new, v_new, k_pool, v_pool, page_table):
    Bs = q.shape[0]
    n_pages = page_table.shape[1]
    ppb = MOBA_BLOCK // PAGE_SIZE
    assert ppb == 2 and n_pages % ppb == 0
    nblk = n_pages // ppb
    assert nblk >= MOBA_TOPK
    q3, k3, v3 = (a.reshape(Bs, 1, A_W) for a in (q, k_new, v_new))
    idx = pl.pallas_call(
        functools.partial(_paged_topk_kernel, nblk=nblk),
        grid_spec=pltpu.PrefetchScalarGridSpec(
            num_scalar_prefetch=1, grid=(Bs, nblk),
            in_specs=[pl.BlockSpec((1, PAGE_SIZE, A_W), lambda b, n, pt: (pt[b, 2 * n], 0, 0)),
                      pl.BlockSpec((1, PAGE_SIZE, A_W), lambda b, n, pt: (pt[b, 2 * n + 1], 0, 0)),
                      pl.BlockSpec((1, 1, A_W), lambda b, n, pt: (b, 0, 0))],
            out_specs=pl.BlockSpec((1, A_HEADS, LANES), lambda b, n, pt: (b, 0, 0)),
            scratch_shapes=[pltpu.VMEM((nblk, A_W), F32)]),
        out_shape=jax.ShapeDtypeStruct((Bs, A_HEADS, LANES), jnp.int32),
        compiler_params=_cparams(("parallel", "arbitrary")), name="paged_topk")(page_table, k_pool, k_pool, q3)
    top = idx[:, :, :MOBA_TOPK].reshape(-1)

    n_sel_pages = MOBA_TOPK * ppb

    def page_spec(sel, r):
        def imap(b, h, pt, tp):
            blk = tp[(b * A_HEADS + h) * MOBA_TOPK + sel]
            return (pt[b, blk * ppb + r], 0, h)
        return pl.BlockSpec((1, PAGE_SIZE, HEAD_DIM), imap)

    page_specs = [page_spec(s, r) for s in range(MOBA_TOPK) for r in range(ppb)]
    head_spec = pl.BlockSpec((1, 1, HEAD_DIM), lambda b, h, pt, tp: (b, 0, h))
    att = pl.pallas_call(
        functools.partial(_paged_attn_kernel, n_pages=n_sel_pages),
        grid_spec=pltpu.PrefetchScalarGridSpec(
            num_scalar_prefetch=2, grid=(Bs, A_HEADS),
            in_specs=[head_spec, head_spec, head_spec] + page_specs + page_specs,
            out_specs=head_spec),
        out_shape=jax.ShapeDtypeStruct((Bs, 1, A_W), BF16),
        compiler_params=_cparams(("parallel", "parallel")), name="paged_attn")(
            page_table, top, q3, k3, v3, *([k_pool] * n_sel_pages), *([v_pool] * n_sel_pages))
    return att.reshape(Bs, A_W)


def _rope_tables(pos):
    inv = ROPE_THETA ** (-jnp.arange(ROT_HALF, dtype=F32) / ROT_HALF)
    ang = pos.astype(F32)[:, None] * inv
    cos, sin = jnp.cos(ang), jnp.sin(ang)
    T = pos.shape[0]
    rest = HEAD_DIM - ROT_DIM
    cos_t = jnp.concatenate([cos, cos, jnp.ones((T, rest), F32)], axis=-1)
    sin_t = jnp.concatenate([-sin, sin, jnp.zeros((T, rest), F32)], axis=-1)
    return cos_t, sin_t


def _row_tile(m):
    return 512 if m % 512 == 0 else m


def kernel(x_prompt, x_sample, cache_k, cache_v, state_lru_conv, state_lru_h, state_ssd_conv, state_ssd_h, page_table, norm_mix, norm_mlp, w_in_a, q_norm, k_norm, lru_conv_w, lru_conv_b, lru_wa, lru_ba, lru_wx, lru_bx, lru_lambda, w_out_a, w_in_s, ssd_conv_w, ssd_conv_b, ssd_dt_bias, ssd_a_log, ssd_d, ssd_norm, w_out_s, w_up, w_down):
    Bp, Tp, D = x_prompt.shape
    Bs, Ts, _ = x_sample.shape
    assert Ts == 1
    depth = norm_mix.shape[0]
    past_len = page_table.shape[1] * PAGE_SIZE
    Mp, Ms = Bp * Tp, Bs * Ts
    tmp, tms = _row_tile(Mp), _row_tile(Ms)
    assert Tp % tmp == 0
    xp = x_prompt.reshape(Mp, D)
    xs = x_sample.reshape(Ms, D)
    lru_w = lru_lambda.shape[1]
    d_inner = ssd_norm.shape[1]
    xbc_w = ssd_conv_w.shape[2]
    n_ssd_heads = ssd_d.shape[1]

    cos_p, sin_p = _rope_tables(jnp.arange(Tp, dtype=jnp.int32))
    cos_s, sin_s = _rope_tables(jnp.full((Ms,), past_len, dtype=jnp.int32))

    outs = {k: [] for k in ("kp", "vp", "ks", "vs", "lcp", "lcs", "lhp", "lhs", "scp", "scs", "shp", "shs")}
    for layer in range(depth):
        i = layer // 2
        if layer % 2 == 0:
            w_in = w_in_a[i].astype(BF16)
            w_out = w_out_a[i].astype(BF16)
            splits = [A_W, A_W, A_W, lru_w, lru_w]
            lru_args = (lru_conv_w[i], lru_conv_b[i], lru_wa[i], lru_ba[i], lru_wx[i], lru_bx[i], lru_lambda[i])
            q, k, v, xr, yg = _norm_proj(xp, norm_mix[layer], w_in, splits, tm=tmp,
                                         rope=((q_norm[i], k_norm[i]), cos_p, sin_p, Tp // tmp))
            att = _moba_prompt(q, k, v, batch=Bp, seq=Tp)
            rec, h_last = _lru_prompt(xr, yg, *lru_args, batch=Bp, seq=Tp)
            mixed_p = _out_proj([att, rec], w_out, xp, tm=tmp)
            outs["kp"].append(k.reshape(Bp, Tp, A_HEADS, HEAD_DIM))
            outs["vp"].append(v.reshape(Bp, Tp, A_HEADS, HEAD_DIM))
            outs["lcp"].append(xr.reshape(Bp, Tp, lru_w)[:, Tp - (CONV_K - 1):])
            outs["lhp"].append(h_last.reshape(Bp, lru_w))
            q, k, v, xr, yg = _norm_proj(xs, norm_mix[layer], w_in, splits, tm=tms,
                                         rope=((q_norm[i], k_norm[i]), cos_s, sin_s, 1))
            n_phys = cache_k.shape[1]
            att = _paged_moba(q, k, v, cache_k[i].reshape(n_phys, PAGE_SIZE, A_W),
                              cache_v[i].reshape(n_phys, PAGE_SIZE, A_W), page_table)
            buf = state_lru_conv[i]
            rec, h_new = _lru_sample(xr, yg, buf.swapaxes(0, 1), state_lru_h[i], *lru_args)
            mixed_s = _out_proj([att, rec], w_out, xs, tm=tms)
            outs["ks"].append(k.reshape(Bs, Ts, A_HEADS, HEAD_DIM))
            outs["vs"].append(v.reshape(Bs, Ts, A_HEADS, HEAD_DIM))
            outs["lcs"].append(jnp.concatenate([buf[:, 1:], xr[:, None, :]], axis=1))
            outs["lhs"].append(h_new)
        else:
            main_w = d_inner + xbc_w
            w_in = w_in_s[i][:, :main_w].astype(BF16)
            w_dt = jnp.pad(w_in_s[i][:, main_w:], ((0, 0), (0, LANES - n_ssd_heads))).astype(BF16)
            w_out = w_out_s[i].astype(BF16)
            ssd_args = (ssd_conv_w[i], ssd_conv_b[i], ssd_dt_bias[i], ssd_a_log[i], ssd_d[i], ssd_norm[i])
            z, xbc, dt = _norm_proj(xp, norm_mix[layer], w_in, [d_inner, xbc_w], tm=tmp, extra_w=w_dt)
            yn, s_last = _ssd_prompt(xbc, z, dt, *ssd_args, batch=Bp, seq=Tp)
            mixed_p = _out_proj([yn], w_out, xp, tm=tmp)
            outs["scp"].append(xbc.reshape(Bp, Tp, xbc_w)[:, Tp - (CONV_K - 1):])
            outs["shp"].append(s_last.reshape(Bp, n_ssd_heads, SSD_HEAD_DIM, SSD_STATE))
            z, xbc, dt = _norm_proj(xs, norm_mix[layer], w_in, [d_inner, xbc_w], tm=tms, extra_w=w_dt)
            buf = state_ssd_conv[i]
            yn, s_new = _ssd_sample(xbc, buf.swapaxes(0, 1), z, dt, state_ssd_h[i], *ssd_args)
            mixed_s = _out_proj([yn], w_out, xs, tm=tms)
            outs["scs"].append(jnp.concatenate([buf[:, 1:], xbc[:, None, :]], axis=1))
            outs["shs"].append(s_new)
        wu, wd = w_up[layer].astype(BF16), w_down[layer].astype(BF16)
        xp = _mlp(mixed_p, norm_mlp[layer], wu, wd, tm=tmp)
        xs = _mlp(mixed_s, norm_mlp[layer], wu, wd, tm=tms)

    st = lambda key: jnp.stack(outs[key])
    return (xp.reshape(Bp, Tp, D), xs.reshape(Bs, Ts, D), st("kp"), st("vp"), st("ks"), st("vs"),
            st("lcp"), st("lcs"), st("lhp"), st("lhs"), st("scp"), st("scs"), st("shp"), st("shs"))
```

```python
import functools
import math

import jax
import jax.numpy as jnp
from jax import lax
from jax.experimental import pallas as pl
from jax.experimental.pallas import tpu as pltpu

F32 = jnp.float32
BF16 = jnp.bfloat16
HIGHEST = lax.Precision.HIGHEST

A_HEADS = 8
HEAD_DIM = 128
A_W = A_HEADS * HEAD_DIM
ROT_DIM = HEAD_DIM // 4
ROT_HALF = ROT_DIM // 2
ROPE_THETA = 500000.0
MOBA_BLOCK = 256
MOBA_TOPK = 3
ATTN_SCALE = HEAD_DIM ** -0.5
LRU_HEADS = 8
LRU_C = 8.0
CONV_K = 4
SSD_HEAD_DIM = 64
SSD_STATE = 128
SSD_GROUPS = 8
SSD_CHUNK = 128
PAGE_SIZE = 128
EPS = 1e-6

LANES = 128
SUBLANES = 8
NEG_BIG = -1e30
LOG2E = math.log2(math.e)
VMEM_BYTES_V7X = 64 * 1024 * 1024
VMEM_LIMIT = VMEM_BYTES_V7X - 8 * 1024 * 1024
VMEM_LIMIT_MLP = VMEM_BYTES_V7X - 4 * 1024 * 1024

NT_DIMS = (((1,), (1,)), ((), ()))


def _cparams(sem, vmem=VMEM_LIMIT):
    return pltpu.CompilerParams(dimension_semantics=sem, vmem_limit_bytes=vmem)


def _rms_rows(x, g):
    return x * lax.rsqrt(jnp.mean(x * x, axis=-1, keepdims=True) + EPS) * g


def _head_norm_rope(acc, g, cos, sin):
    y = _rms_rows(acc, g)
    lane = lax.broadcasted_iota(jnp.int32, y.shape, 1)
    partner = jnp.where(lane < ROT_HALF, pltpu.roll(y, LANES - ROT_HALF, 1), pltpu.roll(y, ROT_HALF, 1))
    return y * cos + partner * sin


def _norm_proj_kernel(*refs, ranges, n_rope, has_extra, tn, slab):
    it = iter(refs)
    x_ref, g_ref, w_ref = next(it), next(it), next(it)
    rope_g = [next(it) for _ in range(n_rope)]
    if n_rope:
        cos_ref, sin_ref = next(it), next(it)
    if has_extra:
        we_ref = next(it)
    outs = [next(it) for _ in ranges]
    if has_extra:
        oe_ref = next(it)
    xn_ref = next(it)
    j = pl.program_id(1)

    @pl.when(j == 0)
    def _():
        xn_ref[...] = _rms_rows(x_ref[...], g_ref[...]).astype(BF16)
        if has_extra:
            oe_ref[...] = jnp.dot(xn_ref[...], we_ref[...], preferred_element_type=F32)

    acc = jnp.dot(xn_ref[...], w_ref[...], preferred_element_type=F32)
    for s, (a, b) in enumerate(ranges):
        @pl.when((j >= a) & (j < b))
        def _(s=s):
            if s < n_rope:
                cos, sin = cos_ref[...], sin_ref[...]
                for hh in range(tn // HEAD_DIM):
                    sl = slice(hh * HEAD_DIM, (hh + 1) * HEAD_DIM)
                    outs[s][:, sl] = _head_norm_rope(acc[:, sl], rope_g[s][...], cos, sin)
            elif slab:
                for cblk in range(tn // LANES):
                    outs[s][cblk] = acc[:, cblk * LANES:(cblk + 1) * LANES]
            else:
                outs[s][...] = acc


def _norm_proj(x, g, w, splits, *, tm, tn=512, rope=None, extra_w=None, slab=False):
    M, K = x.shape
    N = sum(splits)
    assert w.shape == (K, N) and M % tm == 0 and all(s % tn == 0 for s in splits)
    edges = [0]
    for s in splits:
        edges.append(edges[-1] + s // tn)
    ranges = tuple((edges[i], edges[i + 1]) for i in range(len(splits)))
    n_rope = 0 if rope is None else len(rope[0])
    in_specs = [pl.BlockSpec((tm, K), lambda i, j: (i, 0)),
                pl.BlockSpec((1, K), lambda i, j: (0, 0)),
                pl.BlockSpec((K, tn), lambda i, j: (0, j))]
    args = [x, g.reshape(1, K), w]
    if rope is not None:
        gains, cos, sin, period = rope
        for gg in gains:
            in_specs.append(pl.BlockSpec((1, HEAD_DIM), lambda i, j: (0, 0)))
            args.append(gg.reshape(1, HEAD_DIM))
        for tab in (cos, sin):
            in_specs.append(pl.BlockSpec((tm, HEAD_DIM), lambda i, j, period=period: (i % period, 0)))
            args.append(tab)
    if extra_w is not None:
        in_specs.append(pl.BlockSpec((K, LANES), lambda i, j: (0, 0)))
        args.append(extra_w)
    out_specs, out_shape = [], []
    for (a, b), s in zip(ranges, splits):
        if slab:
            out_specs.append(pl.BlockSpec((tn // LANES, tm, LANES),
                                          lambda i, j, a=a, b=b: (jnp.clip(j - a, 0, b - a - 1), i, 0)))
            out_shape.append(jax.ShapeDtypeStruct((s // LANES, M, LANES), F32))
        else:
            out_specs.append(pl.BlockSpec((tm, tn), lambda i, j, a=a, b=b: (i, jnp.clip(j - a, 0, b - a - 1))))
            out_shape.append(jax.ShapeDtypeStruct((M, s), F32))
    if extra_w is not None:
        out_specs.append(pl.BlockSpec((tm, LANES), lambda i, j: (i, 0)))
        out_shape.append(jax.ShapeDtypeStruct((M, LANES), F32))
    kern = functools.partial(_norm_proj_kernel, ranges=ranges, n_rope=n_rope,
                             has_extra=extra_w is not None, tn=tn, slab=slab)
    return pl.pallas_call(
        kern, grid=(M // tm, N // tn), in_specs=in_specs, out_specs=out_specs, out_shape=out_shape,
        scratch_shapes=[pltpu.VMEM((tm, K), BF16)],
        compiler_params=_cparams(("parallel", "arbitrary")), name="norm_proj")(*args)


def _out_proj_kernel(*refs, n_parts):
    xs = refs[:n_parts]
    ws = refs[n_parts:2 * n_parts]
    res_ref, o_ref = refs[2 * n_parts], refs[2 * n_parts + 1]
    acc = res_ref[...]
    for x_ref, w_ref in zip(xs, ws):
        acc = acc + jnp.dot(x_ref[...], w_ref[...], preferred_element_type=F32)
    o_ref[...] = acc


def _out_proj(parts, w, res, *, tm, tn=512):
    M, N = res.shape
    kp = parts[0].shape[1]
    assert all(p.shape == (M, kp) for p in parts) and w.shape == (kp * len(parts), N)
    in_specs = [pl.BlockSpec((tm, kp), lambda i, j: (i, 0)) for _ in parts]
    in_specs += [pl.BlockSpec((kp, tn), lambda i, j, p=p: (p, j)) for p in range(len(parts))]
    in_specs.append(pl.BlockSpec((tm, tn), lambda i, j: (i, j)))
    return pl.pallas_call(
        functools.partial(_out_proj_kernel, n_parts=len(parts)),
        grid=(M // tm, N // tn), in_specs=in_specs,
        out_specs=pl.BlockSpec((tm, tn), lambda i, j: (i, j)),
        out_shape=jax.ShapeDtypeStruct((M, N), F32),
        compiler_params=_cparams(("parallel", "parallel")), name="out_proj")(*parts, *([w] * len(parts)), res)


def _mlp_kernel(x_ref, g_ref, wu_ref, wd_ref, o_ref, xn_ref):
    f = pl.program_id(1)

    @pl.when(f == 0)
    def _():
        x = x_ref[...]
        xn_ref[...] = _rms_rows(x, g_ref[...]).astype(BF16)
        o_ref[...] = x

    h = jnp.maximum(jnp.dot(xn_ref[...], wu_ref[...], preferred_element_type=F32), 0.0)
    o_ref[...] += jnp.dot((h * h).astype(BF16), wd_ref[...], preferred_element_type=F32)


def _mlp(x, g, w_up, w_down, *, tm, tf=1024):
    M, D = x.shape
    FF = w_up.shape[1]
    return pl.pallas_call(
        _mlp_kernel, grid=(M // tm, FF // tf),
        in_specs=[pl.BlockSpec((tm, D), lambda i, f: (i, 0)),
                  pl.BlockSpec((1, D), lambda i, f: (0, 0)),
                  pl.BlockSpec((D, tf), lambda i, f: (0, f)),
                  pl.BlockSpec((tf, D), lambda i, f: (f, 0))],
        out_specs=pl.BlockSpec((tm, D), lambda i, f: (i, 0)),
        out_shape=jax.ShapeDtypeStruct((M, D), F32),
        scratch_shapes=[pltpu.VMEM((tm, D), BF16)],
        compiler_params=_cparams(("parallel", "arbitrary"), VMEM_LIMIT_MLP), name="mlp")(
            x, g.reshape(1, D), w_up, w_down)


def _moba_prompt_kernel(q_ref, k_ref, v_ref, hot_ref, o_ref, kaug_ref, vb_ref, kmean_ref, *, nb):
    blk, D = MOBA_BLOCK, HEAD_DIM
    kaug_ref[:, 0:D] = k_ref[...].astype(BF16)
    kaug_ref[:, D:2 * D] = hot_ref[...]
    vb_ref[...] = v_ref[...].astype(BF16)
    kmean_ref[...] = jnp.zeros(kmean_ref.shape, F32)
    for n in range(nb):
        kmean_ref[n:n + 1, :] = jnp.mean(k_ref[n * blk:(n + 1) * blk, :], axis=0, keepdims=True)

    lane = lax.broadcasted_iota(jnp.int32, (blk, LANES), 1)
    row = lax.broadcasted_iota(jnp.int32, (blk, blk), 0)
    col = lax.broadcasted_iota(jnp.int32, (blk, blk), 1)
    nbp = kmean_ref.shape[0]
    blk_id = lax.broadcasted_iota(jnp.int32, (nbp, blk), 0)
    eye = (row == col).astype(BF16)
    for qb in range(nb):
        q = q_ref[qb * blk:(qb + 1) * blk, :]
        if qb > MOBA_TOPK:
            gate = lax.dot_general(kmean_ref[...], q, NT_DIMS, precision=HIGHEST, preferred_element_type=F32)
            gate = jnp.where(blk_id < qb, gate, -jnp.inf)
            bias_t = jnp.where(blk_id == qb, 0.0, NEG_BIG)
            for _ in range(MOBA_TOPK):
                m = jnp.max(gate, axis=0, keepdims=True)
                idx = jnp.min(jnp.where(gate == m, blk_id, nbp), axis=0, keepdims=True)
                hit = blk_id == idx
                bias_t = jnp.where(hit & (m > -jnp.inf), 0.0, bias_t)
                gate = jnp.where(hit, -jnp.inf, gate)
            pad = jnp.zeros((LANES - nbp, blk), F32)
            bias = lax.dot_general(eye, jnp.concatenate([bias_t, pad], axis=0).astype(BF16), NT_DIMS,
                                   preferred_element_type=F32).astype(BF16)
        else:
            bias = jnp.where(lane <= qb, 0.0, NEG_BIG).astype(BF16)
        lhs = jnp.concatenate([(q * (ATTN_SCALE * LOG2E)).astype(BF16), bias], axis=1)
        own = slice(qb * blk, (qb + 1) * blk)
        s_own = lax.dot_general(lhs, kaug_ref[own, :], NT_DIMS, preferred_element_type=F32)
        s_own = jnp.where(col <= row, s_own, NEG_BIG)
        m = jnp.max(s_own, axis=-1, keepdims=True)
        if qb > 0:
            s_past = lax.dot_general(lhs, kaug_ref[0:qb * blk, :], NT_DIMS, preferred_element_type=F32)
            m = jnp.maximum(m, jnp.max(s_past, axis=-1, keepdims=True))
        p = jnp.exp2(s_own - m)
        l = jnp.sum(p, axis=-1, keepdims=True)
        acc = jnp.dot(p.astype(BF16), vb_ref[own, :], preferred_element_type=F32)
        if qb > 0:
            p = jnp.exp2(s_past - m)
            l = l + jnp.sum(p, axis=-1, keepdims=True)
            acc = acc + jnp.dot(p.astype(BF16), vb_ref[0:qb * blk, :], preferred_element_type=F32)
        o_ref[own, :] = (acc / l).astype(o_ref.dtype)


def _moba_prompt(q, k, v, *, batch, seq):
    M = q.shape[0]
    nb = seq // MOBA_BLOCK
    assert nb <= LANES and seq % MOBA_BLOCK == 0
    hot = (lax.broadcasted_iota(jnp.int32, (seq, LANES), 0) // MOBA_BLOCK
           == lax.broadcasted_iota(jnp.int32, (seq, LANES), 1)).astype(BF16)
    seq_head = pl.BlockSpec((seq, HEAD_DIM), lambda b, h: (b, h))
    return pl.pallas_call(
        functools.partial(_moba_prompt_kernel, nb=nb),
        grid=(batch, A_HEADS),
        in_specs=[seq_head, seq_head, seq_head, pl.BlockSpec((seq, LANES), lambda b, h: (0, 0))],
        out_specs=seq_head,
        out_shape=jax.ShapeDtypeStruct((M, A_W), BF16),
        scratch_shapes=[pltpu.VMEM((seq, 2 * HEAD_DIM), BF16), pltpu.VMEM((seq, HEAD_DIM), BF16),
                        pltpu.VMEM((-(-nb // SUBLANES) * SUBLANES, HEAD_DIM), F32)],
        compiler_params=_cparams(("parallel", "parallel")), name="moba_prompt")(q, k, v, hot)


def _lru_gates(xc, wa_ref, ba, wx_ref, bx, lam):
    bw = xc.shape[1] // LRU_HEADS
    rs, is_ = [], []
    for n in range(LRU_HEADS):
        xb = xc[:, n * bw:(n + 1) * bw].astype(BF16)
        rs.append(jnp.dot(xb, wa_ref[n], preferred_element_type=F32))
        is_.append(jnp.dot(xb, wx_ref[n], preferred_element_type=F32))
    r = jax.nn.sigmoid(jnp.concatenate(rs, axis=-1) + ba)
    i = jax.nn.sigmoid(jnp.concatenate(is_, axis=-1) + bx)
    log_a = -LRU_C * r * jax.nn.softplus(-lam)
    a = jnp.exp(log_a)
    t = jnp.tanh(log_a)
    u = jnp.sqrt(-2.0 * t / (1.0 - t)) * (i * xc)
    return a, u


def _lru_prompt_kernel(xr_ref, yg_ref, cw_ref, cb_ref, wa_ref, ba_ref, wx_ref, bx_ref, lam_ref,
                       rec_ref, hl_ref, xx_ref, a_ref, u_ref, h_ref, *, tt):
    ti = pl.program_id(1)
    pad = SUBLANES

    @pl.when(ti == 0)
    def _():
        xx_ref[0:pad, :] = jnp.zeros((pad, xx_ref.shape[1]), F32)
        h_ref[...] = jnp.zeros(h_ref.shape, F32)

    @pl.when(ti > 0)
    def _():
        xx_ref[0:pad, :] = xx_ref[tt:tt + pad, :]

    xx_ref[pad:pad + tt, :] = xr_ref[...]
    xc = cb_ref[...]
    for j in range(CONV_K):
        off = pad - (CONV_K - 1) + j
        xc = xc + xx_ref[off:off + tt, :] * cw_ref[j:j + 1, :]
    a, u = _lru_gates(xc, wa_ref, ba_ref[...], wx_ref, bx_ref[...], lam_ref[...])
    a_ref[...] = a
    u_ref[...] = u

    sub = lax.broadcasted_iota(jnp.int32, (SUBLANES, a.shape[1]), 0)

    def group(gi, h):
        r0 = pl.multiple_of(gi * SUBLANES, SUBLANES)
        a8 = a_ref[pl.ds(r0, SUBLANES), :]
        u8 = u_ref[pl.ds(r0, SUBLANES), :]
        for s in (1, 2, 4):
            keep = sub >= s
            a_sh = pltpu.roll(a8, s, 0)
            u_sh = pltpu.roll(u8, s, 0)
            u8 = jnp.where(keep, a8 * u_sh + u8, u8)
            a8 = jnp.where(keep, a8 * a_sh, a8)
        hs = a8 * h + u8
        yg = yg_ref[pl.ds(r0, SUBLANES), :]
        rec_ref[pl.ds(r0, SUBLANES), :] = (hs * jax.nn.gelu(yg)).astype(rec_ref.dtype)
        return hs[SUBLANES - 1:SUBLANES, :]

    h = lax.fori_loop(0, tt // SUBLANES, group, h_ref[...])
    h_ref[...] = h

    @pl.when(ti == pl.num_programs(1) - 1)
    def _():
        hl_ref[0] = h


def _lru_prompt(xr, yg, cw, cb, wa, ba, wx, bx, lam, *, batch, seq, tt=256):
    M, W = xr.shape
    nt = seq // tt
    row = lambda a: a.reshape(1, W)
    full2 = lambda shp: pl.BlockSpec(shp, lambda b, t: (0, 0))
    full3 = lambda shp: pl.BlockSpec(shp, lambda b, t: (0, 0, 0))
    return pl.pallas_call(
        functools.partial(_lru_prompt_kernel, tt=tt),
        grid=(batch, nt),
        in_specs=[pl.BlockSpec((tt, W), lambda b, t: (b * nt + t, 0)),
                  pl.BlockSpec((tt, W), lambda b, t: (b * nt + t, 0)),
                  full2((CONV_K, W)), full2((1, W)), full3(wa.shape), full2((1, W)),
                  full3(wx.shape), full2((1, W)), full2((1, W))],
        out_specs=[pl.BlockSpec((tt, W), lambda b, t: (b * nt + t, 0)),
                   pl.BlockSpec((1, 1, W), lambda b, t: (b, 0, 0))],
        out_shape=[jax.ShapeDtypeStruct((M, W), BF16), jax.ShapeDtypeStruct((batch, 1, W), F32)],
        scratch_shapes=[pltpu.VMEM((tt + SUBLANES, W), F32), pltpu.VMEM((tt, W), F32),
                        pltpu.VMEM((tt, W), F32), pltpu.VMEM((1, W), F32)],
        compiler_params=_cparams(("parallel", "arbitrary")), name="lru_prompt")(
            xr, yg, cw, row(cb), wa.astype(BF16), row(ba), wx.astype(BF16), row(bx), row(lam))


def _lru_sample_kernel(xr_ref, yg_ref, buf_ref, h0_ref, cw_ref, cb_ref, wa_ref, ba_ref, wx_ref, bx_ref,
                       lam_ref, rec_ref, hn_ref):
    xc = cb_ref[...]
    for j in range(CONV_K - 1):
        xc = xc + buf_ref[j] * cw_ref[j:j + 1, :]
    xc = xc + xr_ref[...] * cw_ref[CONV_K - 1:CONV_K, :]
    a, u = _lru_gates(xc, wa_ref, ba_ref[...], wx_ref, bx_ref[...], lam_ref[...])
    h = a * h0_ref[...] + u
    hn_ref[...] = h
    rec_ref[...] = (h * jax.nn.gelu(yg_ref[...])).astype(rec_ref.dtype)


def _lru_sample(xr, yg, buf_t, h0, cw, cb, wa, ba, wx, bx, lam):
    Bs, W = xr.shape
    row = lambda a: a.reshape(1, W)
    return pl.pallas_call(
        _lru_sample_kernel,
        out_shape=[jax.ShapeDtypeStruct((Bs, W), BF16), jax.ShapeDtypeStruct((Bs, W), F32)],
        compiler_params=pltpu.CompilerParams(vmem_limit_bytes=VMEM_LIMIT), name="lru_sample")(
            xr, yg, buf_t, h0, cw, row(cb), wa.astype(BF16), row(ba), wx.astype(BF16), row(bx), row(lam))


def _ssd_prompt_kernel(xbc_ref, prev_ref, z_ref, dt_ref, cw_ref, cb_ref, dtb_ref, alog_ref, dsk_ref, ng_ref,
                       ex_ref, yn_ref, sl_ref, xx_ref, xcv_ref, y_ref, st_ref, *, d_inner):
    c = pl.program_id(1)
    L = SSD_CHUNK
    pad = SUBLANES
    gw = d_inner // SSD_GROUPS
    spg = gw // LANES
    hps = LANES // SSD_HEAD_DIM
    head_of_lane = lax.broadcasted_iota(jnp.int32, (L, LANES), 1) // SSD_HEAD_DIM
    assert SSD_STATE == LANES

    @pl.when(c == 0)
    def _():
        xx_ref[:, 0:pad, :] = jnp.zeros((xx_ref.shape[0], pad, LANES), F32)
        st_ref[...] = jnp.zeros(st_ref.shape, F32)

    @pl.when(c > 0)
    def _():
        xx_ref[:, 0:pad, :] = prev_ref[...]

    xx_ref[:, pad:pad + L, :] = xbc_ref[...]

    def conv_slab(k, carry):
        xs = xx_ref[k]
        xc = cb_ref[k]
        for j in range(CONV_K):
            back = CONV_K - 1 - j
            tap = xs if back == 0 else pltpu.roll(xs, back, 0)
            xc = xc + tap[pad:] * cw_ref[k, j:j + 1, :]
        xcv_ref[k] = xc * jax.nn.sigmoid(xc)
        return carry

    lax.fori_loop(0, xx_ref.shape[0], conv_slab, 0)

    def slabs(ref, first, count):
        return jnp.concatenate([ref[first + i] for i in range(count)], axis=1)

    def split2(a):
        hi = a.astype(BF16)
        return hi, (a - hi.astype(F32)).astype(BF16)

    dt = jax.nn.softplus(dt_ref[...] + dtb_ref[...])
    dta = dt * (-jnp.exp(alog_ref[...]))
    tril = lax.broadcasted_iota(jnp.int32, (L, L), 1) <= lax.broadcasted_iota(jnp.int32, (L, L), 0)
    cs = jnp.dot(tril.astype(F32), dta, precision=HIGHEST, preferred_element_type=F32)
    cs_t = cs.T
    dt_t = dt.T
    last = cs[L - 1:L, :]
    w_hi, w_lo = split2(dt * jnp.exp(last - cs))
    e_hi, e_lo = split2(jnp.exp(cs))
    cdec = jnp.broadcast_to(jnp.exp(last), (SUBLANES, LANES))
    cd_hi, cd_rest = split2(cdec)
    cd_mid, cd_lo = split2(cdec - cd_hi.astype(F32))

    def expand(terms, exg):
        out = jnp.dot(terms[0], exg, preferred_element_type=F32)
        for t in terms[1:]:
            out = out + jnp.dot(t, exg, preferred_element_type=F32)
        return out

    for g in range(SSD_GROUPS):
        lo = g * gw
        exg = ex_ref[:, lo:lo + gw]
        w_x = expand((w_hi, w_lo), exg)
        ecs_x = expand((e_hi, e_lo), exg)
        cd_x = expand((cd_hi, cd_mid, cd_lo), exg)[0:1]
        xg = slabs(xcv_ref, g * spg, spg)
        bg = xcv_ref[d_inner // LANES + g]
        cg = xcv_ref[d_inner // LANES + SSD_GROUPS + g].astype(BF16)
        bg_t = bg.T.astype(BF16)
        cb = jnp.dot(cg, bg_t, preferred_element_type=F32)
        st_g = st_ref[:, lo:lo + gw]
        y_off = jnp.dot(cg, st_g.astype(BF16), preferred_element_type=F32) * ecs_x
        st_ref[:, lo:lo + gw] = st_g * cd_x + jnp.dot(bg_t, (xg * w_x).astype(BF16), preferred_element_type=F32)
        for i in range(spg):
            xs_h = xcv_ref[g * spg + i].astype(BF16)
            lhs, rhs = [], []
            for k in range(hps):
                h = (g * spg + i) * hps + k
                dec = jnp.exp(jnp.where(tril, cs[:, h:h + 1] - cs_t[h:h + 1, :], NEG_BIG))
                lhs.append((cb * dec * dt_t[h:h + 1, :]).astype(BF16))
                rhs.append(jnp.where(head_of_lane == k, xs_h, jnp.zeros_like(xs_h)))
            yd = jnp.dot(jnp.concatenate(lhs, axis=1), jnp.concatenate(rhs, axis=0), preferred_element_type=F32)
            cols = slice(lo + i * LANES, lo + (i + 1) * LANES)
            y_ref[:, cols] = yd + y_off[:, i * LANES:(i + 1) * LANES]

    for g in range(SSD_GROUPS):
        lo = g * gw
        cols = slice(lo, lo + gw)
        y = y_ref[:, cols] + slabs(xcv_ref, g * spg, spg) * dsk_ref[:, cols]
        zz = slabs(z_ref, g * spg, spg)
        u = y * (zz * jax.nn.sigmoid(zz))
        yn_ref[:, cols] = _rms_rows(u, ng_ref[:, cols]).astype(yn_ref.dtype)

    @pl.when(c == pl.num_programs(1) - 1)
    def _():
        for k in range(d_inner // LANES):
            sl_ref[0, k * LANES:(k + 1) * LANES, :] = st_ref[:, k * LANES:(k + 1) * LANES].T


def _ssd_expand_matrix(n_heads, d_inner):
    h = lax.broadcasted_iota(jnp.int32, (LANES, d_inner), 0)
    col = lax.broadcasted_iota(jnp.int32, (LANES, d_inner), 1)
    return (col // SSD_HEAD_DIM == h).astype(F32) * (h < n_heads).astype(F32)


def _pad_lanes(v):
    return jnp.pad(v, (0, LANES - v.shape[0])).reshape(1, LANES)


def _ssd_prompt(xbc, z, dt, cw, cb, dt_bias, a_log, d_skip, norm_g, *, batch, seq):
    nxs, M, _ = xbc.shape
    nzs = z.shape[0]
    XW, d_inner = nxs * LANES, nzs * LANES
    n_heads = d_inner // SSD_HEAD_DIM
    L = SSD_CHUNK
    nc = seq // L
    rpb = L // SUBLANES
    const2 = lambda shp: pl.BlockSpec(shp, lambda b, c: (0, 0))
    const3 = lambda shp: pl.BlockSpec(shp, lambda b, c: (0, 0, 0))
    return pl.pallas_call(
        functools.partial(_ssd_prompt_kernel, d_inner=d_inner),
        grid=(batch, nc),
        in_specs=[pl.BlockSpec((nxs, L, LANES), lambda b, c: (0, b * nc + c, 0)),
                  pl.BlockSpec((nxs, SUBLANES, LANES), lambda b, c: (0, jnp.maximum((b * nc + c) * rpb - 1, 0), 0)),
                  pl.BlockSpec((nzs, L, LANES), lambda b, c: (0, b * nc + c, 0)),
                  pl.BlockSpec((L, LANES), lambda b, c: (b * nc + c, 0)),
                  const3((nxs, CONV_K, LANES)), const3((nxs, 1, LANES)), const2((1, LANES)), const2((1, LANES)),
                  const2((1, d_inner)), const2((1, d_inner)), const2((LANES, d_inner))],
        out_specs=[pl.BlockSpec((L, d_inner), lambda b, c: (b * nc + c, 0)),
                   pl.BlockSpec((1, d_inner, SSD_STATE), lambda b, c: (b, 0, 0))],
        out_shape=[jax.ShapeDtypeStruct((M, d_inner), BF16),
                   jax.ShapeDtypeStruct((batch, d_inner, SSD_STATE), F32)],
        scratch_shapes=[pltpu.VMEM((nxs, L + SUBLANES, LANES), F32), pltpu.VMEM((nxs, L, LANES), F32),
                        pltpu.VMEM((L, d_inner), F32), pltpu.VMEM((SSD_STATE, d_inner), F32)],
        compiler_params=_cparams(("parallel", "arbitrary")), name="ssd_prompt")(
            xbc, xbc, z, dt, cw.reshape(CONV_K, nxs, LANES).swapaxes(0, 1), cb.reshape(nxs, 1, LANES),
            _pad_lanes(dt_bias), _pad_lanes(a_log),
            jnp.repeat(d_skip, SSD_HEAD_DIM).reshape(1, d_inner), norm_g.reshape(1, d_inner),
            _ssd_expand_matrix(n_heads, d_inner).astype(BF16))


def _ssd_sample_pre_kernel(x_ref, buf_ref, dt_ref, cw_ref, cb_ref, dtb_ref, alog_ref, ex_ref,
                           xcs_ref, dtx_ref, dax_ref):
    xc = cb_ref[...]
    for j in range(CONV_K - 1):
        xc = xc + buf_ref[j] * cw_ref[j:j + 1, :]
    xc = xc + x_ref[...] * cw_ref[CONV_K - 1:CONV_K, :]
    xcs_ref[...] = xc * jax.nn.sigmoid(xc)
    dt = jax.nn.softplus(dt_ref[...] + dtb_ref[...])
    da = jnp.exp(dt * (-jnp.exp(alog_ref[...])))
    ex = ex_ref[...]
    dtx_ref[...] = jnp.dot(dt, ex, precision=HIGHEST, preferred_element_type=F32)
    dax_ref[...] = jnp.dot(da, ex, precision=HIGHEST, preferred_element_type=F32)


def _ssd_sample_state_kernel(s0_ref, x_ref, b_ref, c_ref, dtx_ref, dax_ref, z_ref, dsk_ref, ng_ref,
                             sn_ref, yn_ref, yrow_ref):
    hpg = s0_ref.shape[1]
    P = SSD_HEAD_DIM
    eye = lax.broadcasted_iota(jnp.int32, (P, P), 0) == lax.broadcasted_iota(jnp.int32, (P, P), 1)
    x = x_ref[0]
    xdt = x * dtx_ref[0]
    b_row = b_ref[0]
    c_rows = jnp.broadcast_to(c_ref[0], (SUBLANES, SSD_STATE)).astype(BF16)
    for e in range(hpg):
        sl = slice(e * P, (e + 1) * P)
        xcol = jnp.sum(jnp.where(eye, jnp.broadcast_to(xdt[:, sl], (P, P)), 0.0), axis=1, keepdims=True)
        da = dax_ref[0][:, e * P:e * P + 1]
        s_new = s0_ref[0, e] * da + xcol * b_row
        sn_ref[0, e] = s_new
        yr = lax.dot_general(c_rows, s_new.astype(BF16), NT_DIMS, preferred_element_type=F32)
        yrow_ref[:, sl] = yr[0:1, :]
    y = yrow_ref[...] + x * dsk_ref[...]
    zz = z_ref[0]
    u = y * (zz * jax.nn.sigmoid(zz))
    yn_ref[0] = _rms_rows(u, ng_ref[...]).astype(yn_ref.dtype)


def _ssd_sample(xbc, buf_t, z, dt, s0, cw, cb, dt_bias, a_log, d_skip, norm_g, s_off=0):
    Bs, XW = xbc.shape
    d_inner = z.shape[1]
    n_heads = d_inner // SSD_HEAD_DIM
    gw = d_inner // SSD_GROUPS
    hpg = gw // SSD_HEAD_DIM
    ex = _ssd_expand_matrix(n_heads, d_inner)
    xcs, dtx, dax = pl.pallas_call(
        _ssd_sample_pre_kernel,
        out_shape=[jax.ShapeDtypeStruct((Bs, XW), F32), jax.ShapeDtypeStruct((Bs, d_inner), F32),
                   jax.ShapeDtypeStruct((Bs, d_inner), F32)],
        compiler_params=pltpu.CompilerParams(vmem_limit_bytes=VMEM_LIMIT), name="ssd_sample_pre")(
            xbc, buf_t, dt, cw, cb.reshape(1, XW), _pad_lanes(dt_bias), _pad_lanes(a_log), ex)
    r3 = lambda a: a.reshape(Bs, 1, a.shape[1])
    nbg = d_inner // SSD_STATE
    rowblk = lambda w, off=0: pl.BlockSpec((1, 1, w), lambda b, g, off=off: (b, 0, g + off))
    s_new, yn = pl.pallas_call(
        _ssd_sample_state_kernel,
        grid=(Bs, SSD_GROUPS),
        in_specs=[pl.BlockSpec((1, hpg, SSD_HEAD_DIM, SSD_STATE), lambda b, g: (b + s_off, g, 0, 0)),
                  rowblk(gw), rowblk(SSD_STATE, nbg), rowblk(SSD_STATE, nbg + SSD_GROUPS),
                  rowblk(gw), rowblk(gw), rowblk(gw),
                  pl.BlockSpec((1, gw), lambda b, g: (0, g)), pl.BlockSpec((1, gw), lambda b, g: (0, g))],
        out_specs=[pl.BlockSpec((1, hpg, SSD_HEAD_DIM, SSD_STATE), lambda b, g: (b, g, 0, 0)),
                   rowblk(gw)],
        out_shape=[jax.ShapeDtypeStruct((Bs,) + s0.shape[1:], F32), jax.ShapeDtypeStruct((Bs, 1, d_inner), BF16)],
        scratch_shapes=[pltpu.VMEM((1, gw), F32)],
        compiler_params=_cparams(("parallel", "parallel")), name="ssd_sample_state")(
            s0, r3(xcs), r3(xcs), r3(xcs), r3(dtx), r3(dax), r3(z),
            jnp.repeat(d_skip, SSD_HEAD_DIM).reshape(1, d_inner), norm_g.reshape(1, d_inner))
    return yn.reshape(Bs, d_inner), s_new


def _paged_topk_kernel(pt_ref, *refs, nblk, bps, ppb):
    pages = refs[:bps * ppb]
    q_ref, idx_ref, ksum_ref = refs[bps * ppb:]
    n = pl.program_id(1)
    for j in range(bps):
        tot = jnp.sum(pages[j * ppb][0], axis=0, keepdims=True)
        for r in range(1, ppb):
            tot = tot + jnp.sum(pages[j * ppb + r][0], axis=0, keepdims=True)
        ksum_ref[pl.ds(n * bps + j, 1), :] = tot / float(MOBA_BLOCK)

    @pl.when(n == pl.num_programs(1) - 1)
    def _():
        q = q_ref[0]
        rows = lax.broadcasted_iota(jnp.int32, (nblk, 1), 0)
        lane = lax.broadcasted_iota(jnp.int32, (1, LANES), 1)
        for h in range(A_HEADS):
            sl = slice(h * HEAD_DIM, (h + 1) * HEAD_DIM)
            gate = jnp.sum(ksum_ref[:, sl] * q[:, sl], axis=-1, keepdims=True)
            out = jnp.zeros((1, LANES), jnp.int32)
            for t in range(MOBA_TOPK):
                m = jnp.max(gate, axis=0, keepdims=True)
                idx = jnp.min(jnp.where(gate == m, rows, nblk), axis=0, keepdims=True)
                out = jnp.where(lane == t, idx, out)
                gate = jnp.where(rows == idx, -jnp.inf, gate)
            idx_ref[0, h:h + 1, :] = out


def _paged_attn_kernel(pt_ref, top_ref, q_ref, kn_ref, vn_ref, *refs, n_pages):
    k_refs = refs[:n_pages]
    v_refs = refs[n_pages:2 * n_pages]
    o_ref = refs[2 * n_pages]
    q = q_ref[0]
    qh = jnp.broadcast_to(q, (SUBLANES, HEAD_DIM)).astype(BF16)
    s_own = jnp.sum(q * kn_ref[0], axis=-1, keepdims=True) * ATTN_SCALE
    ss = [lax.dot_general(qh, kr[0].astype(BF16), NT_DIMS, preferred_element_type=F32) * ATTN_SCALE
          for kr in k_refs]
    m = s_own
    for s in ss:
        m = jnp.maximum(m, jnp.max(s[0:1], axis=-1, keepdims=True))
    p_own = jnp.exp(s_own - m)
    l = p_own
    acc = p_own * vn_ref[0]
    for s, vr in zip(ss, v_refs):
        p = jnp.exp(s - m)
        l = l + jnp.sum(p[0:1], axis=-1, keepdims=True)
        acc = acc + jnp.dot(p.astype(BF16), vr[0].astype(BF16), preferred_element_type=F32)[0:1]
    o_ref[0] = (acc / l).astype(o_ref.dtype)


def _paged_moba(q, k_new, v_new, k_pool, v_pool, page_table, page_off=0):
    Bs = q.shape[0]
    n_pages = page_table.shape[1]
    ppb = MOBA_BLOCK // PAGE_SIZE
    assert n_pages % ppb == 0
    nblk = n_pages // ppb
    assert nblk >= MOBA_TOPK
    bps = math.gcd(nblk, 4)
    q3, k3, v3 = (a.reshape(Bs, 1, A_W) for a in (q, k_new, v_new))

    def mean_spec(p):
        return pl.BlockSpec((1, PAGE_SIZE, A_W),
                            lambda b, n, pt, p=p: (pt[b, n * bps * ppb + p] + page_off, 0, 0))

    idx = pl.pallas_call(
        functools.partial(_paged_topk_kernel, nblk=nblk, bps=bps, ppb=ppb),
        grid_spec=pltpu.PrefetchScalarGridSpec(
            num_scalar_prefetch=1, grid=(Bs, nblk // bps),
            in_specs=[mean_spec(p) for p in range(bps * ppb)]
            + [pl.BlockSpec((1, 1, A_W), lambda b, n, pt: (b, 0, 0))],
            out_specs=pl.BlockSpec((1, A_HEADS, LANES), lambda b, n, pt: (b, 0, 0)),
            scratch_shapes=[pltpu.VMEM((nblk, A_W), F32)]),
        out_shape=jax.ShapeDtypeStruct((Bs, A_HEADS, LANES), jnp.int32),
        compiler_params=_cparams(("parallel", "arbitrary")), name="paged_topk")(
            page_table, *([k_pool] * (bps * ppb)), q3)
    top = idx[:, :, :MOBA_TOPK].reshape(-1)

    n_sel_pages = MOBA_TOPK * ppb

    def page_spec(sel, r):
        def imap(b, h, pt, tp):
            blk = tp[(b * A_HEADS + h) * MOBA_TOPK + sel]
            return (pt[b, blk * ppb + r] + page_off, 0, h)
        return pl.BlockSpec((1, PAGE_SIZE, HEAD_DIM), imap)

    page_specs = [page_spec(s, r) for s in range(MOBA_TOPK) for r in range(ppb)]
    head_spec = pl.BlockSpec((1, 1, HEAD_DIM), lambda b, h, pt, tp: (b, 0, h))
    att = pl.pallas_call(
        functools.partial(_paged_attn_kernel, n_pages=n_sel_pages),
        grid_spec=pltpu.PrefetchScalarGridSpec(
            num_scalar_prefetch=2, grid=(Bs, A_HEADS),
            in_specs=[head_spec, head_spec, head_spec] + page_specs + page_specs,
            out_specs=head_spec),
        out_shape=jax.ShapeDtypeStruct((Bs, 1, A_W), BF16),
        compiler_params=_cparams(("parallel", "parallel")), name="paged_attn")(
            page_table, top, q3, k3, v3, *([k_pool] * n_sel_pages), *([v_pool] * n_sel_pages))
    return att.reshape(Bs, A_W)


def _rope_tables(pos):
    inv = ROPE_THETA ** (-jnp.arange(ROT_HALF, dtype=F32) / ROT_HALF)
    ang = pos.astype(F32)[:, None] * inv
    cos, sin = jnp.cos(ang), jnp.sin(ang)
    T = pos.shape[0]
    rest = HEAD_DIM - ROT_DIM
    cos_t = jnp.concatenate([cos, cos, jnp.ones((T, rest), F32)], axis=-1)
    sin_t = jnp.concatenate([-sin, sin, jnp.zeros((T, rest), F32)], axis=-1)
    return cos_t, sin_t


ROW_TILE = 1024


def _row_tile(m):
    return ROW_TILE if m % ROW_TILE == 0 else m


def kernel(x_prompt, x_sample, cache_k, cache_v, state_lru_conv, state_lru_h, state_ssd_conv, state_ssd_h, page_table, norm_mix, norm_mlp, w_in_a, q_norm, k_norm, lru_conv_w, lru_conv_b, lru_wa, lru_ba, lru_wx, lru_bx, lru_lambda, w_out_a, w_in_s, ssd_conv_w, ssd_conv_b, ssd_dt_bias, ssd_a_log, ssd_d, ssd_norm, w_out_s, w_up, w_down):
    Bp, Tp, D = x_prompt.shape
    Bs, Ts, _ = x_sample.shape
    assert Ts == 1
    depth = norm_mix.shape[0]
    past_len = page_table.shape[1] * PAGE_SIZE
    Mp, Ms = Bp * Tp, Bs * Ts
    tmp, tms = _row_tile(Mp), _row_tile(Ms)
    assert Tp % tmp == 0
    xp = x_prompt.reshape(Mp, D)
    xs = x_sample.reshape(Ms, D)
    lru_w = lru_lambda.shape[1]
    d_inner = ssd_norm.shape[1]
    xbc_w = ssd_conv_w.shape[2]
    n_ssd_heads = ssd_d.shape[1]

    cos_p, sin_p = _rope_tables(jnp.arange(Tp, dtype=jnp.int32))
    cos_s, sin_s = _rope_tables(jnp.full((Ms,), past_len, dtype=jnp.int32))

    outs = {k: [] for k in ("kp", "vp", "ks", "vs", "lcp", "lcs", "lhp", "lhs", "scp", "scs", "shp", "shs")}
    for layer in range(depth):
        i = layer // 2
        if layer % 2 == 0:
            w_in = w_in_a[i].astype(BF16)
            w_out = w_out_a[i].astype(BF16)
            splits = [A_W, A_W, A_W, lru_w, lru_w]
            lru_args = (lru_conv_w[i], lru_conv_b[i], lru_wa[i], lru_ba[i], lru_wx[i], lru_bx[i], lru_lambda[i])
            q, k, v, xr, yg = _norm_proj(xp, norm_mix[layer], w_in, splits, tm=tmp,
                                         rope=((q_norm[i], k_norm[i]), cos_p, sin_p, Tp // tmp))
            att = _moba_prompt(q, k, v, batch=Bp, seq=Tp)
            rec, h_last = _lru_prompt(xr, yg, *lru_args, batch=Bp, seq=Tp)
            mixed_p = _out_proj([att, rec], w_out, xp, tm=tmp)
            outs["kp"].append(k.reshape(Bp, Tp, A_HEADS, HEAD_DIM))
            outs["vp"].append(v.reshape(Bp, Tp, A_HEADS, HEAD_DIM))
            outs["lcp"].append(xr.reshape(Bp, Tp, lru_w)[:, Tp - (CONV_K - 1):])
            outs["lhp"].append(h_last.reshape(Bp, lru_w))
            q, k, v, xr, yg = _norm_proj(xs, norm_mix[layer], w_in, splits, tm=tms,
                                         rope=((q_norm[i], k_norm[i]), cos_s, sin_s, 1))
            n_phys = cache_k.shape[1]
            att = _paged_moba(q, k, v, cache_k.reshape(-1, PAGE_SIZE, A_W), cache_v.reshape(-1, PAGE_SIZE, A_W),
                              page_table, page_off=i * n_phys)
            buf = state_lru_conv[i]
            rec, h_new = _lru_sample(xr, yg, buf.swapaxes(0, 1), state_lru_h[i], *lru_args)
            mixed_s = _out_proj([att, rec], w_out, xs, tm=tms)
            outs["ks"].append(k.reshape(Bs, Ts, A_HEADS, HEAD_DIM))
            outs["vs"].append(v.reshape(Bs, Ts, A_HEADS, HEAD_DIM))
            outs["lcs"].append(jnp.concatenate([buf[:, 1:], xr[:, None, :]], axis=1))
            outs["lhs"].append(h_new)
        else:
            main_w = d_inner + xbc_w
            w_in = w_in_s[i][:, :main_w].astype(BF16)
            w_dt = jnp.pad(w_in_s[i][:, main_w:], ((0, 0), (0, LANES - n_ssd_heads))).astype(BF16)
            w_out = w_out_s[i].astype(BF16)
            ssd_args = (ssd_conv_w[i], ssd_conv_b[i], ssd_dt_bias[i], ssd_a_log[i], ssd_d[i], ssd_norm[i])
            unslab = lambda a: a.swapaxes(0, 1).reshape(a.shape[1], -1)
            z, xbc, dt = _norm_proj(xp, norm_mix[layer], w_in, [d_inner, xbc_w], tm=tmp, extra_w=w_dt, slab=True)
            yn, s_last = _ssd_prompt(xbc, z, dt, *ssd_args, batch=Bp, seq=Tp)
            mixed_p = _out_proj([yn], w_out, xp, tm=tmp)
            tail = xbc.reshape(-1, Bp, Tp, LANES)[:, :, Tp - (CONV_K - 1):]
            outs["scp"].append(tail.transpose(1, 2, 0, 3).reshape(Bp, CONV_K - 1, xbc_w))
            outs["shp"].append(s_last.reshape(Bp, n_ssd_heads, SSD_HEAD_DIM, SSD_STATE))
            z, xbc, dt = _norm_proj(xs, norm_mix[layer], w_in, [d_inner, xbc_w], tm=tms, extra_w=w_dt, slab=True)
            z, xbc = unslab(z), unslab(xbc)
            buf = state_ssd_conv[i]
            yn, s_new = _ssd_sample(xbc, buf.swapaxes(0, 1), z, dt,
                                    state_ssd_h.reshape((-1,) + state_ssd_h.shape[2:]), *ssd_args, s_off=i * Bs)
            mixed_s = _out_proj([yn], w_out, xs, tm=tms)
            outs["scs"].append(jnp.concatenate([buf[:, 1:], xbc[:, None, :]], axis=1))
            outs["shs"].append(s_new)
        wu, wd = w_up[layer].astype(BF16), w_down[layer].astype(BF16)
        xp = _mlp(mixed_p, norm_mlp[layer], wu, wd, tm=tmp)
        xs = _mlp(mixed_s, norm_mlp[layer], wu, wd, tm=tms)

    st = lambda key: jnp.stack(outs[key])
    return (xp.reshape(Bp, Tp, D), xs.reshape(Bs, Ts, D), st("kp"), st("vp"), st("ks"), st("vs"),
            st("lcp"), st("lcs"), st("lhp"), st("lhs"), st("scp"), st("scs"), st("shp"), st("shs"))
```

```python
import functools
import math

import jax
import jax.numpy as jnp
from jax import lax
from jax.experimental import pallas as pl
from jax.experimental.pallas import tpu as pltpu

F32 = jnp.float32
BF16 = jnp.bfloat16
HIGHEST = lax.Precision.HIGHEST

A_HEADS = 8
HEAD_DIM = 128
A_W = A_HEADS * HEAD_DIM
ROT_DIM = HEAD_DIM // 4
ROT_HALF = ROT_DIM // 2
ROPE_THETA = 500000.0
MOBA_BLOCK = 256
MOBA_TOPK = 3
ATTN_SCALE = HEAD_DIM ** -0.5
LRU_HEADS = 8
LRU_C = 8.0
CONV_K = 4
SSD_HEAD_DIM = 64
SSD_STATE = 128
SSD_GROUPS = 8
SSD_CHUNK = 128
PAGE_SIZE = 128
EPS = 1e-6

LANES = 128
SUBLANES = 8
NEG_BIG = -1e30
LOG2E = math.log2(math.e)
VMEM_BYTES_V7X = 64 * 1024 * 1024
VMEM_LIMIT = VMEM_BYTES_V7X - 8 * 1024 * 1024
VMEM_LIMIT_MLP = VMEM_BYTES_V7X - 4 * 1024 * 1024

NT_DIMS = (((1,), (1,)), ((), ()))


def _cparams(sem, vmem=VMEM_LIMIT):
    return pltpu.CompilerParams(dimension_semantics=sem, vmem_limit_bytes=vmem)


def _sigmoid(x):
    return 0.5 * (jnp.tanh(0.5 * x) + 1.0)


def _silu(x):
    h = 0.5 * x
    return h * (jnp.tanh(h) + 1.0)


def _rms_rows(x, g):
    return x * lax.rsqrt(jnp.mean(x * x, axis=-1, keepdims=True) + EPS) * g


def _head_norm_rope(acc, g, cos, sin):
    y = _rms_rows(acc, g)
    lane = lax.broadcasted_iota(jnp.int32, y.shape, 1)
    partner = jnp.where(lane < ROT_HALF, pltpu.roll(y, LANES - ROT_HALF, 1), pltpu.roll(y, ROT_HALF, 1))
    return y * cos + partner * sin


def _norm_proj_kernel(*refs, ranges, n_rope, has_extra, tn, slab):
    it = iter(refs)
    x_ref, g_ref, w_ref = next(it), next(it), next(it)
    rope_g = [next(it) for _ in range(n_rope)]
    if n_rope:
        cos_ref, sin_ref = next(it), next(it)
    if has_extra:
        we_ref = next(it)
    outs = [next(it) for _ in ranges]
    if has_extra:
        oe_ref = next(it)
    xn_ref = next(it)
    j = pl.program_id(1)

    @pl.when(j == 0)
    def _():
        xn_ref[...] = _rms_rows(x_ref[...], g_ref[...]).astype(BF16)
        if has_extra:
            oe_ref[...] = jnp.dot(xn_ref[...], we_ref[...], preferred_element_type=F32)

    acc = jnp.dot(xn_ref[...], w_ref[...], preferred_element_type=F32)
    for s, (a, b) in enumerate(ranges):
        @pl.when((j >= a) & (j < b))
        def _(s=s):
            if s < n_rope:
                cos, sin = cos_ref[...], sin_ref[...]
                for hh in range(tn // HEAD_DIM):
                    sl = slice(hh * HEAD_DIM, (hh + 1) * HEAD_DIM)
                    outs[s][:, sl] = _head_norm_rope(acc[:, sl], rope_g[s][...], cos, sin)
            elif slab:
                for cblk in range(tn // LANES):
                    outs[s][cblk] = acc[:, cblk * LANES:(cblk + 1) * LANES]
            else:
                outs[s][...] = acc


def _norm_proj(x, g, w, splits, *, tm, tn=512, rope=None, extra_w=None, slab=False):
    M, K = x.shape
    N = sum(splits)
    assert w.shape[0] == K and w.shape[1] >= N and M % tm == 0 and all(s % tn == 0 for s in splits)
    edges = [0]
    for s in splits:
        edges.append(edges[-1] + s // tn)
    ranges = tuple((edges[i], edges[i + 1]) for i in range(len(splits)))
    n_rope = 0 if rope is None else len(rope[0])
    in_specs = [pl.BlockSpec((tm, K), lambda i, j: (i, 0)),
                pl.BlockSpec((1, K), lambda i, j: (0, 0)),
                pl.BlockSpec((K, tn), lambda i, j: (0, j))]
    args = [x, g.reshape(1, K), w]
    if rope is not None:
        gains, cos, sin, period = rope
        for gg in gains:
            in_specs.append(pl.BlockSpec((1, HEAD_DIM), lambda i, j: (0, 0)))
            args.append(gg.reshape(1, HEAD_DIM))
        for tab in (cos, sin):
            in_specs.append(pl.BlockSpec((tm, HEAD_DIM), lambda i, j, period=period: (i % period, 0)))
            args.append(tab)
    if extra_w is not None:
        in_specs.append(pl.BlockSpec((K, LANES), lambda i, j: (0, 0)))
        args.append(extra_w)
    out_specs, out_shape = [], []
    for (a, b), s in zip(ranges, splits):
        if slab:
            out_specs.append(pl.BlockSpec((tn // LANES, tm, LANES),
                                          lambda i, j, a=a, b=b: (jnp.clip(j - a, 0, b - a - 1), i, 0)))
            out_shape.append(jax.ShapeDtypeStruct((s // LANES, M, LANES), F32))
        else:
            out_specs.append(pl.BlockSpec((tm, tn), lambda i, j, a=a, b=b: (i, jnp.clip(j - a, 0, b - a - 1))))
            out_shape.append(jax.ShapeDtypeStruct((M, s), F32))
    if extra_w is not None:
        out_specs.append(pl.BlockSpec((tm, LANES), lambda i, j: (i, 0)))
        out_shape.append(jax.ShapeDtypeStruct((M, LANES), F32))
    kern = functools.partial(_norm_proj_kernel, ranges=ranges, n_rope=n_rope,
                             has_extra=extra_w is not None, tn=tn, slab=slab)
    return pl.pallas_call(
        kern, grid=(M // tm, N // tn), in_specs=in_specs, out_specs=out_specs, out_shape=out_shape,
        scratch_shapes=[pltpu.VMEM((tm, K), BF16)],
        compiler_params=_cparams(("parallel", "arbitrary")), name="norm_proj")(*args)


def _out_proj_kernel(*refs, n_parts):
    xs = refs[:n_parts]
    ws = refs[n_parts:2 * n_parts]
    res_ref, o_ref = refs[2 * n_parts], refs[2 * n_parts + 1]
    acc = res_ref[...]
    for x_ref, w_ref in zip(xs, ws):
        acc = acc + jnp.dot(x_ref[...], w_ref[...], preferred_element_type=F32)
    o_ref[...] = acc


def _out_proj(parts, w, res, *, tm, tn=512):
    M, N = res.shape
    kp = parts[0].shape[1]
    assert all(p.shape == (M, kp) for p in parts) and w.shape == (kp * len(parts), N)
    in_specs = [pl.BlockSpec((tm, kp), lambda i, j: (i, 0)) for _ in parts]
    in_specs += [pl.BlockSpec((kp, tn), lambda i, j, p=p: (p, j)) for p in range(len(parts))]
    in_specs.append(pl.BlockSpec((tm, tn), lambda i, j: (i, j)))
    return pl.pallas_call(
        functools.partial(_out_proj_kernel, n_parts=len(parts)),
        grid=(M // tm, N // tn), in_specs=in_specs,
        out_specs=pl.BlockSpec((tm, tn), lambda i, j: (i, j)),
        out_shape=jax.ShapeDtypeStruct((M, N), F32),
        compiler_params=_cparams(("parallel", "parallel")), name="out_proj")(*parts, *([w] * len(parts)), res)


def _mlp_kernel(x_ref, g_ref, wu_ref, wd_ref, o_ref, xn_ref):
    f = pl.program_id(1)

    @pl.when(f == 0)
    def _():
        x = x_ref[...]
        xn_ref[...] = _rms_rows(x, g_ref[...]).astype(BF16)
        o_ref[...] = x

    h = jnp.maximum(jnp.dot(xn_ref[...], wu_ref[...], preferred_element_type=F32), 0.0)
    o_ref[...] += jnp.dot((h * h).astype(BF16), wd_ref[...], preferred_element_type=F32)


def _mlp(x, g, w_up, w_down, *, tm, tf=1024):
    M, D = x.shape
    FF = w_up.shape[1]
    return pl.pallas_call(
        _mlp_kernel, grid=(M // tm, FF // tf),
        in_specs=[pl.BlockSpec((tm, D), lambda i, f: (i, 0)),
                  pl.BlockSpec((1, D), lambda i, f: (0, 0)),
                  pl.BlockSpec((D, tf), lambda i, f: (0, f)),
                  pl.BlockSpec((tf, D), lambda i, f: (f, 0))],
        out_specs=pl.BlockSpec((tm, D), lambda i, f: (i, 0)),
        out_shape=jax.ShapeDtypeStruct((M, D), F32),
        scratch_shapes=[pltpu.VMEM((tm, D), BF16)],
        compiler_params=_cparams(("parallel", "arbitrary"), VMEM_LIMIT_MLP), name="mlp")(
            x, g.reshape(1, D), w_up, w_down)


def _moba_prompt_kernel(q_ref, k_ref, v_ref, hot_ref, o_ref, kaug_ref, vb_ref, kmean_ref, *, nb):
    blk, D = MOBA_BLOCK, HEAD_DIM
    kaug_ref[:, 0:D] = k_ref[...].astype(BF16)
    kaug_ref[:, D:2 * D] = hot_ref[...]
    vb_ref[...] = v_ref[...].astype(BF16)
    kmean_ref[...] = jnp.zeros(kmean_ref.shape, F32)
    for n in range(nb):
        kmean_ref[n:n + 1, :] = jnp.mean(k_ref[n * blk:(n + 1) * blk, :], axis=0, keepdims=True)

    lane = lax.broadcasted_iota(jnp.int32, (blk, LANES), 1)
    row = lax.broadcasted_iota(jnp.int32, (blk, blk), 0)
    col = lax.broadcasted_iota(jnp.int32, (blk, blk), 1)
    nbp = kmean_ref.shape[0]
    blk_id = lax.broadcasted_iota(jnp.int32, (nbp, blk), 0)
    eye = (row == col).astype(BF16)
    for qb in range(nb):
        q = q_ref[qb * blk:(qb + 1) * blk, :]
        if qb > MOBA_TOPK:
            gate = lax.dot_general(kmean_ref[...], q, NT_DIMS, precision=HIGHEST, preferred_element_type=F32)
            gate = jnp.where(blk_id < qb, gate, -jnp.inf)
            bias_t = jnp.where(blk_id == qb, 0.0, NEG_BIG)
            for _ in range(MOBA_TOPK):
                m = jnp.max(gate, axis=0, keepdims=True)
                idx = jnp.min(jnp.where(gate == m, blk_id, nbp), axis=0, keepdims=True)
                hit = blk_id == idx
                bias_t = jnp.where(hit & (m > -jnp.inf), 0.0, bias_t)
                gate = jnp.where(hit, -jnp.inf, gate)
            pad = jnp.zeros((LANES - nbp, blk), F32)
            bias = lax.dot_general(eye, jnp.concatenate([bias_t, pad], axis=0).astype(BF16), NT_DIMS,
                                   preferred_element_type=F32).astype(BF16)
        else:
            bias = jnp.where(lane <= qb, 0.0, NEG_BIG).astype(BF16)
        lhs = jnp.concatenate([(q * (ATTN_SCALE * LOG2E)).astype(BF16), bias], axis=1)
        own = slice(qb * blk, (qb + 1) * blk)
        s_own = lax.dot_general(lhs, kaug_ref[own, :], NT_DIMS, preferred_element_type=F32)
        s_own = jnp.where(col <= row, s_own, NEG_BIG)
        m = jnp.max(s_own, axis=-1, keepdims=True)
        if qb > 0:
            s_past = lax.dot_general(lhs, kaug_ref[0:qb * blk, :], NT_DIMS, preferred_element_type=F32)
            m = jnp.maximum(m, jnp.max(s_past, axis=-1, keepdims=True))
        p = jnp.exp2(s_own - m)
        l = jnp.sum(p, axis=-1, keepdims=True)
        acc = jnp.dot(p.astype(BF16), vb_ref[own, :], preferred_element_type=F32)
        if qb > 0:
            p = jnp.exp2(s_past - m)
            l = l + jnp.sum(p, axis=-1, keepdims=True)
            acc = acc + jnp.dot(p.astype(BF16), vb_ref[0:qb * blk, :], preferred_element_type=F32)
        o_ref[own, :] = (acc / l).astype(o_ref.dtype)


def _moba_prompt(q, k, v, *, batch, seq):
    M = q.shape[0]
    nb = seq // MOBA_BLOCK
    assert nb <= LANES and seq % MOBA_BLOCK == 0
    hot = (lax.broadcasted_iota(jnp.int32, (seq, LANES), 0) // MOBA_BLOCK
           == lax.broadcasted_iota(jnp.int32, (seq, LANES), 1)).astype(BF16)
    seq_head = pl.BlockSpec((seq, HEAD_DIM), lambda b, h: (b, h))
    return pl.pallas_call(
        functools.partial(_moba_prompt_kernel, nb=nb),
        grid=(batch, A_HEADS),
        in_specs=[seq_head, seq_head, seq_head, pl.BlockSpec((seq, LANES), lambda b, h: (0, 0))],
        out_specs=seq_head,
        out_shape=jax.ShapeDtypeStruct((M, A_W), BF16),
        scratch_shapes=[pltpu.VMEM((seq, 2 * HEAD_DIM), BF16), pltpu.VMEM((seq, HEAD_DIM), BF16),
                        pltpu.VMEM((-(-nb // SUBLANES) * SUBLANES, HEAD_DIM), F32)],
        compiler_params=_cparams(("parallel", "parallel")), name="moba_prompt")(q, k, v, hot)


def _lru_gates(xc, wa_ref, ba, wx_ref, bx, lam):
    bw = xc.shape[1] // LRU_HEADS
    rs, is_ = [], []
    for n in range(LRU_HEADS):
        xb = xc[:, n * bw:(n + 1) * bw].astype(BF16)
        rs.append(jnp.dot(xb, wa_ref[n], preferred_element_type=F32))
        is_.append(jnp.dot(xb, wx_ref[n], preferred_element_type=F32))
    r = _sigmoid(jnp.concatenate(rs, axis=-1) + ba)
    i = _sigmoid(jnp.concatenate(is_, axis=-1) + bx)
    log_a = -LRU_C * r * jax.nn.softplus(-lam)
    a = jnp.exp(log_a)
    t = jnp.tanh(log_a)
    u = jnp.sqrt(-2.0 * t / (1.0 - t)) * (i * xc)
    return a, u


def _lru_prompt_kernel(xr_ref, yg_ref, cw_ref, cb_ref, wa_ref, ba_ref, wx_ref, bx_ref, lam_ref,
                       rec_ref, hl_ref, xx_ref, a_ref, u_ref, h_ref, *, tt):
    ti = pl.program_id(1)
    pad = SUBLANES

    @pl.when(ti == 0)
    def _():
        xx_ref[0:pad, :] = jnp.zeros((pad, xx_ref.shape[1]), F32)
        h_ref[...] = jnp.zeros(h_ref.shape, F32)

    @pl.when(ti > 0)
    def _():
        xx_ref[0:pad, :] = xx_ref[tt:tt + pad, :]

    xx_ref[pad:pad + tt, :] = xr_ref[...]
    xc = cb_ref[...]
    for j in range(CONV_K):
        off = pad - (CONV_K - 1) + j
        xc = xc + xx_ref[off:off + tt, :] * cw_ref[j:j + 1, :]
    a, u = _lru_gates(xc, wa_ref, ba_ref[...], wx_ref, bx_ref[...], lam_ref[...])
    a_ref[...] = a
    u_ref[...] = u

    sub = lax.broadcasted_iota(jnp.int32, (SUBLANES, a.shape[1]), 0)

    def group(gi, h):
        r0 = pl.multiple_of(gi * SUBLANES, SUBLANES)
        a8 = a_ref[pl.ds(r0, SUBLANES), :]
        u8 = u_ref[pl.ds(r0, SUBLANES), :]
        for s in (1, 2, 4):
            keep = sub >= s
            a_sh = pltpu.roll(a8, s, 0)
            u_sh = pltpu.roll(u8, s, 0)
            u8 = jnp.where(keep, a8 * u_sh + u8, u8)
            a8 = jnp.where(keep, a8 * a_sh, a8)
        hs = a8 * h + u8
        yg = yg_ref[pl.ds(r0, SUBLANES), :]
        rec_ref[pl.ds(r0, SUBLANES), :] = (hs * jax.nn.gelu(yg)).astype(rec_ref.dtype)
        return hs[SUBLANES - 1:SUBLANES, :]

    h = lax.fori_loop(0, tt // SUBLANES, group, h_ref[...])
    h_ref[...] = h

    @pl.when(ti == pl.num_programs(1) - 1)
    def _():
        hl_ref[0] = h


def _lru_prompt(xr, yg, cw, cb, wa, ba, wx, bx, lam, *, batch, seq, tt=256):
    M, W = xr.shape
    nt = seq // tt
    row = lambda a: a.reshape(1, W)
    full2 = lambda shp: pl.BlockSpec(shp, lambda b, t: (0, 0))
    full3 = lambda shp: pl.BlockSpec(shp, lambda b, t: (0, 0, 0))
    return pl.pallas_call(
        functools.partial(_lru_prompt_kernel, tt=tt),
        grid=(batch, nt),
        in_specs=[pl.BlockSpec((tt, W), lambda b, t: (b * nt + t, 0)),
                  pl.BlockSpec((tt, W), lambda b, t: (b * nt + t, 0)),
                  full2((CONV_K, W)), full2((1, W)), full3(wa.shape), full2((1, W)),
                  full3(wx.shape), full2((1, W)), full2((1, W))],
        out_specs=[pl.BlockSpec((tt, W), lambda b, t: (b * nt + t, 0)),
                   pl.BlockSpec((1, 1, W), lambda b, t: (b, 0, 0))],
        out_shape=[jax.ShapeDtypeStruct((M, W), BF16), jax.ShapeDtypeStruct((batch, 1, W), F32)],
        scratch_shapes=[pltpu.VMEM((tt + SUBLANES, W), F32), pltpu.VMEM((tt, W), F32),
                        pltpu.VMEM((tt, W), F32), pltpu.VMEM((1, W), F32)],
        compiler_params=_cparams(("parallel", "arbitrary")), name="lru_prompt")(
            xr, yg, cw, row(cb), wa.astype(BF16), row(ba), wx.astype(BF16), row(bx), row(lam))


def _lru_sample_kernel(xr_ref, yg_ref, buf_ref, h0_ref, cw_ref, cb_ref, wa_ref, ba_ref, wx_ref, bx_ref,
                       lam_ref, rec_ref, hn_ref):
    xc = cb_ref[...]
    for j in range(CONV_K - 1):
        xc = xc + buf_ref[j] * cw_ref[j:j + 1, :]
    xc = xc + xr_ref[...] * cw_ref[CONV_K - 1:CONV_K, :]
    a, u = _lru_gates(xc, wa_ref, ba_ref[...], wx_ref, bx_ref[...], lam_ref[...])
    h = a * h0_ref[...] + u
    hn_ref[...] = h
    rec_ref[...] = (h * jax.nn.gelu(yg_ref[...])).astype(rec_ref.dtype)


def _lru_sample(xr, yg, buf_t, h0, cw, cb, wa, ba, wx, bx, lam):
    Bs, W = xr.shape
    row = lambda a: a.reshape(1, W)
    return pl.pallas_call(
        _lru_sample_kernel,
        out_shape=[jax.ShapeDtypeStruct((Bs, W), BF16), jax.ShapeDtypeStruct((Bs, W), F32)],
        compiler_params=pltpu.CompilerParams(vmem_limit_bytes=VMEM_LIMIT), name="lru_sample")(
            xr, yg, buf_t, h0, cw, row(cb), wa.astype(BF16), row(ba), wx.astype(BF16), row(bx), row(lam))


def _ssd_prompt_kernel(xbc_ref, prev_ref, z_ref, dt_ref, cw_ref, cb_ref, dtb_ref, alog_ref, dsk_ref, ng_ref,
                       ex_ref, yn_ref, sl_ref, xx_ref, xcv_ref, y_ref, st_ref, *, d_inner):
    c = pl.program_id(1)
    L = SSD_CHUNK
    pad = SUBLANES
    gw = d_inner // SSD_GROUPS
    spg = gw // LANES
    hps = LANES // SSD_HEAD_DIM
    head_of_lane = lax.broadcasted_iota(jnp.int32, (L, LANES), 1) // SSD_HEAD_DIM
    assert SSD_STATE == LANES

    @pl.when(c == 0)
    def _():
        xx_ref[:, 0:pad, :] = jnp.zeros((xx_ref.shape[0], pad, LANES), F32)
        st_ref[...] = jnp.zeros(st_ref.shape, F32)

    @pl.when(c > 0)
    def _():
        xx_ref[:, 0:pad, :] = prev_ref[...]

    xx_ref[:, pad:pad + L, :] = xbc_ref[...]

    def conv_slab(k, carry):
        xs = xx_ref[k]
        xc = cb_ref[k]
        for j in range(CONV_K):
            back = CONV_K - 1 - j
            tap = xs if back == 0 else pltpu.roll(xs, back, 0)
            xc = xc + tap[pad:] * cw_ref[k, j:j + 1, :]
        xcv_ref[k] = _silu(xc)
        return carry

    lax.fori_loop(0, xx_ref.shape[0], conv_slab, 0)

    def slabs(ref, first, count):
        return jnp.concatenate([ref[first + i] for i in range(count)], axis=1)

    def split2(a):
        hi = a.astype(BF16)
        return hi, (a - hi.astype(F32)).astype(BF16)

    dt = jax.nn.softplus(dt_ref[...] + dtb_ref[...])
    dta = dt * (-jnp.exp(alog_ref[...]))
    tril = lax.broadcasted_iota(jnp.int32, (L, L), 1) <= lax.broadcasted_iota(jnp.int32, (L, L), 0)
    cs = jnp.dot(tril.astype(F32), dta, precision=HIGHEST, preferred_element_type=F32)
    cs_t = cs.T
    dt_t = dt.T
    last = cs[L - 1:L, :]
    w_hi, w_lo = split2(dt * jnp.exp(last - cs))
    e_hi, e_lo = split2(jnp.exp(cs))
    cdec = jnp.broadcast_to(jnp.exp(last), (SUBLANES, LANES))
    cd_hi, cd_rest = split2(cdec)
    cd_mid, cd_lo = split2(cdec - cd_hi.astype(F32))

    def expand(terms, exg):
        out = jnp.dot(terms[0], exg, preferred_element_type=F32)
        for t in terms[1:]:
            out = out + jnp.dot(t, exg, preferred_element_type=F32)
        return out

    for g in range(SSD_GROUPS):
        lo = g * gw
        exg = ex_ref[:, lo:lo + gw]
        w_x = expand((w_hi, w_lo), exg)
        ecs_x = expand((e_hi, e_lo), exg)
        cd_x = expand((cd_hi, cd_mid, cd_lo), exg)[0:1]
        xg = slabs(xcv_ref, g * spg, spg)
        bg = xcv_ref[d_inner // LANES + g]
        cg = xcv_ref[d_inner // LANES + SSD_GROUPS + g].astype(BF16)
        bg_t = bg.T.astype(BF16)
        cb = jnp.dot(cg, bg_t, preferred_element_type=F32)
        st_g = st_ref[:, lo:lo + gw]
        y_off = jnp.dot(cg, st_g.astype(BF16), preferred_element_type=F32) * ecs_x
        st_ref[:, lo:lo + gw] = st_g * cd_x + jnp.dot(bg_t, (xg * w_x).astype(BF16), preferred_element_type=F32)
        for i in range(spg):
            xs_h = xcv_ref[g * spg + i].astype(BF16)
            lhs, rhs = [], []
            for k in range(hps):
                h = (g * spg + i) * hps + k
                dec = jnp.exp(jnp.where(tril, cs[:, h:h + 1] - cs_t[h:h + 1, :], NEG_BIG))
                lhs.append((cb * dec * dt_t[h:h + 1, :]).astype(BF16))
                rhs.append(jnp.where(head_of_lane == k, xs_h, jnp.zeros_like(xs_h)))
            yd = jnp.dot(jnp.concatenate(lhs, axis=1), jnp.concatenate(rhs, axis=0), preferred_element_type=F32)
            cols = slice(lo + i * LANES, lo + (i + 1) * LANES)
            y_ref[:, cols] = yd + y_off[:, i * LANES:(i + 1) * LANES]

    for g in range(SSD_GROUPS):
        lo = g * gw
        cols = slice(lo, lo + gw)
        y = y_ref[:, cols] + slabs(xcv_ref, g * spg, spg) * dsk_ref[:, cols]
        zz = slabs(z_ref, g * spg, spg)
        u = y * _silu(zz)
        yn_ref[:, cols] = _rms_rows(u, ng_ref[:, cols]).astype(yn_ref.dtype)

    @pl.when(c == pl.num_programs(1) - 1)
    def _():
        for k in range(d_inner // LANES):
            sl_ref[0, k * LANES:(k + 1) * LANES, :] = st_ref[:, k * LANES:(k + 1) * LANES].T


def _ssd_expand_matrix(n_heads, d_inner):
    h = lax.broadcasted_iota(jnp.int32, (LANES, d_inner), 0)
    col = lax.broadcasted_iota(jnp.int32, (LANES, d_inner), 1)
    return (col // SSD_HEAD_DIM == h).astype(F32) * (h < n_heads).astype(F32)


def _pad_lanes(v):
    return jnp.pad(v, (0, LANES - v.shape[0])).reshape(1, LANES)


def _ssd_prompt(xbc, z, dt, cw, cb, dt_bias, a_log, d_skip, norm_g, *, batch, seq):
    nxs, M, _ = xbc.shape
    nzs = z.shape[0]
    XW, d_inner = nxs * LANES, nzs * LANES
    n_heads = d_inner // SSD_HEAD_DIM
    L = SSD_CHUNK
    nc = seq // L
    rpb = L // SUBLANES
    const2 = lambda shp: pl.BlockSpec(shp, lambda b, c: (0, 0))
    const3 = lambda shp: pl.BlockSpec(shp, lambda b, c: (0, 0, 0))
    return pl.pallas_call(
        functools.partial(_ssd_prompt_kernel, d_inner=d_inner),
        grid=(batch, nc),
        in_specs=[pl.BlockSpec((nxs, L, LANES), lambda b, c: (0, b * nc + c, 0)),
                  pl.BlockSpec((nxs, SUBLANES, LANES), lambda b, c: (0, jnp.maximum((b * nc + c) * rpb - 1, 0), 0)),
                  pl.BlockSpec((nzs, L, LANES), lambda b, c: (0, b * nc + c, 0)),
                  pl.BlockSpec((L, LANES), lambda b, c: (b * nc + c, 0)),
                  const3((nxs, CONV_K, LANES)), const3((nxs, 1, LANES)), const2((1, LANES)), const2((1, LANES)),
                  const2((1, d_inner)), const2((1, d_inner)), const2((LANES, d_inner))],
        out_specs=[pl.BlockSpec((L, d_inner), lambda b, c: (b * nc + c, 0)),
                   pl.BlockSpec((1, d_inner, SSD_STATE), lambda b, c: (b, 0, 0))],
        out_shape=[jax.ShapeDtypeStruct((M, d_inner), BF16),
                   jax.ShapeDtypeStruct((batch, d_inner, SSD_STATE), F32)],
        scratch_shapes=[pltpu.VMEM((nxs, L + SUBLANES, LANES), F32), pltpu.VMEM((nxs, L, LANES), F32),
                        pltpu.VMEM((L, d_inner), F32), pltpu.VMEM((SSD_STATE, d_inner), F32)],
        compiler_params=_cparams(("parallel", "arbitrary")), name="ssd_prompt")(
            xbc, xbc, z, dt, cw.reshape(CONV_K, nxs, LANES).swapaxes(0, 1), cb.reshape(nxs, 1, LANES),
            _pad_lanes(dt_bias), _pad_lanes(a_log),
            jnp.repeat(d_skip, SSD_HEAD_DIM).reshape(1, d_inner), norm_g.reshape(1, d_inner),
            _ssd_expand_matrix(n_heads, d_inner).astype(BF16))


def _ssd_sample_pre_kernel(x_ref, buf_ref, dt_ref, cw_ref, cb_ref, dtb_ref, alog_ref, ex_ref,
                           xcs_ref, dtx_ref, dax_ref):
    xc = cb_ref[...]
    for j in range(CONV_K - 1):
        xc = xc + buf_ref[j] * cw_ref[j:j + 1, :]
    xc = xc + x_ref[...] * cw_ref[CONV_K - 1:CONV_K, :]
    xcs_ref[...] = _silu(xc)
    dt = jax.nn.softplus(dt_ref[...] + dtb_ref[...])
    da = jnp.exp(dt * (-jnp.exp(alog_ref[...])))
    ex = ex_ref[...]
    dtx_ref[...] = jnp.dot(dt, ex, precision=HIGHEST, preferred_element_type=F32)
    dax_ref[...] = jnp.dot(da, ex, precision=HIGHEST, preferred_element_type=F32)


def _ssd_sample_state_kernel(s0_ref, x_ref, b_ref, c_ref, dtx_ref, dax_ref, z_ref, dsk_ref, ng_ref,
                             sn_ref, yn_ref, yrow_ref):
    hpg = s0_ref.shape[1]
    P = SSD_HEAD_DIM
    eye = lax.broadcasted_iota(jnp.int32, (P, P), 0) == lax.broadcasted_iota(jnp.int32, (P, P), 1)
    x = x_ref[0]
    xdt = x * dtx_ref[0]
    b_row = b_ref[0]
    c_rows = jnp.broadcast_to(c_ref[0], (SUBLANES, SSD_STATE)).astype(BF16)
    for e in range(hpg):
        sl = slice(e * P, (e + 1) * P)
        xcol = jnp.sum(jnp.where(eye, jnp.broadcast_to(xdt[:, sl], (P, P)), 0.0), axis=1, keepdims=True)
        da = dax_ref[0][:, e * P:e * P + 1]
        s_new = s0_ref[0, e] * da + xcol * b_row
        sn_ref[0, e] = s_new
        yr = lax.dot_general(c_rows, s_new.astype(BF16), NT_DIMS, preferred_element_type=F32)
        yrow_ref[:, sl] = yr[0:1, :]
    y = yrow_ref[...] + x * dsk_ref[...]
    zz = z_ref[0]
    u = y * _silu(zz)
    yn_ref[0] = _rms_rows(u, ng_ref[...]).astype(yn_ref.dtype)


def _ssd_sample(xbc, buf_t, z, dt, s0, cw, cb, dt_bias, a_log, d_skip, norm_g, s_off=0):
    Bs, XW = xbc.shape
    d_inner = z.shape[1]
    n_heads = d_inner // SSD_HEAD_DIM
    gw = d_inner // SSD_GROUPS
    hpg = gw // SSD_HEAD_DIM
    ex = _ssd_expand_matrix(n_heads, d_inner)
    xcs, dtx, dax = pl.pallas_call(
        _ssd_sample_pre_kernel,
        out_shape=[jax.ShapeDtypeStruct((Bs, XW), F32), jax.ShapeDtypeStruct((Bs, d_inner), F32),
                   jax.ShapeDtypeStruct((Bs, d_inner), F32)],
        compiler_params=pltpu.CompilerParams(vmem_limit_bytes=VMEM_LIMIT), name="ssd_sample_pre")(
            xbc, buf_t, dt, cw, cb.reshape(1, XW), _pad_lanes(dt_bias), _pad_lanes(a_log), ex)
    r3 = lambda a: a.reshape(Bs, 1, a.shape[1])
    nbg = d_inner // SSD_STATE
    rowblk = lambda w, off=0: pl.BlockSpec((1, 1, w), lambda b, g, off=off: (b, 0, g + off))
    s_new, yn = pl.pallas_call(
        _ssd_sample_state_kernel,
        grid=(Bs, SSD_GROUPS),
        in_specs=[pl.BlockSpec((1, hpg, SSD_HEAD_DIM, SSD_STATE), lambda b, g: (b + s_off, g, 0, 0)),
                  rowblk(gw), rowblk(SSD_STATE, nbg), rowblk(SSD_STATE, nbg + SSD_GROUPS),
                  rowblk(gw), rowblk(gw), rowblk(gw),
                  pl.BlockSpec((1, gw), lambda b, g: (0, g)), pl.BlockSpec((1, gw), lambda b, g: (0, g))],
        out_specs=[pl.BlockSpec((1, hpg, SSD_HEAD_DIM, SSD_STATE), lambda b, g: (b, g, 0, 0)),
                   rowblk(gw)],
        out_shape=[jax.ShapeDtypeStruct((Bs,) + s0.shape[1:], F32), jax.ShapeDtypeStruct((Bs, 1, d_inner), BF16)],
        scratch_shapes=[pltpu.VMEM((1, gw), F32)],
        compiler_params=_cparams(("parallel", "parallel")), name="ssd_sample_state")(
            s0, r3(xcs), r3(xcs), r3(xcs), r3(dtx), r3(dax), r3(z),
            jnp.repeat(d_skip, SSD_HEAD_DIM).reshape(1, d_inner), norm_g.reshape(1, d_inner))
    return yn.reshape(Bs, d_inner), s_new


def _paged_topk_kernel(pt_ref, *refs, nblk, bps, ppb):
    pages = refs[:bps * ppb]
    q_ref, idx_ref, kmean_ref = refs[bps * ppb:]
    n = pl.program_id(1)
    for j in range(bps):
        tot = jnp.sum(pages[j * ppb][0, 0], axis=0)
        for r in range(1, ppb):
            tot = tot + jnp.sum(pages[j * ppb + r][0, 0], axis=0)
        kmean_ref[n * bps + j] = tot / float(MOBA_BLOCK)

    @pl.when(n == pl.num_programs(1) - 1)
    def _():
        gate = jnp.sum(kmean_ref[...] * q_ref[...], axis=-1, keepdims=True)
        blk_id = lax.broadcasted_iota(jnp.int32, gate.shape, 0)
        lane = lax.broadcasted_iota(jnp.int32, (A_HEADS, LANES), 1)
        out = jnp.zeros((A_HEADS, LANES), jnp.int32)
        for t in range(MOBA_TOPK):
            m = jnp.max(gate, axis=0, keepdims=True)
            idx = jnp.min(jnp.where(gate == m, blk_id, nblk), axis=0, keepdims=True)
            out = jnp.where(lane == t, idx[0], out)
            gate = jnp.where(blk_id == idx, -jnp.inf, gate)
        idx_ref[0] = out


def _paged_attn_kernel(pt_ref, top_ref, q_ref, kn_ref, vn_ref, k_hbm, v_hbm, o_ref, kbuf, vbuf, sem, *, layer, ppb):
    s = pl.program_id(0)
    n_pages = MOBA_TOPK * ppb

    def page_copies(step, slot):
        b, h = step // A_HEADS, step % A_HEADS
        cps = []
        for sel in range(MOBA_TOPK):
            blk = top_ref[step * MOBA_TOPK + sel]
            for r in range(ppb):
                page = pt_ref[b, blk * ppb + r]
                i = sel * ppb + r
                cps.append(pltpu.make_async_copy(k_hbm.at[layer, page, :, h, :], kbuf.at[slot, i], sem.at[0, slot]))
                cps.append(pltpu.make_async_copy(v_hbm.at[layer, page, :, h, :], vbuf.at[slot, i], sem.at[1, slot]))
        return cps

    @pl.when(s == 0)
    def _():
        for cp in page_copies(0, 0):
            cp.start()

    @pl.when(s + 1 < pl.num_programs(0))
    def _():
        for cp in page_copies(s + 1, (s + 1) % 2):
            cp.start()

    slot = s % 2
    for cp in page_copies(s, slot):
        cp.wait()

    h = s % A_HEADS
    q = q_ref[0, pl.ds(h, 1), :]
    qh = jnp.broadcast_to(q, (SUBLANES, HEAD_DIM)).astype(BF16)
    s_own = jnp.sum(q * kn_ref[0, pl.ds(h, 1), :], axis=-1, keepdims=True) * ATTN_SCALE
    ss = [lax.dot_general(qh, kbuf[slot, i].astype(BF16), NT_DIMS, preferred_element_type=F32) * ATTN_SCALE
          for i in range(n_pages)]
    m = s_own
    for sc in ss:
        m = jnp.maximum(m, jnp.max(sc[0:1], axis=-1, keepdims=True))
    p_own = jnp.exp(s_own - m)
    l = p_own
    acc = p_own * vn_ref[0, pl.ds(h, 1), :]
    for i, sc in enumerate(ss):
        p = jnp.exp(sc - m)
        l = l + jnp.sum(p[0:1], axis=-1, keepdims=True)
        acc = acc + jnp.dot(p.astype(BF16), vbuf[slot, i].astype(BF16), preferred_element_type=F32)[0:1]
    o_ref[0, pl.ds(h, 1), :] = acc / l


def _paged_moba(q, k_new, v_new, k_pool, v_pool, page_table, layer):
    Bs = q.shape[0]
    n_pages = page_table.shape[1]
    ppb = MOBA_BLOCK // PAGE_SIZE
    assert n_pages % ppb == 0
    nblk = n_pages // ppb
    assert nblk >= MOBA_TOPK
    bps = math.gcd(nblk, 4)
    q3, k3, v3 = (a.reshape(Bs, A_HEADS, HEAD_DIM) for a in (q, k_new, v_new))
    page_blk = (1, 1, PAGE_SIZE, A_HEADS, HEAD_DIM)

    def mean_spec(p):
        return pl.BlockSpec(page_blk, lambda b, n, pt, p=p: (layer, pt[b, n * bps * ppb + p], 0, 0, 0))

    idx = pl.pallas_call(
        functools.partial(_paged_topk_kernel, nblk=nblk, bps=bps, ppb=ppb),
        grid_spec=pltpu.PrefetchScalarGridSpec(
            num_scalar_prefetch=1, grid=(Bs, nblk // bps),
            in_specs=[mean_spec(p) for p in range(bps * ppb)]
            + [pl.BlockSpec((1, A_HEADS, HEAD_DIM), lambda b, n, pt: (b, 0, 0))],
            out_specs=pl.BlockSpec((1, A_HEADS, LANES), lambda b, n, pt: (b, 0, 0)),
            scratch_shapes=[pltpu.VMEM((nblk, A_HEADS, HEAD_DIM), F32)]),
        out_shape=jax.ShapeDtypeStruct((Bs, A_HEADS, LANES), jnp.int32),
        compiler_params=_cparams(("parallel", "arbitrary")), name="paged_topk")(
            page_table, *([k_pool] * (bps * ppb)), q3)
    top = idx[:, :, :MOBA_TOPK].reshape(-1)

    n_sel_pages = MOBA_TOPK * ppb
    seq_spec = pl.BlockSpec((1, A_HEADS, HEAD_DIM), lambda s, pt, tp: (s // A_HEADS, 0, 0))
    hbm_spec = pl.BlockSpec(memory_space=pl.ANY)
    att = pl.pallas_call(
        functools.partial(_paged_attn_kernel, layer=layer, ppb=ppb),
        grid_spec=pltpu.PrefetchScalarGridSpec(
            num_scalar_prefetch=2, grid=(Bs * A_HEADS,),
            in_specs=[seq_spec, seq_spec, seq_spec, hbm_spec, hbm_spec],
            out_specs=seq_spec,
            scratch_shapes=[pltpu.VMEM((2, n_sel_pages, PAGE_SIZE, HEAD_DIM), F32),
                            pltpu.VMEM((2, n_sel_pages, PAGE_SIZE, HEAD_DIM), F32),
                            pltpu.SemaphoreType.DMA((2, 2))]),
        out_shape=jax.ShapeDtypeStruct((Bs, A_HEADS, HEAD_DIM), F32),
        compiler_params=_cparams(("arbitrary",)), name="paged_attn")(
            page_table, top, q3, k3, v3, k_pool, v_pool)
    return att.reshape(Bs, A_W).astype(BF16)


def _rope_tables(pos):
    inv = ROPE_THETA ** (-jnp.arange(ROT_HALF, dtype=F32) / ROT_HALF)
    ang = pos.astype(F32)[:, None] * inv
    cos, sin = jnp.cos(ang), jnp.sin(ang)
    T = pos.shape[0]
    rest = HEAD_DIM - ROT_DIM
    cos_t = jnp.concatenate([cos, cos, jnp.ones((T, rest), F32)], axis=-1)
    sin_t = jnp.concatenate([-sin, sin, jnp.zeros((T, rest), F32)], axis=-1)
    return cos_t, sin_t


ROW_TILE = 1024


def _row_tile(m):
    return ROW_TILE if m % ROW_TILE == 0 else m


def kernel(x_prompt, x_sample, cache_k, cache_v, state_lru_conv, state_lru_h, state_ssd_conv, state_ssd_h, page_table, norm_mix, norm_mlp, w_in_a, q_norm, k_norm, lru_conv_w, lru_conv_b, lru_wa, lru_ba, lru_wx, lru_bx, lru_lambda, w_out_a, w_in_s, ssd_conv_w, ssd_conv_b, ssd_dt_bias, ssd_a_log, ssd_d, ssd_norm, w_out_s, w_up, w_down):
    Bp, Tp, D = x_prompt.shape
    Bs, Ts, _ = x_sample.shape
    assert Ts == 1
    depth = norm_mix.shape[0]
    past_len = page_table.shape[1] * PAGE_SIZE
    Mp, Ms = Bp * Tp, Bs * Ts
    tmp, tms = _row_tile(Mp), _row_tile(Ms)
    assert Tp % tmp == 0
    xp = x_prompt.reshape(Mp, D)
    xs = x_sample.reshape(Ms, D)
    lru_w = lru_lambda.shape[1]
    d_inner = ssd_norm.shape[1]
    xbc_w = ssd_conv_w.shape[2]
    n_ssd_heads = ssd_d.shape[1]

    cos_p, sin_p = _rope_tables(jnp.arange(Tp, dtype=jnp.int32))
    cos_s, sin_s = _rope_tables(jnp.full((Ms,), past_len, dtype=jnp.int32))

    outs = {k: [] for k in ("kp", "vp", "ks", "vs", "lcp", "lcs", "lhp", "lhs", "scp", "scs", "shp", "shs")}
    for layer in range(depth):
        i = layer // 2
        if layer % 2 == 0:
            w_in = w_in_a[i].astype(BF16)
            w_out = w_out_a[i].astype(BF16)
            splits = [A_W, A_W, A_W, lru_w, lru_w]
            lru_args = (lru_conv_w[i], lru_conv_b[i], lru_wa[i], lru_ba[i], lru_wx[i], lru_bx[i], lru_lambda[i])
            q, k, v, xr, yg = _norm_proj(xp, norm_mix[layer], w_in, splits, tm=tmp,
                                         rope=((q_norm[i], k_norm[i]), cos_p, sin_p, Tp // tmp))
            att = _moba_prompt(q, k, v, batch=Bp, seq=Tp)
            rec, h_last = _lru_prompt(xr, yg, *lru_args, batch=Bp, seq=Tp)
            mixed_p = _out_proj([att, rec], w_out, xp, tm=tmp)
            outs["kp"].append(k.reshape(Bp, Tp, A_HEADS, HEAD_DIM))
            outs["vp"].append(v.reshape(Bp, Tp, A_HEADS, HEAD_DIM))
            outs["lcp"].append(xr.reshape(Bp, Tp, lru_w)[:, Tp - (CONV_K - 1):])
            outs["lhp"].append(h_last.reshape(Bp, lru_w))
            q, k, v, xr, yg = _norm_proj(xs, norm_mix[layer], w_in, splits, tm=tms,
                                         rope=((q_norm[i], k_norm[i]), cos_s, sin_s, 1))
            att = _paged_moba(q, k, v, cache_k, cache_v, page_table, layer=i)
            buf = state_lru_conv[i]
            rec, h_new = _lru_sample(xr, yg, buf.swapaxes(0, 1), state_lru_h[i], *lru_args)
            mixed_s = _out_proj([att, rec], w_out, xs, tm=tms)
            outs["ks"].append(k.reshape(Bs, Ts, A_HEADS, HEAD_DIM))
            outs["vs"].append(v.reshape(Bs, Ts, A_HEADS, HEAD_DIM))
            outs["lcs"].append(jnp.concatenate([buf[:, 1:], xr[:, None, :]], axis=1))
            outs["lhs"].append(h_new)
        else:
            main_w = d_inner + xbc_w
            w_in = w_in_s[i].astype(BF16)
            w_dt = jnp.pad(w_in_s[i][:, main_w:], ((0, 0), (0, LANES - n_ssd_heads))).astype(BF16)
            w_out = w_out_s[i].astype(BF16)
            ssd_args = (ssd_conv_w[i], ssd_conv_b[i], ssd_dt_bias[i], ssd_a_log[i], ssd_d[i], ssd_norm[i])
            unslab = lambda a: a.swapaxes(0, 1).reshape(a.shape[1], -1)
            z, xbc, dt = _norm_proj(xp, norm_mix[layer], w_in, [d_inner, xbc_w], tm=tmp, extra_w=w_dt, slab=True)
            yn, s_last = _ssd_prompt(xbc, z, dt, *ssd_args, batch=Bp, seq=Tp)
            mixed_p = _out_proj([yn], w_out, xp, tm=tmp)
            tail = xbc.reshape(-1, Bp, Tp, LANES)[:, :, Tp - (CONV_K - 1):]
            outs["scp"].append(tail.transpose(1, 2, 0, 3).reshape(Bp, CONV_K - 1, xbc_w))
            outs["shp"].append(s_last.reshape(Bp, n_ssd_heads, SSD_HEAD_DIM, SSD_STATE))
            z, xbc, dt = _norm_proj(xs, norm_mix[layer], w_in, [d_inner, xbc_w], tm=tms, extra_w=w_dt, slab=True)
            z, xbc = unslab(z), unslab(xbc)
            buf = state_ssd_conv[i]
            yn, s_new = _ssd_sample(xbc, buf.swapaxes(0, 1), z, dt,
                                    state_ssd_h.reshape((-1,) + state_ssd_h.shape[2:]), *ssd_args, s_off=i * Bs)
            mixed_s = _out_proj([yn], w_out, xs, tm=tms)
            outs["scs"].append(jnp.concatenate([buf[:, 1:], xbc[:, None, :]], axis=1))
            outs["shs"].append(s_new)
        wu, wd = w_up[layer].astype(BF16), w_down[layer].astype(BF16)
        xp = _mlp(mixed_p, norm_mlp[layer], wu, wd, tm=tmp)
        xs = _mlp(mixed_s, norm_mlp[layer], wu, wd, tm=tms)

    st = lambda key: jnp.stack(outs[key])
    return (xp.reshape(Bp, Tp, D), xs.reshape(Bs, Ts, D), st("kp"), st("vp"), st("ks"), st("vs"),
            st("lcp"), st("lcs"), st("lhp"), st("lhs"), st("scp"), st("scs"), st("shp"), st("shs"))
```

```python
import functools
import math

import jax
import jax.numpy as jnp
from jax import lax
from jax.experimental import pallas as pl
from jax.experimental.pallas import tpu as pltpu

F32 = jnp.float32
BF16 = jnp.bfloat16
HIGHEST = lax.Precision.HIGHEST

A_HEADS = 8
HEAD_DIM = 128
A_W = A_HEADS * HEAD_DIM
ROT_DIM = HEAD_DIM // 4
ROT_HALF = ROT_DIM // 2
ROPE_THETA = 500000.0
MOBA_BLOCK = 256
MOBA_TOPK = 3
ATTN_SCALE = HEAD_DIM ** -0.5
LRU_HEADS = 8
LRU_C = 8.0
CONV_K = 4
SSD_HEAD_DIM = 64
SSD_STATE = 128
SSD_GROUPS = 8
SSD_CHUNK = 128
PAGE_SIZE = 128
EPS = 1e-6

LANES = 128
SUBLANES = 8
NEG_BIG = -1e30
LOG2E = math.log2(math.e)
VMEM_BYTES_V7X = 64 * 1024 * 1024
VMEM_LIMIT = VMEM_BYTES_V7X - 8 * 1024 * 1024
VMEM_LIMIT_MLP = VMEM_BYTES_V7X - 4 * 1024 * 1024

NT_DIMS = (((1,), (1,)), ((), ()))


def _cparams(sem, vmem=VMEM_LIMIT):
    return pltpu.CompilerParams(dimension_semantics=sem, vmem_limit_bytes=vmem)


def _sigmoid(x):
    return 0.5 * (jnp.tanh(0.5 * x) + 1.0)


def _silu(x):
    h = 0.5 * x
    return h * (jnp.tanh(h) + 1.0)


def _rms_rows(x, g):
    return x * lax.rsqrt(jnp.mean(x * x, axis=-1, keepdims=True) + EPS) * g


def _head_norm_rope(acc, g, cos, sin):
    y = _rms_rows(acc, g)
    lane = lax.broadcasted_iota(jnp.int32, y.shape, 1)
    partner = jnp.where(lane < ROT_HALF, pltpu.roll(y, LANES - ROT_HALF, 1), pltpu.roll(y, ROT_HALF, 1))
    return y * cos + partner * sin


def _norm_proj_kernel(*refs, ranges, n_rope, has_extra, tn, slab):
    it = iter(refs)
    x_ref, g_ref, w_ref = next(it), next(it), next(it)
    rope_g = [next(it) for _ in range(n_rope)]
    if n_rope:
        cos_ref, sin_ref = next(it), next(it)
    if has_extra:
        we_ref = next(it)
    outs = [next(it) for _ in ranges]
    if has_extra:
        oe_ref = next(it)
    xn_ref = next(it)
    j = pl.program_id(1)

    @pl.when(j == 0)
    def _():
        xn_ref[...] = _rms_rows(x_ref[...], g_ref[...]).astype(BF16)
        if has_extra:
            oe_ref[...] = jnp.dot(xn_ref[...], we_ref[...], preferred_element_type=F32)

    acc = jnp.dot(xn_ref[...], w_ref[...], preferred_element_type=F32)
    for s, (a, b) in enumerate(ranges):
        @pl.when((j >= a) & (j < b))
        def _(s=s):
            if s < n_rope:
                cos, sin = cos_ref[...], sin_ref[...]
                for hh in range(tn // HEAD_DIM):
                    sl = slice(hh * HEAD_DIM, (hh + 1) * HEAD_DIM)
                    outs[s][:, sl] = _head_norm_rope(acc[:, sl], rope_g[s][...], cos, sin)
            elif slab:
                for cblk in range(tn // LANES):
                    outs[s][cblk] = acc[:, cblk * LANES:(cblk + 1) * LANES]
            else:
                outs[s][...] = acc


def _norm_proj(x, g, w, splits, *, tm, tn=512, rope=None, extra_w=None, slab=False):
    M, K = x.shape
    N = sum(splits)
    assert w.shape[0] == K and w.shape[1] >= N and M % tm == 0 and all(s % tn == 0 for s in splits)
    edges = [0]
    for s in splits:
        edges.append(edges[-1] + s // tn)
    ranges = tuple((edges[i], edges[i + 1]) for i in range(len(splits)))
    n_rope = 0 if rope is None else len(rope[0])
    in_specs = [pl.BlockSpec((tm, K), lambda i, j: (i, 0)),
                pl.BlockSpec((1, K), lambda i, j: (0, 0)),
                pl.BlockSpec((K, tn), lambda i, j: (0, j))]
    args = [x, g.reshape(1, K), w]
    if rope is not None:
        gains, cos, sin, period = rope
        for gg in gains:
            in_specs.append(pl.BlockSpec((1, HEAD_DIM), lambda i, j: (0, 0)))
            args.append(gg.reshape(1, HEAD_DIM))
        for tab in (cos, sin):
            in_specs.append(pl.BlockSpec((tm, HEAD_DIM), lambda i, j, period=period: (i % period, 0)))
            args.append(tab)
    if extra_w is not None:
        in_specs.append(pl.BlockSpec((K, LANES), lambda i, j: (0, 0)))
        args.append(extra_w)
    out_specs, out_shape = [], []
    for (a, b), s in zip(ranges, splits):
        if slab:
            out_specs.append(pl.BlockSpec((tn // LANES, tm, LANES),
                                          lambda i, j, a=a, b=b: (jnp.clip(j - a, 0, b - a - 1), i, 0)))
            out_shape.append(jax.ShapeDtypeStruct((s // LANES, M, LANES), F32))
        else:
            out_specs.append(pl.BlockSpec((tm, tn), lambda i, j, a=a, b=b: (i, jnp.clip(j - a, 0, b - a - 1))))
            out_shape.append(jax.ShapeDtypeStruct((M, s), F32))
    if extra_w is not None:
        out_specs.append(pl.BlockSpec((tm, LANES), lambda i, j: (i, 0)))
        out_shape.append(jax.ShapeDtypeStruct((M, LANES), F32))
    kern = functools.partial(_norm_proj_kernel, ranges=ranges, n_rope=n_rope,
                             has_extra=extra_w is not None, tn=tn, slab=slab)
    return pl.pallas_call(
        kern, grid=(M // tm, N // tn), in_specs=in_specs, out_specs=out_specs, out_shape=out_shape,
        scratch_shapes=[pltpu.VMEM((tm, K), BF16)],
        compiler_params=_cparams(("parallel", "arbitrary")), name="norm_proj")(*args)


def _norm_proj_resident_kernel(*refs, splits, n_rope):
    it = iter(refs)
    x_ref, g_ref, w_ref = next(it), next(it), next(it)
    rope_g = [next(it) for _ in range(n_rope)]
    if n_rope:
        cos_ref, sin_ref = next(it), next(it)
    outs = [next(it) for _ in splits]
    xn = _rms_rows(x_ref[...], g_ref[...]).astype(BF16)
    col = 0
    for s, width in enumerate(splits):
        acc = jnp.dot(xn, w_ref[:, col:col + width], preferred_element_type=F32)
        if s < n_rope:
            for hh in range(width // HEAD_DIM):
                sl = slice(hh * HEAD_DIM, (hh + 1) * HEAD_DIM)
                outs[s][:, sl] = _head_norm_rope(acc[:, sl], rope_g[s][...], cos_ref[...], sin_ref[...])
        else:
            outs[s][...] = acc
        col += width


def _norm_proj_resident(x, g, w, splits, *, tm, rope=None):
    M, K = x.shape
    N = sum(splits)
    assert w.shape == (K, N) and M % tm == 0
    n_rope = 0 if rope is None else len(rope[0])
    in_specs = [pl.BlockSpec((tm, K), lambda i: (i, 0)),
                pl.BlockSpec((1, K), lambda i: (0, 0)),
                pl.BlockSpec((K, N), lambda i: (0, 0), pipeline_mode=pl.Buffered(1))]
    args = [x, g.reshape(1, K), w]
    if rope is not None:
        gains, cos, sin, period = rope
        for gg in gains:
            in_specs.append(pl.BlockSpec((1, HEAD_DIM), lambda i: (0, 0)))
            args.append(gg.reshape(1, HEAD_DIM))
        for tab in (cos, sin):
            in_specs.append(pl.BlockSpec((tm, HEAD_DIM), lambda i, period=period: (i % period, 0)))
            args.append(tab)
    return pl.pallas_call(
        functools.partial(_norm_proj_resident_kernel, splits=tuple(splits), n_rope=n_rope),
        grid=(M // tm,), in_specs=in_specs,
        out_specs=[pl.BlockSpec((tm, s), lambda i: (i, 0)) for s in splits],
        out_shape=[jax.ShapeDtypeStruct((M, s), F32) for s in splits],
        compiler_params=_cparams(("parallel",)), name="norm_proj_resident")(*args)


def _out_proj_kernel(*refs, n_parts, kp):
    xs = refs[:n_parts]
    w_ref, res_ref, o_ref = refs[n_parts:]
    acc = res_ref[...]
    for p, x_ref in enumerate(xs):
        acc = acc + jnp.dot(x_ref[...], w_ref[p * kp:(p + 1) * kp, :], preferred_element_type=F32)
    o_ref[...] = acc


def _out_proj(parts, w, res, *, tm):
    M, N = res.shape
    kp = parts[0].shape[1]
    assert all(p.shape == (M, kp) for p in parts) and w.shape == (kp * len(parts), N) and M % tm == 0
    in_specs = [pl.BlockSpec((tm, kp), lambda i: (i, 0)) for _ in parts]
    in_specs.append(pl.BlockSpec(w.shape, lambda i: (0, 0), pipeline_mode=pl.Buffered(1)))
    in_specs.append(pl.BlockSpec((tm, N), lambda i: (i, 0)))
    return pl.pallas_call(
        functools.partial(_out_proj_kernel, n_parts=len(parts), kp=kp),
        grid=(M // tm,), in_specs=in_specs,
        out_specs=pl.BlockSpec((tm, N), lambda i: (i, 0)),
        out_shape=jax.ShapeDtypeStruct((M, N), F32),
        compiler_params=_cparams(("parallel",)), name="out_proj")(*parts, w, res)


def _mlp_kernel(x_ref, g_ref, wu_ref, wd_ref, o_ref, xn_ref):
    f = pl.program_id(1)

    @pl.when(f == 0)
    def _():
        x = x_ref[...]
        xn_ref[...] = _rms_rows(x, g_ref[...]).astype(BF16)
        o_ref[...] = x

    h = jnp.maximum(jnp.dot(xn_ref[...], wu_ref[...], preferred_element_type=F32), 0.0)
    o_ref[...] += jnp.dot((h * h).astype(BF16), wd_ref[...], preferred_element_type=F32)


def _mlp(x, g, w_up, w_down, *, tm, tf=1024):
    M, D = x.shape
    FF = w_up.shape[1]
    return pl.pallas_call(
        _mlp_kernel, grid=(M // tm, FF // tf),
        in_specs=[pl.BlockSpec((tm, D), lambda i, f: (i, 0)),
                  pl.BlockSpec((1, D), lambda i, f: (0, 0)),
                  pl.BlockSpec((D, tf), lambda i, f: (0, f)),
                  pl.BlockSpec((tf, D), lambda i, f: (f, 0))],
        out_specs=pl.BlockSpec((tm, D), lambda i, f: (i, 0)),
        out_shape=jax.ShapeDtypeStruct((M, D), F32),
        scratch_shapes=[pltpu.VMEM((tm, D), BF16)],
        compiler_params=_cparams(("parallel", "arbitrary"), VMEM_LIMIT_MLP), name="mlp")(
            x, g.reshape(1, D), w_up, w_down)


def _moba_prompt_kernel(q_ref, k_ref, v_ref, hot_ref, o_ref, kaug_ref, vb_ref, kmean_ref, *, nb):
    blk, D = MOBA_BLOCK, HEAD_DIM
    kaug_ref[:, 0:D] = k_ref[...].astype(BF16)
    kaug_ref[:, D:2 * D] = hot_ref[...]
    vb_ref[...] = v_ref[...].astype(BF16)
    kmean_ref[...] = jnp.zeros(kmean_ref.shape, F32)
    for n in range(nb):
        kmean_ref[n:n + 1, :] = jnp.mean(k_ref[n * blk:(n + 1) * blk, :], axis=0, keepdims=True)

    lane = lax.broadcasted_iota(jnp.int32, (blk, LANES), 1)
    row = lax.broadcasted_iota(jnp.int32, (blk, blk), 0)
    col = lax.broadcasted_iota(jnp.int32, (blk, blk), 1)
    nbp = kmean_ref.shape[0]
    blk_id = lax.broadcasted_iota(jnp.int32, (nbp, blk), 0)
    eye = (row == col).astype(BF16)
    for qb in range(nb):
        q = q_ref[qb * blk:(qb + 1) * blk, :]
        if qb > MOBA_TOPK:
            gate = lax.dot_general(kmean_ref[...], q, NT_DIMS, precision=HIGHEST, preferred_element_type=F32)
            gate = jnp.where(blk_id < qb, gate, -jnp.inf)
            bias_t = jnp.where(blk_id == qb, 0.0, NEG_BIG)
            for _ in range(MOBA_TOPK):
                m = jnp.max(gate, axis=0, keepdims=True)
                idx = jnp.min(jnp.where(gate == m, blk_id, nbp), axis=0, keepdims=True)
                hit = blk_id == idx
                bias_t = jnp.where(hit & (m > -jnp.inf), 0.0, bias_t)
                gate = jnp.where(hit, -jnp.inf, gate)
            pad = jnp.zeros((LANES - nbp, blk), F32)
            bias = lax.dot_general(eye, jnp.concatenate([bias_t, pad], axis=0).astype(BF16), NT_DIMS,
                                   preferred_element_type=F32).astype(BF16)
        else:
            bias = jnp.where(lane <= qb, 0.0, NEG_BIG).astype(BF16)
        lhs = jnp.concatenate([(q * (ATTN_SCALE * LOG2E)).astype(BF16), bias], axis=1)
        own = slice(qb * blk, (qb + 1) * blk)
        s_own = lax.dot_general(lhs, kaug_ref[own, :], NT_DIMS, preferred_element_type=F32)
        s_own = jnp.where(col <= row, s_own, NEG_BIG)
        m = jnp.max(s_own, axis=-1, keepdims=True)
        if qb > 0:
            s_past = lax.dot_general(lhs, kaug_ref[0:qb * blk, :], NT_DIMS, preferred_element_type=F32)
            m = jnp.maximum(m, jnp.max(s_past, axis=-1, keepdims=True))
        p = jnp.exp2(s_own - m)
        l = jnp.sum(p, axis=-1, keepdims=True)
        acc = jnp.dot(p.astype(BF16), vb_ref[own, :], preferred_element_type=F32)
        if qb > 0:
            p = jnp.exp2(s_past - m)
            l = l + jnp.sum(p, axis=-1, keepdims=True)
            acc = acc + jnp.dot(p.astype(BF16), vb_ref[0:qb * blk, :], preferred_element_type=F32)
        o_ref[own, :] = (acc / l).astype(o_ref.dtype)


def _moba_prompt(q, k, v, *, batch, seq):
    M = q.shape[0]
    nb = seq // MOBA_BLOCK
    assert nb <= LANES and seq % MOBA_BLOCK == 0
    hot = (lax.broadcasted_iota(jnp.int32, (seq, LANES), 0) // MOBA_BLOCK
           == lax.broadcasted_iota(jnp.int32, (seq, LANES), 1)).astype(BF16)
    seq_head = pl.BlockSpec((seq, HEAD_DIM), lambda b, h: (b, h))
    return pl.pallas_call(
        functools.partial(_moba_prompt_kernel, nb=nb),
        grid=(batch, A_HEADS),
        in_specs=[seq_head, seq_head, seq_head, pl.BlockSpec((seq, LANES), lambda b, h: (0, 0))],
        out_specs=seq_head,
        out_shape=jax.ShapeDtypeStruct((M, A_W), BF16),
        scratch_shapes=[pltpu.VMEM((seq, 2 * HEAD_DIM), BF16), pltpu.VMEM((seq, HEAD_DIM), BF16),
                        pltpu.VMEM((-(-nb // SUBLANES) * SUBLANES, HEAD_DIM), F32)],
        compiler_params=_cparams(("parallel", "parallel")), name="moba_prompt")(q, k, v, hot)


def _lru_gates(xc, wa_ref, ba, wx_ref, bx, lam):
    bw = xc.shape[1] // LRU_HEADS
    rs, is_ = [], []
    for n in range(LRU_HEADS):
        xb = xc[:, n * bw:(n + 1) * bw].astype(BF16)
        rs.append(jnp.dot(xb, wa_ref[n], preferred_element_type=F32))
        is_.append(jnp.dot(xb, wx_ref[n], preferred_element_type=F32))
    r = _sigmoid(jnp.concatenate(rs, axis=-1) + ba)
    i = _sigmoid(jnp.concatenate(is_, axis=-1) + bx)
    log_a = -LRU_C * r * jax.nn.softplus(-lam)
    a = jnp.exp(log_a)
    t = jnp.tanh(log_a)
    u = jnp.sqrt(-2.0 * t / (1.0 - t)) * (i * xc)
    return a, u


def _lru_prompt_kernel(xr_ref, yg_ref, cw_ref, cb_ref, wa_ref, ba_ref, wx_ref, bx_ref, lam_ref,
                       rec_ref, hl_ref, xx_ref, a_ref, u_ref, h_ref, *, tt):
    ti = pl.program_id(1)
    pad = SUBLANES

    @pl.when(ti == 0)
    def _():
        xx_ref[0:pad, :] = jnp.zeros((pad, xx_ref.shape[1]), F32)
        h_ref[...] = jnp.zeros(h_ref.shape, F32)

    @pl.when(ti > 0)
    def _():
        xx_ref[0:pad, :] = xx_ref[tt:tt + pad, :]

    xx_ref[pad:pad + tt, :] = xr_ref[...]
    xc = cb_ref[...]
    for j in range(CONV_K):
        off = pad - (CONV_K - 1) + j
        xc = xc + xx_ref[off:off + tt, :] * cw_ref[j:j + 1, :]
    a, u = _lru_gates(xc, wa_ref, ba_ref[...], wx_ref, bx_ref[...], lam_ref[...])
    a_ref[...] = a
    u_ref[...] = u

    sub = lax.broadcasted_iota(jnp.int32, (SUBLANES, a.shape[1]), 0)

    def group(gi, h):
        r0 = pl.multiple_of(gi * SUBLANES, SUBLANES)
        a8 = a_ref[pl.ds(r0, SUBLANES), :]
        u8 = u_ref[pl.ds(r0, SUBLANES), :]
        for s in (1, 2, 4):
            keep = sub >= s
            a_sh = pltpu.roll(a8, s, 0)
            u_sh = pltpu.roll(u8, s, 0)
            u8 = jnp.where(keep, a8 * u_sh + u8, u8)
            a8 = jnp.where(keep, a8 * a_sh, a8)
        hs = a8 * h + u8
        yg = yg_ref[pl.ds(r0, SUBLANES), :]
        rec_ref[pl.ds(r0, SUBLANES), :] = (hs * jax.nn.gelu(yg)).astype(rec_ref.dtype)
        return hs[SUBLANES - 1:SUBLANES, :]

    h = lax.fori_loop(0, tt // SUBLANES, group, h_ref[...])
    h_ref[...] = h

    @pl.when(ti == pl.num_programs(1) - 1)
    def _():
        hl_ref[0] = h


def _lru_prompt(xr, yg, cw, cb, wa, ba, wx, bx, lam, *, batch, seq, tt=256):
    M, W = xr.shape
    nt = seq // tt
    row = lambda a: a.reshape(1, W)
    full2 = lambda shp: pl.BlockSpec(shp, lambda b, t: (0, 0))
    full3 = lambda shp: pl.BlockSpec(shp, lambda b, t: (0, 0, 0))
    return pl.pallas_call(
        functools.partial(_lru_prompt_kernel, tt=tt),
        grid=(batch, nt),
        in_specs=[pl.BlockSpec((tt, W), lambda b, t: (b * nt + t, 0)),
                  pl.BlockSpec((tt, W), lambda b, t: (b * nt + t, 0)),
                  full2((CONV_K, W)), full2((1, W)), full3(wa.shape), full2((1, W)),
                  full3(wx.shape), full2((1, W)), full2((1, W))],
        out_specs=[pl.BlockSpec((tt, W), lambda b, t: (b * nt + t, 0)),
                   pl.BlockSpec((1, 1, W), lambda b, t: (b, 0, 0))],
        out_shape=[jax.ShapeDtypeStruct((M, W), BF16), jax.ShapeDtypeStruct((batch, 1, W), F32)],
        scratch_shapes=[pltpu.VMEM((tt + SUBLANES, W), F32), pltpu.VMEM((tt, W), F32),
                        pltpu.VMEM((tt, W), F32), pltpu.VMEM((1, W), F32)],
        compiler_params=_cparams(("parallel", "arbitrary")), name="lru_prompt")(
            xr, yg, cw, row(cb), wa.astype(BF16), row(ba), wx.astype(BF16), row(bx), row(lam))


def _lru_sample_kernel(xr_ref, yg_ref, buf_ref, h0_ref, cw_ref, cb_ref, wa_ref, ba_ref, wx_ref, bx_ref,
                       lam_ref, rec_ref, hn_ref):
    xc = cb_ref[...]
    for j in range(CONV_K - 1):
        xc = xc + buf_ref[j] * cw_ref[j:j + 1, :]
    xc = xc + xr_ref[...] * cw_ref[CONV_K - 1:CONV_K, :]
    a, u = _lru_gates(xc, wa_ref, ba_ref[...], wx_ref, bx_ref[...], lam_ref[...])
    h = a * h0_ref[...] + u
    hn_ref[...] = h
    rec_ref[...] = (h * jax.nn.gelu(yg_ref[...])).astype(rec_ref.dtype)


def _lru_sample(xr, yg, buf_t, h0, cw, cb, wa, ba, wx, bx, lam):
    Bs, W = xr.shape
    row = lambda a: a.reshape(1, W)
    return pl.pallas_call(
        _lru_sample_kernel,
        out_shape=[jax.ShapeDtypeStruct((Bs, W), BF16), jax.ShapeDtypeStruct((Bs, W), F32)],
        compiler_params=pltpu.CompilerParams(vmem_limit_bytes=VMEM_LIMIT), name="lru_sample")(
            xr, yg, buf_t, h0, cw, row(cb), wa.astype(BF16), row(ba), wx.astype(BF16), row(bx), row(lam))


def _ssd_prompt_kernel(xbc_ref, prev_ref, z_ref, dt_ref, cw_ref, cb_ref, dtb_ref, alog_ref, dsk_ref, ng_ref,
                       ex_ref, yn_ref, sl_ref, xx_ref, xcv_ref, y_ref, st_ref, *, d_inner):
    c = pl.program_id(1)
    L = SSD_CHUNK
    pad = SUBLANES
    gw = d_inner // SSD_GROUPS
    spg = gw // LANES
    hps = LANES // SSD_HEAD_DIM
    head_of_lane = lax.broadcasted_iota(jnp.int32, (L, LANES), 1) // SSD_HEAD_DIM
    assert SSD_STATE == LANES

    @pl.when(c == 0)
    def _():
        xx_ref[:, 0:pad, :] = jnp.zeros((xx_ref.shape[0], pad, LANES), F32)
        st_ref[...] = jnp.zeros(st_ref.shape, F32)

    @pl.when(c > 0)
    def _():
        xx_ref[:, 0:pad, :] = prev_ref[...]

    xx_ref[:, pad:pad + L, :] = xbc_ref[...]

    def conv_slab(k, carry):
        xs = xx_ref[k]
        xc = cb_ref[k]
        for j in range(CONV_K):
            back = CONV_K - 1 - j
            tap = xs if back == 0 else pltpu.roll(xs, back, 0)
            xc = xc + tap[pad:] * cw_ref[k, j:j + 1, :]
        xcv_ref[k] = _silu(xc)
        return carry

    lax.fori_loop(0, xx_ref.shape[0], conv_slab, 0)

    def slabs(ref, first, count):
        return jnp.concatenate([ref[first + i] for i in range(count)], axis=1)

    def split2(a):
        hi = a.astype(BF16)
        return hi, (a - hi.astype(F32)).astype(BF16)

    dt = jax.nn.softplus(dt_ref[...] + dtb_ref[...])
    dta = dt * (-jnp.exp(alog_ref[...]))
    tril = lax.broadcasted_iota(jnp.int32, (L, L), 1) <= lax.broadcasted_iota(jnp.int32, (L, L), 0)
    cs = jnp.dot(tril.astype(F32), dta, precision=HIGHEST, preferred_element_type=F32)
    cs_l2 = cs * LOG2E
    adj_t = (cs_l2 - jnp.log2(dt)).T
    last = cs[L - 1:L, :]
    w_hi, w_lo = split2(dt * jnp.exp(last - cs))
    e_hi, e_lo = split2(jnp.exp(cs))
    cdec = jnp.broadcast_to(jnp.exp(last), (SUBLANES, LANES))
    cd_hi, cd_rest = split2(cdec)
    cd_mid, cd_lo = split2(cdec - cd_hi.astype(F32))

    def expand(terms, exg):
        out = jnp.dot(terms[0], exg, preferred_element_type=F32)
        for t in terms[1:]:
            out = out + jnp.dot(t, exg, preferred_element_type=F32)
        return out

    for g in range(SSD_GROUPS):
        lo = g * gw
        exg = ex_ref[:, lo:lo + gw]
        w_x = expand((w_hi, w_lo), exg)
        ecs_x = expand((e_hi, e_lo), exg)
        cd_x = expand((cd_hi, cd_mid, cd_lo), exg)[0:1]
        xg = slabs(xcv_ref, g * spg, spg)
        bg = xcv_ref[d_inner // LANES + g]
        cg = xcv_ref[d_inner // LANES + SSD_GROUPS + g].astype(BF16)
        bg_t = bg.T.astype(BF16)
        cb = jnp.dot(cg, bg_t, preferred_element_type=F32)
        st_g = st_ref[:, lo:lo + gw]
        y_off = jnp.dot(cg, st_g.astype(BF16), preferred_element_type=F32) * ecs_x
        st_ref[:, lo:lo + gw] = st_g * cd_x + jnp.dot(bg_t, (xg * w_x).astype(BF16), preferred_element_type=F32)
        for i in range(spg):
            xs_h = xcv_ref[g * spg + i].astype(BF16)
            lhs, rhs = [], []
            for k in range(hps):
                h = (g * spg + i) * hps + k
                dec = jnp.exp2(jnp.where(tril, cs_l2[:, h:h + 1] - adj_t[h:h + 1, :], NEG_BIG))
                lhs.append((cb * dec).astype(BF16))
                rhs.append(jnp.where(head_of_lane == k, xs_h, jnp.zeros_like(xs_h)))
            yd = jnp.dot(jnp.concatenate(lhs, axis=1), jnp.concatenate(rhs, axis=0), preferred_element_type=F32)
            cols = slice(lo + i * LANES, lo + (i + 1) * LANES)
            y_ref[:, cols] = yd + y_off[:, i * LANES:(i + 1) * LANES]

    for g in range(SSD_GROUPS):
        lo = g * gw
        cols = slice(lo, lo + gw)
        y = y_ref[:, cols] + slabs(xcv_ref, g * spg, spg) * dsk_ref[:, cols]
        zz = slabs(z_ref, g * spg, spg)
        u = y * _silu(zz)
        yn_ref[:, cols] = _rms_rows(u, ng_ref[:, cols]).astype(yn_ref.dtype)

    @pl.when(c == pl.num_programs(1) - 1)
    def _():
        for k in range(d_inner // LANES):
            sl_ref[0, k * LANES:(k + 1) * LANES, :] = st_ref[:, k * LANES:(k + 1) * LANES].T


def _ssd_expand_matrix(n_heads, d_inner):
    h = lax.broadcasted_iota(jnp.int32, (LANES, d_inner), 0)
    col = lax.broadcasted_iota(jnp.int32, (LANES, d_inner), 1)
    return (col // SSD_HEAD_DIM == h).astype(F32) * (h < n_heads).astype(F32)


def _pad_lanes(v):
    return jnp.pad(v, (0, LANES - v.shape[0])).reshape(1, LANES)


def _ssd_prompt(xbc, z, dt, cw, cb, dt_bias, a_log, d_skip, norm_g, *, batch, seq):
    nxs, M, _ = xbc.shape
    nzs = z.shape[0]
    XW, d_inner = nxs * LANES, nzs * LANES
    n_heads = d_inner // SSD_HEAD_DIM
    L = SSD_CHUNK
    nc = seq // L
    rpb = L // SUBLANES
    const2 = lambda shp: pl.BlockSpec(shp, lambda b, c: (0, 0))
    const3 = lambda shp: pl.BlockSpec(shp, lambda b, c: (0, 0, 0))
    return pl.pallas_call(
        functools.partial(_ssd_prompt_kernel, d_inner=d_inner),
        grid=(batch, nc),
        in_specs=[pl.BlockSpec((nxs, L, LANES), lambda b, c: (0, b * nc + c, 0)),
                  pl.BlockSpec((nxs, SUBLANES, LANES), lambda b, c: (0, jnp.maximum((b * nc + c) * rpb - 1, 0), 0)),
                  pl.BlockSpec((nzs, L, LANES), lambda b, c: (0, b * nc + c, 0)),
                  pl.BlockSpec((L, LANES), lambda b, c: (b * nc + c, 0)),
                  const3((nxs, CONV_K, LANES)), const3((nxs, 1, LANES)), const2((1, LANES)), const2((1, LANES)),
                  const2((1, d_inner)), const2((1, d_inner)), const2((LANES, d_inner))],
        out_specs=[pl.BlockSpec((L, d_inner), lambda b, c: (b * nc + c, 0)),
                   pl.BlockSpec((1, d_inner, SSD_STATE), lambda b, c: (b, 0, 0))],
        out_shape=[jax.ShapeDtypeStruct((M, d_inner), BF16),
                   jax.ShapeDtypeStruct((batch, d_inner, SSD_STATE), F32)],
        scratch_shapes=[pltpu.VMEM((nxs, L + SUBLANES, LANES), F32), pltpu.VMEM((nxs, L, LANES), F32),
                        pltpu.VMEM((L, d_inner), F32), pltpu.VMEM((SSD_STATE, d_inner), F32)],
        compiler_params=_cparams(("parallel", "arbitrary")), name="ssd_prompt")(
            xbc, xbc, z, dt, cw.reshape(CONV_K, nxs, LANES).swapaxes(0, 1), cb.reshape(nxs, 1, LANES),
            _pad_lanes(dt_bias), _pad_lanes(a_log),
            jnp.repeat(d_skip, SSD_HEAD_DIM).reshape(1, d_inner), norm_g.reshape(1, d_inner),
            _ssd_expand_matrix(n_heads, d_inner).astype(BF16))


def _ssd_sample_pre_kernel(x_ref, buf_ref, dt_ref, cw_ref, cb_ref, dtb_ref, alog_ref, ex_ref,
                           xcs_ref, dtx_ref, dax_ref):
    xc = cb_ref[...]
    for j in range(CONV_K - 1):
        xc = xc + buf_ref[j] * cw_ref[j:j + 1, :]
    xc = xc + x_ref[...] * cw_ref[CONV_K - 1:CONV_K, :]
    xcs_ref[...] = _silu(xc)
    dt = jax.nn.softplus(dt_ref[...] + dtb_ref[...])
    da = jnp.exp(dt * (-jnp.exp(alog_ref[...])))
    ex = ex_ref[...]
    dtx_ref[...] = jnp.dot(dt, ex, precision=HIGHEST, preferred_element_type=F32)
    dax_ref[...] = jnp.dot(da, ex, precision=HIGHEST, preferred_element_type=F32)


def _ssd_sample_state_kernel(s0_ref, x_ref, b_ref, c_ref, dtx_ref, dax_ref, z_ref, dsk_ref, ng_ref,
                             sn_ref, yn_ref, yrow_ref):
    hpg = s0_ref.shape[1]
    P = SSD_HEAD_DIM
    eye = lax.broadcasted_iota(jnp.int32, (P, P), 0) == lax.broadcasted_iota(jnp.int32, (P, P), 1)
    x = x_ref[0]
    xdt = x * dtx_ref[0]
    b_row = b_ref[0]
    c_rows = jnp.broadcast_to(c_ref[0], (SUBLANES, SSD_STATE)).astype(BF16)
    for e in range(hpg):
        sl = slice(e * P, (e + 1) * P)
        xcol = jnp.sum(jnp.where(eye, jnp.broadcast_to(xdt[:, sl], (P, P)), 0.0), axis=1, keepdims=True)
        da = dax_ref[0][:, e * P:e * P + 1]
        s_new = s0_ref[0, e] * da + xcol * b_row
        sn_ref[0, e] = s_new
        yr = lax.dot_general(c_rows, s_new.astype(BF16), NT_DIMS, preferred_element_type=F32)
        yrow_ref[:, sl] = yr[0:1, :]
    y = yrow_ref[...] + x * dsk_ref[...]
    zz = z_ref[0]
    u = y * _silu(zz)
    yn_ref[0] = _rms_rows(u, ng_ref[...]).astype(yn_ref.dtype)


def _ssd_sample(xbc, buf_t, z, dt, s0, cw, cb, dt_bias, a_log, d_skip, norm_g, s_off=0):
    Bs, XW = xbc.shape
    d_inner = z.shape[1]
    n_heads = d_inner // SSD_HEAD_DIM
    gw = d_inner // SSD_GROUPS
    hpg = gw // SSD_HEAD_DIM
    ex = _ssd_expand_matrix(n_heads, d_inner)
    xcs, dtx, dax = pl.pallas_call(
        _ssd_sample_pre_kernel,
        out_shape=[jax.ShapeDtypeStruct((Bs, XW), F32), jax.ShapeDtypeStruct((Bs, d_inner), F32),
                   jax.ShapeDtypeStruct((Bs, d_inner), F32)],
        compiler_params=pltpu.CompilerParams(vmem_limit_bytes=VMEM_LIMIT), name="ssd_sample_pre")(
            xbc, buf_t, dt, cw, cb.reshape(1, XW), _pad_lanes(dt_bias), _pad_lanes(a_log), ex)
    r3 = lambda a: a.reshape(Bs, 1, a.shape[1])
    nbg = d_inner // SSD_STATE
    rowblk = lambda w, off=0: pl.BlockSpec((1, 1, w), lambda b, g, off=off: (b, 0, g + off))
    s_new, yn = pl.pallas_call(
        _ssd_sample_state_kernel,
        grid=(Bs, SSD_GROUPS),
        in_specs=[pl.BlockSpec((1, hpg, SSD_HEAD_DIM, SSD_STATE), lambda b, g: (b + s_off, g, 0, 0)),
                  rowblk(gw), rowblk(SSD_STATE, nbg), rowblk(SSD_STATE, nbg + SSD_GROUPS),
                  rowblk(gw), rowblk(gw), rowblk(gw),
                  pl.BlockSpec((1, gw), lambda b, g: (0, g)), pl.BlockSpec((1, gw), lambda b, g: (0, g))],
        out_specs=[pl.BlockSpec((1, hpg, SSD_HEAD_DIM, SSD_STATE), lambda b, g: (b, g, 0, 0)),
                   rowblk(gw)],
        out_shape=[jax.ShapeDtypeStruct((Bs,) + s0.shape[1:], F32), jax.ShapeDtypeStruct((Bs, 1, d_inner), BF16)],
        scratch_shapes=[pltpu.VMEM((1, gw), F32)],
        compiler_params=_cparams(("parallel", "parallel")), name="ssd_sample_state")(
            s0, r3(xcs), r3(xcs), r3(xcs), r3(dtx), r3(dax), r3(z),
            jnp.repeat(d_skip, SSD_HEAD_DIM).reshape(1, d_inner), norm_g.reshape(1, d_inner))
    return yn.reshape(Bs, d_inner), s_new


def _paged_topk_kernel(pt_ref, *refs, nblk, bps, ppb):
    pages = refs[:bps * ppb]
    q_ref, idx_ref, kmean_ref = refs[bps * ppb:]
    n = pl.program_id(1)
    for j in range(bps):
        tot = jnp.sum(pages[j * ppb][0, 0], axis=0)
        for r in range(1, ppb):
            tot = tot + jnp.sum(pages[j * ppb + r][0, 0], axis=0)
        kmean_ref[n * bps + j] = tot / float(MOBA_BLOCK)

    @pl.when(n == pl.num_programs(1) - 1)
    def _():
        gate = jnp.sum(kmean_ref[...] * q_ref[...], axis=-1, keepdims=True)
        blk_id = lax.broadcasted_iota(jnp.int32, gate.shape, 0)
        lane = lax.broadcasted_iota(jnp.int32, (A_HEADS, LANES), 1)
        out = jnp.zeros((A_HEADS, LANES), jnp.int32)
        for t in range(MOBA_TOPK):
            m = jnp.max(gate, axis=0, keepdims=True)
            idx = jnp.min(jnp.where(gate == m, blk_id, nblk), axis=0, keepdims=True)
            out = jnp.where(lane == t, idx[0], out)
            gate = jnp.where(blk_id == idx, -jnp.inf, gate)
        idx_ref[0] = out


def _paged_attn_kernel(pt_ref, top_ref, q_ref, kn_ref, vn_ref, k_hbm, v_hbm, o_ref, kbuf, vbuf, sem, *, layer, ppb):
    s = pl.program_id(0)
    n_pages = MOBA_TOPK * ppb

    def page_copies(step, slot):
        b, h = step // A_HEADS, step % A_HEADS
        cps = []
        for sel in range(MOBA_TOPK):
            blk = top_ref[step * MOBA_TOPK + sel]
            for r in range(ppb):
                page = pt_ref[b, blk * ppb + r]
                i = sel * ppb + r
                cps.append(pltpu.make_async_copy(k_hbm.at[layer, page, :, h, :], kbuf.at[slot, i], sem.at[0, slot]))
                cps.append(pltpu.make_async_copy(v_hbm.at[layer, page, :, h, :], vbuf.at[slot, i], sem.at[1, slot]))
        return cps

    @pl.when(s == 0)
    def _():
        for cp in page_copies(0, 0):
            cp.start()

    @pl.when(s + 1 < pl.num_programs(0))
    def _():
        for cp in page_copies(s + 1, (s + 1) % 2):
            cp.start()

    slot = s % 2
    for cp in page_copies(s, slot):
        cp.wait()

    h = s % A_HEADS
    q = q_ref[0, pl.ds(h, 1), :]
    qh = jnp.broadcast_to(q, (SUBLANES, HEAD_DIM)).astype(BF16)
    s_own = jnp.sum(q * kn_ref[0, pl.ds(h, 1), :], axis=-1, keepdims=True) * ATTN_SCALE
    ss = [lax.dot_general(qh, kbuf[slot, i].astype(BF16), NT_DIMS, preferred_element_type=F32) * ATTN_SCALE
          for i in range(n_pages)]
    m = s_own
    for sc in ss:
        m = jnp.maximum(m, jnp.max(sc[0:1], axis=-1, keepdims=True))
    p_own = jnp.exp(s_own - m)
    l = p_own
    acc = p_own * vn_ref[0, pl.ds(h, 1), :]
    for i, sc in enumerate(ss):
        p = jnp.exp(sc - m)
        l = l + jnp.sum(p[0:1], axis=-1, keepdims=True)
        acc = acc + jnp.dot(p.astype(BF16), vbuf[slot, i].astype(BF16), preferred_element_type=F32)[0:1]
    o_ref[0, pl.ds(h, 1), :] = acc / l


def _paged_moba(q, k_new, v_new, k_pool, v_pool, page_table, layer):
    Bs = q.shape[0]
    n_pages = page_table.shape[1]
    ppb = MOBA_BLOCK // PAGE_SIZE
    assert n_pages % ppb == 0
    nblk = n_pages // ppb
    assert nblk >= MOBA_TOPK
    bps = math.gcd(nblk, 4)
    q3, k3, v3 = (a.reshape(Bs, A_HEADS, HEAD_DIM) for a in (q, k_new, v_new))
    page_blk = (1, 1, PAGE_SIZE, A_HEADS, HEAD_DIM)

    def mean_spec(p):
        return pl.BlockSpec(page_blk, lambda b, n, pt, p=p: (layer, pt[b, n * bps * ppb + p], 0, 0, 0))

    idx = pl.pallas_call(
        functools.partial(_paged_topk_kernel, nblk=nblk, bps=bps, ppb=ppb),
        grid_spec=pltpu.PrefetchScalarGridSpec(
            num_scalar_prefetch=1, grid=(Bs, nblk // bps),
            in_specs=[mean_spec(p) for p in range(bps * ppb)]
            + [pl.BlockSpec((1, A_HEADS, HEAD_DIM), lambda b, n, pt: (b, 0, 0))],
            out_specs=pl.BlockSpec((1, A_HEADS, LANES), lambda b, n, pt: (b, 0, 0)),
            scratch_shapes=[pltpu.VMEM((nblk, A_HEADS, HEAD_DIM), F32)]),
        out_shape=jax.ShapeDtypeStruct((Bs, A_HEADS, LANES), jnp.int32),
        compiler_params=_cparams(("parallel", "arbitrary")), name="paged_topk")(
            page_table, *([k_pool] * (bps * ppb)), q3)
    top = idx[:, :, :MOBA_TOPK].reshape(-1)

    n_sel_pages = MOBA_TOPK * ppb
    seq_spec = pl.BlockSpec((1, A_HEADS, HEAD_DIM), lambda s, pt, tp: (s // A_HEADS, 0, 0))
    hbm_spec = pl.BlockSpec(memory_space=pl.ANY)
    att = pl.pallas_call(
        functools.partial(_paged_attn_kernel, layer=layer, ppb=ppb),
        grid_spec=pltpu.PrefetchScalarGridSpec(
            num_scalar_prefetch=2, grid=(Bs * A_HEADS,),
            in_specs=[seq_spec, seq_spec, seq_spec, hbm_spec, hbm_spec],
            out_specs=seq_spec,
            scratch_shapes=[pltpu.VMEM((2, n_sel_pages, PAGE_SIZE, HEAD_DIM), F32),
                            pltpu.VMEM((2, n_sel_pages, PAGE_SIZE, HEAD_DIM), F32),
                            pltpu.SemaphoreType.DMA((2, 2))]),
        out_shape=jax.ShapeDtypeStruct((Bs, A_HEADS, HEAD_DIM), F32),
        compiler_params=_cparams(("arbitrary",)), name="paged_attn")(
            page_table, top, q3, k3, v3, k_pool, v_pool)
    return att.reshape(Bs, A_W).astype(BF16)


def _rope_tables(pos):
    inv = ROPE_THETA ** (-jnp.arange(ROT_HALF, dtype=F32) / ROT_HALF)
    ang = pos.astype(F32)[:, None] * inv
    cos, sin = jnp.cos(ang), jnp.sin(ang)
    T = pos.shape[0]
    rest = HEAD_DIM - ROT_DIM
    cos_t = jnp.concatenate([cos, cos, jnp.ones((T, rest), F32)], axis=-1)
    sin_t = jnp.concatenate([-sin, sin, jnp.zeros((T, rest), F32)], axis=-1)
    return cos_t, sin_t


ROW_TILE = 1024


def _row_tile(m):
    return ROW_TILE if m % ROW_TILE == 0 else m


def kernel(x_prompt, x_sample, cache_k, cache_v, state_lru_conv, state_lru_h, state_ssd_conv, state_ssd_h, page_table, norm_mix, norm_mlp, w_in_a, q_norm, k_norm, lru_conv_w, lru_conv_b, lru_wa, lru_ba, lru_wx, lru_bx, lru_lambda, w_out_a, w_in_s, ssd_conv_w, ssd_conv_b, ssd_dt_bias, ssd_a_log, ssd_d, ssd_norm, w_out_s, w_up, w_down):
    Bp, Tp, D = x_prompt.shape
    Bs, Ts, _ = x_sample.shape
    assert Ts == 1
    depth = norm_mix.shape[0]
    past_len = page_table.shape[1] * PAGE_SIZE
    Mp, Ms = Bp * Tp, Bs * Ts
    tmp, tms = _row_tile(Mp), _row_tile(Ms)
    tmo = tmp // 2 if tmp == ROW_TILE else tmp
    tmr = tmp // 4 if tmp == ROW_TILE else tmp
    assert Tp % tmr == 0
    assert Tp % tmp == 0
    xp = x_prompt.reshape(Mp, D)
    xs = x_sample.reshape(Ms, D)
    lru_w = lru_lambda.shape[1]
    d_inner = ssd_norm.shape[1]
    xbc_w = ssd_conv_w.shape[2]
    n_ssd_heads = ssd_d.shape[1]

    cos_p, sin_p = _rope_tables(jnp.arange(Tp, dtype=jnp.int32))
    cos_s, sin_s = _rope_tables(jnp.full((Ms,), past_len, dtype=jnp.int32))

    outs = {k: [] for k in ("kp", "vp", "ks", "vs", "lcp", "lcs", "lhp", "lhs", "scp", "scs", "shp", "shs")}
    for layer in range(depth):
        i = layer // 2
        if layer % 2 == 0:
            w_in = w_in_a[i].astype(BF16)
            w_out = w_out_a[i].astype(BF16)
            splits = [A_W, A_W, A_W, lru_w, lru_w]
            lru_args = (lru_conv_w[i], lru_conv_b[i], lru_wa[i], lru_ba[i], lru_wx[i], lru_bx[i], lru_lambda[i])
            q, k, v, xr, yg = _norm_proj_resident(xp, norm_mix[layer], w_in, splits, tm=tmr,
                                                  rope=((q_norm[i], k_norm[i]), cos_p, sin_p, Tp // tmr))
            att = _moba_prompt(q, k, v, batch=Bp, seq=Tp)
            rec, h_last = _lru_prompt(xr, yg, *lru_args, batch=Bp, seq=Tp)
            mixed_p = _out_proj([att, rec], w_out, xp, tm=tmo)
            outs["kp"].append(k.reshape(Bp, Tp, A_HEADS, HEAD_DIM))
            outs["vp"].append(v.reshape(Bp, Tp, A_HEADS, HEAD_DIM))
            outs["lcp"].append(xr.reshape(Bp, Tp, lru_w)[:, Tp - (CONV_K - 1):])
            outs["lhp"].append(h_last.reshape(Bp, lru_w))
            q, k, v, xr, yg = _norm_proj_resident(xs, norm_mix[layer], w_in, splits, tm=tms,
                                                  rope=((q_norm[i], k_norm[i]), cos_s, sin_s, 1))
            att = _paged_moba(q, k, v, cache_k, cache_v, page_table, layer=i)
            buf = state_lru_conv[i]
            rec, h_new = _lru_sample(xr, yg, buf.swapaxes(0, 1), state_lru_h[i], *lru_args)
            mixed_s = _out_proj([att, rec], w_out, xs, tm=tms)
            outs["ks"].append(k.reshape(Bs, Ts, A_HEADS, HEAD_DIM))
            outs["vs"].append(v.reshape(Bs, Ts, A_HEADS, HEAD_DIM))
            outs["lcs"].append(jnp.concatenate([buf[:, 1:], xr[:, None, :]], axis=1))
            outs["lhs"].append(h_new)
        else:
            main_w = d_inner + xbc_w
            w_in = w_in_s[i].astype(BF16)
            w_dt = jnp.pad(w_in_s[i][:, main_w:], ((0, 0), (0, LANES - n_ssd_heads))).astype(BF16)
            w_out = w_out_s[i].astype(BF16)
            ssd_args = (ssd_conv_w[i], ssd_conv_b[i], ssd_dt_bias[i], ssd_a_log[i], ssd_d[i], ssd_norm[i])
            unslab = lambda a: a.swapaxes(0, 1).reshape(a.shape[1], -1)
            z, xbc, dt = _norm_proj(xp, norm_mix[layer], w_in, [d_inner, xbc_w], tm=tmp, tn=1024, extra_w=w_dt, slab=True)
            yn, s_last = _ssd_prompt(xbc, z, dt, *ssd_args, batch=Bp, seq=Tp)
            mixed_p = _out_proj([yn], w_out, xp, tm=tmo)
            tail = xbc.reshape(-1, Bp, Tp, LANES)[:, :, Tp - (CONV_K - 1):]
            outs["scp"].append(tail.transpose(1, 2, 0, 3).reshape(Bp, CONV_K - 1, xbc_w))
            outs["shp"].append(s_last.reshape(Bp, n_ssd_heads, SSD_HEAD_DIM, SSD_STATE))
            z, xbc, dt = _norm_proj(xs, norm_mix[layer], w_in, [d_inner, xbc_w], tm=tms, tn=1024, extra_w=w_dt, slab=True)
            z, xbc = unslab(z), unslab(xbc)
            buf = state_ssd_conv[i]
            yn, s_new = _ssd_sample(xbc, buf.swapaxes(0, 1), z, dt,
                                    state_ssd_h.reshape((-1,) + state_ssd_h.shape[2:]), *ssd_args, s_off=i * Bs)
            mixed_s = _out_proj([yn], w_out, xs, tm=tms)
            outs["scs"].append(jnp.concatenate([buf[:, 1:], xbc[:, None, :]], axis=1))
            outs["shs"].append(s_new)
        wu, wd = w_up[layer].astype(BF16), w_down[layer].astype(BF16)
        if layer % 2 == 0:
            xp = _mlp(mixed_p, norm_mlp[layer], wu, wd, tm=tmp)
        else:
            xp = _mlp(mixed_p, norm_mlp[layer], wu, wd, tm=tmp // 2, tf=2048)
        xs = _mlp(mixed_s, norm_mlp[layer], wu, wd, tm=tms)

    st = lambda key: jnp.stack(outs[key])
    return (xp.reshape(Bp, Tp, D), xs.reshape(Bs, Ts, D), st("kp"), st("vp"), st("ks"), st("vs"),
            st("lcp"), st("lcs"), st("lhp"), st("lhs"), st("scp"), st("scs"), st("shp"), st("shs"))
```

```python
import functools
import math

import jax
import jax.numpy as jnp
from jax import lax
from jax.experimental import pallas as pl
from jax.experimental.pallas import tpu as pltpu

F32 = jnp.float32
BF16 = jnp.bfloat16
HIGHEST = lax.Precision.HIGHEST

A_HEADS = 8
HEAD_DIM = 128
A_W = A_HEADS * HEAD_DIM
ROT_DIM = HEAD_DIM // 4
ROT_HALF = ROT_DIM // 2
ROPE_THETA = 500000.0
MOBA_BLOCK = 256
MOBA_TOPK = 3
ATTN_SCALE = HEAD_DIM ** -0.5
LRU_HEADS = 8
LRU_C = 8.0
CONV_K = 4
SSD_HEAD_DIM = 64
SSD_STATE = 128
SSD_GROUPS = 8
SSD_CHUNK = 128
PAGE_SIZE = 128
EPS = 1e-6

LANES = 128
SUBLANES = 8
NEG_BIG = -1e30
LOG2E = math.log2(math.e)
VMEM_BYTES_V7X = 64 * 1024 * 1024
VMEM_LIMIT = VMEM_BYTES_V7X - 8 * 1024 * 1024
VMEM_LIMIT_MLP = VMEM_BYTES_V7X - 4 * 1024 * 1024

NT_DIMS = (((1,), (1,)), ((), ()))


def _cparams(sem, vmem=VMEM_LIMIT):
    return pltpu.CompilerParams(dimension_semantics=sem, vmem_limit_bytes=vmem)


def _sigmoid(x):
    return 0.5 * (jnp.tanh(0.5 * x) + 1.0)


def _silu(x):
    h = 0.5 * x
    return h * (jnp.tanh(h) + 1.0)


def _rms_rows(x, g):
    return x * lax.rsqrt(jnp.mean(x * x, axis=-1, keepdims=True) + EPS) * g


def _head_norm_rope(acc, g, cos, sin):
    y = _rms_rows(acc, g)
    lane = lax.broadcasted_iota(jnp.int32, y.shape, 1)
    partner = jnp.where(lane < ROT_HALF, pltpu.roll(y, LANES - ROT_HALF, 1), pltpu.roll(y, ROT_HALF, 1))
    return y * cos + partner * sin


RESIDENT_DOT_COLS = 1024


def _norm_proj_resident_kernel(*refs, splits, n_rope, has_extra, slab):
    it = iter(refs)
    x_ref, g_ref, w_ref = next(it), next(it), next(it)
    rope_g = [next(it) for _ in range(n_rope)]
    if n_rope:
        cos_ref, sin_ref = next(it), next(it)
    if has_extra:
        we_ref = next(it)
    outs = [next(it) for _ in splits]
    xn = _rms_rows(x_ref[...], g_ref[...]).astype(BF16)
    if has_extra:
        next(it)[...] = jnp.dot(xn, we_ref[...], preferred_element_type=F32)
    col = 0
    for s, width in enumerate(splits):
        for c0 in range(0, width, RESIDENT_DOT_COLS):
            acc = jnp.dot(xn, w_ref[:, col + c0:col + c0 + RESIDENT_DOT_COLS], preferred_element_type=F32)
            for hh in range(RESIDENT_DOT_COLS // LANES):
                sl = slice(hh * LANES, (hh + 1) * LANES)
                val = acc[:, sl]
                if s < n_rope:
                    val = _head_norm_rope(val, rope_g[s][...], cos_ref[...], sin_ref[...])
                if slab[s]:
                    outs[s][c0 // LANES + hh] = val
                else:
                    outs[s][:, c0 + hh * LANES:c0 + (hh + 1) * LANES] = val
        col += width


def _norm_proj_resident(x, g, w, splits, *, tm, rope=None, extra_w=None, slab=False):
    M, K = x.shape
    N = sum(splits)
    assert w.shape == (K, N) and M % tm == 0 and all(s % RESIDENT_DOT_COLS == 0 for s in splits)
    assert HEAD_DIM == LANES
    n_rope = 0 if rope is None else len(rope[0])
    in_specs = [pl.BlockSpec((tm, K), lambda i: (i, 0)),
                pl.BlockSpec((1, K), lambda i: (0, 0)),
                pl.BlockSpec((K, N), lambda i: (0, 0), pipeline_mode=pl.Buffered(1))]
    args = [x, g.reshape(1, K), w]
    if rope is not None:
        gains, cos, sin, period = rope
        for gg in gains:
            in_specs.append(pl.BlockSpec((1, HEAD_DIM), lambda i: (0, 0)))
            args.append(gg.reshape(1, HEAD_DIM))
        for tab in (cos, sin):
            in_specs.append(pl.BlockSpec((tm, HEAD_DIM), lambda i, period=period: (i % period, 0)))
            args.append(tab)
    if extra_w is not None:
        in_specs.append(pl.BlockSpec((K, LANES), lambda i: (0, 0)))
        args.append(extra_w)
    slab = tuple(slab) if isinstance(slab, (tuple, list)) else (bool(slab),) * len(splits)
    out_specs, out_shape = [], []
    for s, as_slab in zip(splits, slab):
        if as_slab:
            out_specs.append(pl.BlockSpec((s // LANES, tm, LANES), lambda i: (0, i, 0)))
            out_shape.append(jax.ShapeDtypeStruct((s // LANES, M, LANES), F32))
        else:
            out_specs.append(pl.BlockSpec((tm, s), lambda i: (i, 0)))
            out_shape.append(jax.ShapeDtypeStruct((M, s), F32))
    if extra_w is not None:
        out_specs.append(pl.BlockSpec((tm, LANES), lambda i: (i, 0)))
        out_shape.append(jax.ShapeDtypeStruct((M, LANES), F32))
    return pl.pallas_call(
        functools.partial(_norm_proj_resident_kernel, splits=tuple(splits), n_rope=n_rope,
                          has_extra=extra_w is not None, slab=slab),
        grid=(M // tm,), in_specs=in_specs, out_specs=out_specs, out_shape=out_shape,
        compiler_params=_cparams(("parallel",)), name="norm_proj_resident")(*args)


def _out_proj_kernel(*refs, n_parts, kp):
    xs = refs[:n_parts]
    w_ref, res_ref, o_ref = refs[n_parts:]
    acc = res_ref[...]
    for p, x_ref in enumerate(xs):
        if len(x_ref.shape) == 3:
            x = jnp.concatenate([x_ref[j] for j in range(x_ref.shape[0])], axis=1)
        else:
            x = x_ref[...]
        acc = acc + jnp.dot(x, w_ref[p * kp:(p + 1) * kp, :], preferred_element_type=F32)
    o_ref[...] = acc


def _out_proj(parts, w, res, *, tm):
    M, N = res.shape
    kp = w.shape[0] // len(parts)
    assert w.shape == (kp * len(parts), N) and M % tm == 0
    in_specs = []
    for p in parts:
        if p.ndim == 3:
            assert p.shape == (kp // LANES, M, LANES)
            in_specs.append(pl.BlockSpec((kp // LANES, tm, LANES), lambda i: (0, i, 0)))
        else:
            assert p.shape == (M, kp)
            in_specs.append(pl.BlockSpec((tm, kp), lambda i: (i, 0)))
    in_specs.append(pl.BlockSpec(w.shape, lambda i: (0, 0), pipeline_mode=pl.Buffered(1)))
    in_specs.append(pl.BlockSpec((tm, N), lambda i: (i, 0)))
    return pl.pallas_call(
        functools.partial(_out_proj_kernel, n_parts=len(parts), kp=kp),
        grid=(M // tm,), in_specs=in_specs,
        out_specs=pl.BlockSpec((tm, N), lambda i: (i, 0)),
        out_shape=jax.ShapeDtypeStruct((M, N), F32),
        compiler_params=_cparams(("parallel",)), name="out_proj")(*parts, w, res)


def _mlp_kernel(x_ref, g_ref, wu_ref, wd_ref, o_ref, xn_ref):
    f = pl.program_id(1)

    @pl.when(f == 0)
    def _():
        x = x_ref[...]
        xn_ref[...] = _rms_rows(x, g_ref[...]).astype(BF16)
        o_ref[...] = x

    h = jnp.maximum(jnp.dot(xn_ref[...], wu_ref[...], preferred_element_type=F32), 0.0)
    o_ref[...] += jnp.dot((h * h).astype(BF16), wd_ref[...], preferred_element_type=F32)


def _mlp(x, g, w_up, w_down, *, tm, tf=1024):
    M, D = x.shape
    FF = w_up.shape[1]
    return pl.pallas_call(
        _mlp_kernel, grid=(M // tm, FF // tf),
        in_specs=[pl.BlockSpec((tm, D), lambda i, f: (i, 0)),
                  pl.BlockSpec((1, D), lambda i, f: (0, 0)),
                  pl.BlockSpec((D, tf), lambda i, f: (0, f)),
                  pl.BlockSpec((tf, D), lambda i, f: (f, 0))],
        out_specs=pl.BlockSpec((tm, D), lambda i, f: (i, 0)),
        out_shape=jax.ShapeDtypeStruct((M, D), F32),
        scratch_shapes=[pltpu.VMEM((tm, D), BF16)],
        compiler_params=_cparams(("parallel", "arbitrary"), VMEM_LIMIT_MLP), name="mlp")(
            x, g.reshape(1, D), w_up, w_down)


def _moba_prompt_kernel(q_ref, k_ref, v_ref, hot_ref, o_ref, kaug_ref, vb_ref, kmean_ref, *, nb):
    blk, D = MOBA_BLOCK, HEAD_DIM
    kaug_ref[:, 0:D] = k_ref[...].astype(BF16)
    kaug_ref[:, D:2 * D] = hot_ref[...]
    vb_ref[...] = v_ref[...].astype(BF16)
    kmean_ref[...] = jnp.zeros(kmean_ref.shape, F32)
    for n in range(nb):
        kmean_ref[n:n + 1, :] = jnp.mean(k_ref[n * blk:(n + 1) * blk, :], axis=0, keepdims=True)

    lane = lax.broadcasted_iota(jnp.int32, (blk, LANES), 1)
    row = lax.broadcasted_iota(jnp.int32, (blk, blk), 0)
    col = lax.broadcasted_iota(jnp.int32, (blk, blk), 1)
    nbp = kmean_ref.shape[0]
    blk_id = lax.broadcasted_iota(jnp.int32, (nbp, blk), 0)
    eye = (row == col).astype(BF16)
    for qb in range(nb):
        q = q_ref[qb * blk:(qb + 1) * blk, :]
        if qb > MOBA_TOPK:
            gate = lax.dot_general(kmean_ref[...], q, NT_DIMS, precision=HIGHEST, preferred_element_type=F32)
            gate = jnp.where(blk_id < qb, gate, -jnp.inf)
            bias_t = jnp.where(blk_id == qb, 0.0, NEG_BIG)
            for _ in range(MOBA_TOPK):
                m = jnp.max(gate, axis=0, keepdims=True)
                idx = jnp.min(jnp.where(gate == m, blk_id, nbp), axis=0, keepdims=True)
                hit = blk_id == idx
                bias_t = jnp.where(hit & (m > -jnp.inf), 0.0, bias_t)
                gate = jnp.where(hit, -jnp.inf, gate)
            pad = jnp.zeros((LANES - nbp, blk), F32)
            bias = lax.dot_general(eye, jnp.concatenate([bias_t, pad], axis=0).astype(BF16), NT_DIMS,
                                   preferred_element_type=F32).astype(BF16)
        else:
            bias = jnp.where(lane <= qb, 0.0, NEG_BIG).astype(BF16)
        lhs = jnp.concatenate([(q * (ATTN_SCALE * LOG2E)).astype(BF16), bias], axis=1)
        own = slice(qb * blk, (qb + 1) * blk)
        s_own = lax.dot_general(lhs, kaug_ref[own, :], NT_DIMS, preferred_element_type=F32)
        s_own = jnp.where(col <= row, s_own, NEG_BIG)
        m = jnp.max(s_own, axis=-1, keepdims=True)
        if qb > 0:
            s_past = lax.dot_general(lhs, kaug_ref[0:qb * blk, :], NT_DIMS, preferred_element_type=F32)
            m = jnp.maximum(m, jnp.max(s_past, axis=-1, keepdims=True))
        p = jnp.exp2(s_own - m)
        l = jnp.sum(p, axis=-1, keepdims=True)
        acc = jnp.dot(p.astype(BF16), vb_ref[own, :], preferred_element_type=F32)
        if qb > 0:
            p = jnp.exp2(s_past - m)
            l = l + jnp.sum(p, axis=-1, keepdims=True)
            acc = acc + jnp.dot(p.astype(BF16), vb_ref[0:qb * blk, :], preferred_element_type=F32)
        o_ref[own, :] = (acc / l).astype(o_ref.dtype)


def _moba_prompt(q, k, v, *, batch, seq):
    M = q.shape[1]
    nb = seq // MOBA_BLOCK
    assert nb <= LANES and seq % MOBA_BLOCK == 0 and q.shape == (A_HEADS, M, HEAD_DIM)
    hot = (lax.broadcasted_iota(jnp.int32, (seq, LANES), 0) // MOBA_BLOCK
           == lax.broadcasted_iota(jnp.int32, (seq, LANES), 1)).astype(BF16)
    seq_head = pl.BlockSpec((None, seq, HEAD_DIM), lambda b, h: (h, b, 0))
    return pl.pallas_call(
        functools.partial(_moba_prompt_kernel, nb=nb),
        grid=(batch, A_HEADS),
        in_specs=[seq_head, seq_head, seq_head, pl.BlockSpec((seq, LANES), lambda b, h: (0, 0))],
        out_specs=seq_head,
        out_shape=jax.ShapeDtypeStruct((A_HEADS, M, HEAD_DIM), BF16),
        scratch_shapes=[pltpu.VMEM((seq, 2 * HEAD_DIM), BF16), pltpu.VMEM((seq, HEAD_DIM), BF16),
                        pltpu.VMEM((-(-nb // SUBLANES) * SUBLANES, HEAD_DIM), F32)],
        compiler_params=_cparams(("parallel", "parallel")), name="moba_prompt")(q, k, v, hot)


def _lru_gates(xc, wa_ref, ba, wx_ref, bx, lam):
    bw = xc.shape[1] // LRU_HEADS
    rs, is_ = [], []
    for n in range(LRU_HEADS):
        xb = xc[:, n * bw:(n + 1) * bw].astype(BF16)
        rs.append(jnp.dot(xb, wa_ref[n], preferred_element_type=F32))
        is_.append(jnp.dot(xb, wx_ref[n], preferred_element_type=F32))
    r = _sigmoid(jnp.concatenate(rs, axis=-1) + ba)
    i = _sigmoid(jnp.concatenate(is_, axis=-1) + bx)
    log_a = -LRU_C * r * jax.nn.softplus(-lam)
    a = jnp.exp(log_a)
    t = jnp.tanh(log_a)
    u = jnp.sqrt(-2.0 * t / (1.0 - t)) * (i * xc)
    return a, u


def _lru_prompt_kernel(xr_ref, yg_ref, cw_ref, cb_ref, wa_ref, ba_ref, wx_ref, bx_ref, lam_ref,
                       rec_ref, hl_ref, xx_ref, a_ref, u_ref, h_ref, *, tt):
    ti = pl.program_id(1)
    pad = SUBLANES

    @pl.when(ti == 0)
    def _():
        xx_ref[0:pad, :] = jnp.zeros((pad, xx_ref.shape[1]), F32)
        h_ref[...] = jnp.zeros(h_ref.shape, F32)

    @pl.when(ti > 0)
    def _():
        xx_ref[0:pad, :] = xx_ref[tt:tt + pad, :]

    xx_ref[pad:pad + tt, :] = xr_ref[...]
    xc = cb_ref[...]
    for j in range(CONV_K):
        off = pad - (CONV_K - 1) + j
        xc = xc + xx_ref[off:off + tt, :] * cw_ref[j:j + 1, :]
    a, u = _lru_gates(xc, wa_ref, ba_ref[...], wx_ref, bx_ref[...], lam_ref[...])
    a_ref[...] = a
    u_ref[...] = u

    sub = lax.broadcasted_iota(jnp.int32, (SUBLANES, a.shape[1]), 0)

    def group(gi, h):
        r0 = pl.multiple_of(gi * SUBLANES, SUBLANES)
        a8 = a_ref[pl.ds(r0, SUBLANES), :]
        u8 = u_ref[pl.ds(r0, SUBLANES), :]
        for s in (1, 2, 4):
            keep = sub >= s
            a_sh = pltpu.roll(a8, s, 0)
            u_sh = pltpu.roll(u8, s, 0)
            u8 = jnp.where(keep, a8 * u_sh + u8, u8)
            a8 = jnp.where(keep, a8 * a_sh, a8)
        hs = a8 * h + u8
        yg = yg_ref[pl.ds(r0, SUBLANES), :]
        rec_ref[pl.ds(r0, SUBLANES), :] = (hs * jax.nn.gelu(yg)).astype(rec_ref.dtype)
        return hs[SUBLANES - 1:SUBLANES, :]

    h = lax.fori_loop(0, tt // SUBLANES, group, h_ref[...])
    h_ref[...] = h

    @pl.when(ti == pl.num_programs(1) - 1)
    def _():
        hl_ref[0] = h


def _lru_prompt(xr, yg, cw, cb, wa, ba, wx, bx, lam, *, batch, seq, tt=256):
    M, W = xr.shape
    nt = seq // tt
    row = lambda a: a.reshape(1, W)
    full2 = lambda shp: pl.BlockSpec(shp, lambda b, t: (0, 0))
    full3 = lambda shp: pl.BlockSpec(shp, lambda b, t: (0, 0, 0))
    return pl.pallas_call(
        functools.partial(_lru_prompt_kernel, tt=tt),
        grid=(batch, nt),
        in_specs=[pl.BlockSpec((tt, W), lambda b, t: (b * nt + t, 0)),
                  pl.BlockSpec((tt, W), lambda b, t: (b * nt + t, 0)),
                  full2((CONV_K, W)), full2((1, W)), full3(wa.shape), full2((1, W)),
                  full3(wx.shape), full2((1, W)), full2((1, W))],
        out_specs=[pl.BlockSpec((tt, W), lambda b, t: (b * nt + t, 0)),
                   pl.BlockSpec((1, 1, W), lambda b, t: (b, 0, 0))],
        out_shape=[jax.ShapeDtypeStruct((M, W), BF16), jax.ShapeDtypeStruct((batch, 1, W), F32)],
        scratch_shapes=[pltpu.VMEM((tt + SUBLANES, W), F32), pltpu.VMEM((tt, W), F32),
                        pltpu.VMEM((tt, W), F32), pltpu.VMEM((1, W), F32)],
        compiler_params=_cparams(("parallel", "arbitrary")), name="lru_prompt")(
            xr, yg, cw, row(cb), wa.astype(BF16), row(ba), wx.astype(BF16), row(bx), row(lam))


def _lru_sample_kernel(xr_ref, yg_ref, buf_ref, h0_ref, cw_ref, cb_ref, wa_ref, ba_ref, wx_ref, bx_ref,
                       lam_ref, rec_ref, hn_ref):
    xc = cb_ref[...]
    for j in range(CONV_K - 1):
        xc = xc + buf_ref[j] * cw_ref[j:j + 1, :]
    xc = xc + xr_ref[...] * cw_ref[CONV_K - 1:CONV_K, :]
    a, u = _lru_gates(xc, wa_ref, ba_ref[...], wx_ref, bx_ref[...], lam_ref[...])
    h = a * h0_ref[...] + u
    hn_ref[...] = h
    rec_ref[...] = (h * jax.nn.gelu(yg_ref[...])).astype(rec_ref.dtype)


def _lru_sample(xr, yg, buf_t, h0, cw, cb, wa, ba, wx, bx, lam):
    Bs, W = xr.shape
    row = lambda a: a.reshape(1, W)
    return pl.pallas_call(
        _lru_sample_kernel,
        out_shape=[jax.ShapeDtypeStruct((Bs, W), BF16), jax.ShapeDtypeStruct((Bs, W), F32)],
        compiler_params=pltpu.CompilerParams(vmem_limit_bytes=VMEM_LIMIT), name="lru_sample")(
            xr, yg, buf_t, h0, cw, row(cb), wa.astype(BF16), row(ba), wx.astype(BF16), row(bx), row(lam))


def _ssd_prompt_kernel(xbc_ref, prev_ref, z_ref, dt_ref, cw_ref, cb_ref, dtb_ref, alog_ref, dsk_ref, ng_ref,
                       ex_ref, yn_ref, sl_ref, xx_ref, xcv_ref, y_ref, st_ref, *, d_inner):
    c = pl.program_id(1)
    L = SSD_CHUNK
    pad = SUBLANES
    gw = d_inner // SSD_GROUPS
    spg = gw // LANES
    hps = LANES // SSD_HEAD_DIM
    head_of_lane = lax.broadcasted_iota(jnp.int32, (L, LANES), 1) // SSD_HEAD_DIM
    assert SSD_STATE == LANES

    @pl.when(c == 0)
    def _():
        xx_ref[:, 0:pad, :] = jnp.zeros((xx_ref.shape[0], pad, LANES), F32)
        st_ref[...] = jnp.zeros(st_ref.shape, F32)

    @pl.when(c > 0)
    def _():
        xx_ref[:, 0:pad, :] = prev_ref[...]

    xx_ref[:, pad:pad + L, :] = xbc_ref[...]

    def conv_slab(k, carry):
        xs = xx_ref[k]
        xc = cb_ref[k]
        for j in range(CONV_K):
            back = CONV_K - 1 - j
            tap = xs if back == 0 else pltpu.roll(xs, back, 0)
            xc = xc + tap[pad:] * cw_ref[k, j:j + 1, :]
        xcv_ref[k] = _silu(xc)
        return carry

    lax.fori_loop(0, xx_ref.shape[0], conv_slab, 0)

    def slabs(ref, first, count):
        return jnp.concatenate([ref[first + i] for i in range(count)], axis=1)

    def split2(a):
        hi = a.astype(BF16)
        return hi, (a - hi.astype(F32)).astype(BF16)

    dt = jax.nn.softplus(dt_ref[...] + dtb_ref[...])
    dta = dt * (-jnp.exp(alog_ref[...]))
    tril = lax.broadcasted_iota(jnp.int32, (L, L), 1) <= lax.broadcasted_iota(jnp.int32, (L, L), 0)
    cs = jnp.dot(tril.astype(F32), dta, precision=HIGHEST, preferred_element_type=F32)
    cs_l2 = cs * LOG2E
    adj_t = (cs_l2 - jnp.log2(dt)).T
    last = cs[L - 1:L, :]
    w_hi, w_lo = split2(dt * jnp.exp(last - cs))
    e_hi, e_lo = split2(jnp.exp(cs))
    cdec = jnp.broadcast_to(jnp.exp(last), (SUBLANES, LANES))
    cd_hi, cd_rest = split2(cdec)
    cd_mid, cd_lo = split2(cdec - cd_hi.astype(F32))

    def expand(terms, exg):
        out = jnp.dot(terms[0], exg, preferred_element_type=F32)
        for t in terms[1:]:
            out = out + jnp.dot(t, exg, preferred_element_type=F32)
        return out

    for g in range(SSD_GROUPS):
        lo = g * gw
        exg = ex_ref[:, lo:lo + gw]
        w_x = expand((w_hi, w_lo), exg)
        ecs_x = expand((e_hi, e_lo), exg)
        cd_x = expand((cd_hi, cd_mid, cd_lo), exg)[0:1]
        xg = slabs(xcv_ref, g * spg, spg)
        bg = xcv_ref[d_inner // LANES + g]
        cg = xcv_ref[d_inner // LANES + SSD_GROUPS + g].astype(BF16)
        bg_t = bg.T.astype(BF16)
        cb = jnp.dot(cg, bg_t, preferred_element_type=F32)
        st_g = st_ref[:, lo:lo + gw]
        y_off = jnp.dot(cg, st_g.astype(BF16), preferred_element_type=F32) * ecs_x
        st_ref[:, lo:lo + gw] = st_g * cd_x + jnp.dot(bg_t, (xg * w_x).astype(BF16), preferred_element_type=F32)
        for i in range(spg):
            xs_h = xcv_ref[g * spg + i].astype(BF16)
            lhs, rhs = [], []
            for k in range(hps):
                h = (g * spg + i) * hps + k
                dec = jnp.exp2(jnp.where(tril, cs_l2[:, h:h + 1] - adj_t[h:h + 1, :], NEG_BIG))
                lhs.append((cb * dec).astype(BF16))
                rhs.append(jnp.where(head_of_lane == k, xs_h, jnp.zeros_like(xs_h)))
            yd = jnp.dot(jnp.concatenate(lhs, axis=1), jnp.concatenate(rhs, axis=0), preferred_element_type=F32)
            cols = slice(lo + i * LANES, lo + (i + 1) * LANES)
            y_ref[:, cols] = yd + y_off[:, i * LANES:(i + 1) * LANES]

    for g in range(SSD_GROUPS):
        lo = g * gw
        cols = slice(lo, lo + gw)
        y = y_ref[:, cols] + slabs(xcv_ref, g * spg, spg) * dsk_ref[:, cols]
        zz = slabs(z_ref, g * spg, spg)
        u = y * _silu(zz)
        yn_ref[:, cols] = _rms_rows(u, ng_ref[:, cols]).astype(yn_ref.dtype)

    @pl.when(c == pl.num_programs(1) - 1)
    def _():
        for k in range(d_inner // LANES):
            sl_ref[0, k * LANES:(k + 1) * LANES, :] = st_ref[:, k * LANES:(k + 1) * LANES].T


def _ssd_expand_matrix(n_heads, d_inner):
    h = lax.broadcasted_iota(jnp.int32, (LANES, d_inner), 0)
    col = lax.broadcasted_iota(jnp.int32, (LANES, d_inner), 1)
    return (col // SSD_HEAD_DIM == h).astype(F32) * (h < n_heads).astype(F32)


def _pad_lanes(v):
    return jnp.pad(v, (0, LANES - v.shape[0])).reshape(1, LANES)


def _ssd_prompt(xbc, z, dt, cw, cb, dt_bias, a_log, d_skip, norm_g, *, batch, seq):
    nxs, M, _ = xbc.shape
    nzs = z.shape[0]
    XW, d_inner = nxs * LANES, nzs * LANES
    n_heads = d_inner // SSD_HEAD_DIM
    L = SSD_CHUNK
    nc = seq // L
    rpb = L // SUBLANES
    const2 = lambda shp: pl.BlockSpec(shp, lambda b, c: (0, 0))
    const3 = lambda shp: pl.BlockSpec(shp, lambda b, c: (0, 0, 0))
    return pl.pallas_call(
        functools.partial(_ssd_prompt_kernel, d_inner=d_inner),
        grid=(batch, nc),
        in_specs=[pl.BlockSpec((nxs, L, LANES), lambda b, c: (0, b * nc + c, 0)),
                  pl.BlockSpec((nxs, SUBLANES, LANES), lambda b, c: (0, jnp.maximum((b * nc + c) * rpb - 1, 0), 0)),
                  pl.BlockSpec((nzs, L, LANES), lambda b, c: (0, b * nc + c, 0)),
                  pl.BlockSpec((L, LANES), lambda b, c: (b * nc + c, 0)),
                  const3((nxs, CONV_K, LANES)), const3((nxs, 1, LANES)), const2((1, LANES)), const2((1, LANES)),
                  const2((1, d_inner)), const2((1, d_inner)), const2((LANES, d_inner))],
        out_specs=[pl.BlockSpec((L, d_inner), lambda b, c: (b * nc + c, 0)),
                   pl.BlockSpec((1, d_inner, SSD_STATE), lambda b, c: (b, 0, 0))],
        out_shape=[jax.ShapeDtypeStruct((M, d_inner), BF16),
                   jax.ShapeDtypeStruct((batch, d_inner, SSD_STATE), F32)],
        scratch_shapes=[pltpu.VMEM((nxs, L + SUBLANES, LANES), F32), pltpu.VMEM((nxs, L, LANES), F32),
                        pltpu.VMEM((L, d_inner), F32), pltpu.VMEM((SSD_STATE, d_inner), F32)],
        compiler_params=_cparams(("parallel", "arbitrary")), name="ssd_prompt")(
            xbc, xbc, z, dt, cw.reshape(CONV_K, nxs, LANES).swapaxes(0, 1), cb.reshape(nxs, 1, LANES),
            _pad_lanes(dt_bias), _pad_lanes(a_log),
            jnp.repeat(d_skip, SSD_HEAD_DIM).reshape(1, d_inner), norm_g.reshape(1, d_inner),
            _ssd_expand_matrix(n_heads, d_inner).astype(BF16))


def _ssd_sample_pre_kernel(x_ref, buf_ref, dt_ref, cw_ref, cb_ref, dtb_ref, alog_ref, ex_ref,
                           xcs_ref, dtx_ref, dax_ref):
    xc = cb_ref[...]
    for j in range(CONV_K - 1):
        xc = xc + buf_ref[j] * cw_ref[j:j + 1, :]
    xc = xc + x_ref[...] * cw_ref[CONV_K - 1:CONV_K, :]
    xcs_ref[...] = _silu(xc)
    dt = jax.nn.softplus(dt_ref[...] + dtb_ref[...])
    da = jnp.exp(dt * (-jnp.exp(alog_ref[...])))
    ex = ex_ref[...]
    dtx_ref[...] = jnp.dot(dt, ex, precision=HIGHEST, preferred_element_type=F32)
    dax_ref[...] = jnp.dot(da, ex, precision=HIGHEST, preferred_element_type=F32)


def _ssd_sample_state_kernel(s0_ref, x_ref, b_ref, c_ref, dtx_ref, dax_ref, z_ref, dsk_ref, ng_ref,
                             sn_ref, yn_ref, yrow_ref):
    hpg = s0_ref.shape[1]
    P = SSD_HEAD_DIM
    eye = lax.broadcasted_iota(jnp.int32, (P, P), 0) == lax.broadcasted_iota(jnp.int32, (P, P), 1)
    x = x_ref[0]
    xdt = x * dtx_ref[0]
    b_row = b_ref[0]
    c_rows = jnp.broadcast_to(c_ref[0], (SUBLANES, SSD_STATE)).astype(BF16)
    for e in range(hpg):
        sl = slice(e * P, (e + 1) * P)
        xcol = jnp.sum(jnp.where(eye, jnp.broadcast_to(xdt[:, sl], (P, P)), 0.0), axis=1, keepdims=True)
        da = dax_ref[0][:, e * P:e * P + 1]
        s_new = s0_ref[0, e] * da + xcol * b_row
        sn_ref[0, e] = s_new
        yr = lax.dot_general(c_rows, s_new.astype(BF16), NT_DIMS, preferred_element_type=F32)
        yrow_ref[:, sl] = yr[0:1, :]
    y = yrow_ref[...] + x * dsk_ref[...]
    zz = z_ref[0]
    u = y * _silu(zz)
    yn_ref[0] = _rms_rows(u, ng_ref[...]).astype(yn_ref.dtype)


def _ssd_sample(xbc, buf_t, z, dt, s0, cw, cb, dt_bias, a_log, d_skip, norm_g, s_off=0):
    Bs, XW = xbc.shape
    d_inner = z.shape[1]
    n_heads = d_inner // SSD_HEAD_DIM
    gw = d_inner // SSD_GROUPS
    hpg = gw // SSD_HEAD_DIM
    ex = _ssd_expand_matrix(n_heads, d_inner)
    xcs, dtx, dax = pl.pallas_call(
        _ssd_sample_pre_kernel,
        out_shape=[jax.ShapeDtypeStruct((Bs, XW), F32), jax.ShapeDtypeStruct((Bs, d_inner), F32),
                   jax.ShapeDtypeStruct((Bs, d_inner), F32)],
        compiler_params=pltpu.CompilerParams(vmem_limit_bytes=VMEM_LIMIT), name="ssd_sample_pre")(
            xbc, buf_t, dt, cw, cb.reshape(1, XW), _pad_lanes(dt_bias), _pad_lanes(a_log), ex)
    r3 = lambda a: a.reshape(Bs, 1, a.shape[1])
    nbg = d_inner // SSD_STATE
    rowblk = lambda w, off=0: pl.BlockSpec((1, 1, w), lambda b, g, off=off: (b, 0, g + off))
    s_new, yn = pl.pallas_call(
        _ssd_sample_state_kernel,
        grid=(Bs, SSD_GROUPS),
        in_specs=[pl.BlockSpec((1, hpg, SSD_HEAD_DIM, SSD_STATE), lambda b, g: (b + s_off, g, 0, 0)),
                  rowblk(gw), rowblk(SSD_STATE, nbg), rowblk(SSD_STATE, nbg + SSD_GROUPS),
                  rowblk(gw), rowblk(gw), rowblk(gw),
                  pl.BlockSpec((1, gw), lambda b, g: (0, g)), pl.BlockSpec((1, gw), lambda b, g: (0, g))],
        out_specs=[pl.BlockSpec((1, hpg, SSD_HEAD_DIM, SSD_STATE), lambda b, g: (b, g, 0, 0)),
                   rowblk(gw)],
        out_shape=[jax.ShapeDtypeStruct((Bs,) + s0.shape[1:], F32), jax.ShapeDtypeStruct((Bs, 1, d_inner), BF16)],
        scratch_shapes=[pltpu.VMEM((1, gw), F32)],
        compiler_params=_cparams(("parallel", "parallel")), name="ssd_sample_state")(
            s0, r3(xcs), r3(xcs), r3(xcs), r3(dtx), r3(dax), r3(z),
            jnp.repeat(d_skip, SSD_HEAD_DIM).reshape(1, d_inner), norm_g.reshape(1, d_inner))
    return yn.reshape(Bs, d_inner), s_new


def _paged_topk_kernel(pt_ref, *refs, nblk, bps, ppb):
    pages = refs[:bps * ppb]
    q_ref, idx_ref, kmean_ref = refs[bps * ppb:]
    n = pl.program_id(1)
    for j in range(bps):
        tot = jnp.sum(pages[j * ppb][0, 0], axis=0)
        for r in range(1, ppb):
            tot = tot + jnp.sum(pages[j * ppb + r][0, 0], axis=0)
        kmean_ref[n * bps + j] = tot / float(MOBA_BLOCK)

    @pl.when(n == pl.num_programs(1) - 1)
    def _():
        gate = jnp.sum(kmean_ref[...] * q_ref[...], axis=-1, keepdims=True)
        blk_id = lax.broadcasted_iota(jnp.int32, gate.shape, 0)
        lane = lax.broadcasted_iota(jnp.int32, (A_HEADS, LANES), 1)
        out = jnp.zeros((A_HEADS, LANES), jnp.int32)
        for t in range(MOBA_TOPK):
            m = jnp.max(gate, axis=0, keepdims=True)
            idx = jnp.min(jnp.where(gate == m, blk_id, nblk), axis=0, keepdims=True)
            out = jnp.where(lane == t, idx[0], out)
            gate = jnp.where(blk_id == idx, -jnp.inf, gate)
        idx_ref[0] = out


def _paged_attn_kernel(pt_ref, top_ref, q_ref, kn_ref, vn_ref, k_hbm, v_hbm, o_ref, kbuf, vbuf, sem, *, layer, ppb):
    b = pl.program_id(0)
    n_pages = MOBA_TOPK * ppb

    def page_copies(seq, slot):
        cps = []
        for h in range(A_HEADS):
            for sel in range(MOBA_TOPK):
                blk = top_ref[(seq * A_HEADS + h) * MOBA_TOPK + sel]
                for r in range(ppb):
                    page = pt_ref[seq, blk * ppb + r]
                    i = h * n_pages + sel * ppb + r
                    cps.append(pltpu.make_async_copy(k_hbm.at[layer, page, :, h, :], kbuf.at[slot, i], sem.at[0, slot]))
                    cps.append(pltpu.make_async_copy(v_hbm.at[layer, page, :, h, :], vbuf.at[slot, i], sem.at[1, slot]))
        return cps

    @pl.when(b == 0)
    def _():
        for cp in page_copies(0, 0):
            cp.start()

    @pl.when(b + 1 < pl.num_programs(0))
    def _():
        for cp in page_copies(b + 1, (b + 1) % 2):
            cp.start()

    slot = b % 2
    for cp in page_copies(b, slot):
        cp.wait()

    rows = []
    for h in range(A_HEADS):
        q = q_ref[0, h:h + 1, :]
        qh = jnp.broadcast_to(q, (SUBLANES, HEAD_DIM)).astype(BF16)
        s_own = jnp.sum(q * kn_ref[0, h:h + 1, :], axis=-1, keepdims=True) * ATTN_SCALE
        ss = [lax.dot_general(qh, kbuf[slot, h * n_pages + i].astype(BF16), NT_DIMS,
                              preferred_element_type=F32) * ATTN_SCALE for i in range(n_pages)]
        m = s_own
        for sc in ss:
            m = jnp.maximum(m, jnp.max(sc[0:1], axis=-1, keepdims=True))
        p_own = jnp.exp(s_own - m)
        l = p_own
        acc = p_own * vn_ref[0, h:h + 1, :]
        for i, sc in enumerate(ss):
            p = jnp.exp(sc - m)
            l = l + jnp.sum(p[0:1], axis=-1, keepdims=True)
            acc = acc + jnp.dot(p.astype(BF16), vbuf[slot, h * n_pages + i].astype(BF16),
                                preferred_element_type=F32)[0:1]
        rows.append(acc / l)
    o_ref[0] = jnp.concatenate(rows, axis=0)


def _paged_moba(q, k_new, v_new, k_pool, v_pool, page_table, layer):
    Bs = q.shape[0]
    assert q.shape == (Bs, A_HEADS, HEAD_DIM)
    n_pages = page_table.shape[1]
    ppb = MOBA_BLOCK // PAGE_SIZE
    assert n_pages % ppb == 0
    nblk = n_pages // ppb
    assert nblk >= MOBA_TOPK
    bps = math.gcd(nblk, 4)
    q3, k3, v3 = q, k_new, v_new
    page_blk = (1, 1, PAGE_SIZE, A_HEADS, HEAD_DIM)

    def mean_spec(p):
        return pl.BlockSpec(page_blk, lambda b, n, pt, p=p: (layer, pt[b, n * bps * ppb + p], 0, 0, 0))

    idx = pl.pallas_call(
        functools.partial(_paged_topk_kernel, nblk=nblk, bps=bps, ppb=ppb),
        grid_spec=pltpu.PrefetchScalarGridSpec(
            num_scalar_prefetch=1, grid=(Bs, nblk // bps),
            in_specs=[mean_spec(p) for p in range(bps * ppb)]
            + [pl.BlockSpec((1, A_HEADS, HEAD_DIM), lambda b, n, pt: (b, 0, 0))],
            out_specs=pl.BlockSpec((1, A_HEADS, LANES), lambda b, n, pt: (b, 0, 0)),
            scratch_shapes=[pltpu.VMEM((nblk, A_HEADS, HEAD_DIM), F32)]),
        out_shape=jax.ShapeDtypeStruct((Bs, A_HEADS, LANES), jnp.int32),
        compiler_params=_cparams(("parallel", "arbitrary")), name="paged_topk")(
            page_table, *([k_pool] * (bps * ppb)), q3)
    top = idx[:, :, :MOBA_TOPK].reshape(-1)

    n_sel_pages = MOBA_TOPK * ppb
    seq_spec = pl.BlockSpec((1, A_HEADS, HEAD_DIM), lambda b, pt, tp: (b, 0, 0))
    hbm_spec = pl.BlockSpec(memory_space=pl.ANY)
    att = pl.pallas_call(
        functools.partial(_paged_attn_kernel, layer=layer, ppb=ppb),
        grid_spec=pltpu.PrefetchScalarGridSpec(
            num_scalar_prefetch=2, grid=(Bs,),
            in_specs=[seq_spec, seq_spec, seq_spec, hbm_spec, hbm_spec],
            out_specs=seq_spec,
            scratch_shapes=[pltpu.VMEM((2, A_HEADS * n_sel_pages, PAGE_SIZE, HEAD_DIM), F32),
                            pltpu.VMEM((2, A_HEADS * n_sel_pages, PAGE_SIZE, HEAD_DIM), F32),
                            pltpu.SemaphoreType.DMA((2, 2))]),
        out_shape=jax.ShapeDtypeStruct((Bs, A_HEADS, HEAD_DIM), F32),
        compiler_params=_cparams(("arbitrary",)), name="paged_attn")(
            page_table, top, q3, k3, v3, k_pool, v_pool)
    return att


def _rope_tables(pos):
    inv = ROPE_THETA ** (-jnp.arange(ROT_HALF, dtype=F32) / ROT_HALF)
    ang = pos.astype(F32)[:, None] * inv
    cos, sin = jnp.cos(ang), jnp.sin(ang)
    T = pos.shape[0]
    rest = HEAD_DIM - ROT_DIM
    cos_t = jnp.concatenate([cos, cos, jnp.ones((T, rest), F32)], axis=-1)
    sin_t = jnp.concatenate([-sin, sin, jnp.zeros((T, rest), F32)], axis=-1)
    return cos_t, sin_t


ROW_TILE = 1024


def _row_tile(m):
    return ROW_TILE if m % ROW_TILE == 0 else m


def kernel(x_prompt, x_sample, cache_k, cache_v, state_lru_conv, state_lru_h, state_ssd_conv, state_ssd_h, page_table, norm_mix, norm_mlp, w_in_a, q_norm, k_norm, lru_conv_w, lru_conv_b, lru_wa, lru_ba, lru_wx, lru_bx, lru_lambda, w_out_a, w_in_s, ssd_conv_w, ssd_conv_b, ssd_dt_bias, ssd_a_log, ssd_d, ssd_norm, w_out_s, w_up, w_down):
    Bp, Tp, D = x_prompt.shape
    Bs, Ts, _ = x_sample.shape
    assert Ts == 1
    depth = norm_mix.shape[0]
    past_len = page_table.shape[1] * PAGE_SIZE
    Mp, Ms = Bp * Tp, Bs * Ts
    tmp, tms = _row_tile(Mp), _row_tile(Ms)
    tmo = tmp // 2 if tmp == ROW_TILE else tmp
    tmr = tmp // 4 if tmp == ROW_TILE else tmp
    assert Tp % tmr == 0
    assert Tp % tmp == 0
    xp = x_prompt.reshape(Mp, D)
    xs = x_sample.reshape(Ms, D)
    lru_w = lru_lambda.shape[1]
    d_inner = ssd_norm.shape[1]
    xbc_w = ssd_conv_w.shape[2]
    n_ssd_heads = ssd_d.shape[1]

    cos_p, sin_p = _rope_tables(jnp.arange(Tp, dtype=jnp.int32))
    cos_s, sin_s = _rope_tables(jnp.full((Ms,), past_len, dtype=jnp.int32))

    outs = {k: [] for k in ("kp", "vp", "ks", "vs", "lcp", "lcs", "lhp", "lhs", "scp", "scs", "shp", "shs")}
    for layer in range(depth):
        i = layer // 2
        if layer % 2 == 0:
            w_in = w_in_a[i].astype(BF16)
            w_out = w_out_a[i].astype(BF16)
            splits = [A_W, A_W, A_W, lru_w, lru_w]
            lru_args = (lru_conv_w[i], lru_conv_b[i], lru_wa[i], lru_ba[i], lru_wx[i], lru_bx[i], lru_lambda[i])
            qkv_slabs = (True, True, True, False, False)
            heads_last = lambda a, b, t: a.reshape(A_HEADS, b, t, HEAD_DIM).transpose(1, 2, 0, 3)
            q, k, v, xr, yg = _norm_proj_resident(xp, norm_mix[layer], w_in, splits, tm=tmr, slab=qkv_slabs,
                                                  rope=((q_norm[i], k_norm[i]), cos_p, sin_p, Tp // tmr))
            att = _moba_prompt(q, k, v, batch=Bp, seq=Tp)
            rec, h_last = _lru_prompt(xr, yg, *lru_args, batch=Bp, seq=Tp)
            mixed_p = _out_proj([att, rec], w_out, xp, tm=tmo)
            outs["kp"].append(heads_last(k, Bp, Tp))
            outs["vp"].append(heads_last(v, Bp, Tp))
            outs["lcp"].append(xr.reshape(Bp, Tp, lru_w)[:, Tp - (CONV_K - 1):])
            outs["lhp"].append(h_last.reshape(Bp, lru_w))
            q, k, v, xr, yg = _norm_proj_resident(xs, norm_mix[layer], w_in, splits, tm=tms, slab=qkv_slabs,
                                                  rope=((q_norm[i], k_norm[i]), cos_s, sin_s, 1))
            att = _paged_moba(q.swapaxes(0, 1), k.swapaxes(0, 1), v.swapaxes(0, 1), cache_k, cache_v, page_table,
                              layer=i)
            att = att.swapaxes(0, 1).astype(BF16)
            buf = state_lru_conv[i]
            rec, h_new = _lru_sample(xr, yg, buf.swapaxes(0, 1), state_lru_h[i], *lru_args)
            mixed_s = _out_proj([att, rec], w_out, xs, tm=tms)
            outs["ks"].append(heads_last(k, Bs, Ts))
            outs["vs"].append(heads_last(v, Bs, Ts))
            outs["lcs"].append(jnp.concatenate([buf[:, 1:], xr[:, None, :]], axis=1))
            outs["lhs"].append(h_new)
        else:
            main_w = d_inner + xbc_w
            w_z = w_in_s[i][:, :d_inner].astype(BF16)
            w_x = w_in_s[i][:, d_inner:main_w].astype(BF16)
            w_dt = jnp.pad(w_in_s[i][:, main_w:], ((0, 0), (0, LANES - n_ssd_heads))).astype(BF16)
            w_out = w_out_s[i].astype(BF16)
            ssd_args = (ssd_conv_w[i], ssd_conv_b[i], ssd_dt_bias[i], ssd_a_log[i], ssd_d[i], ssd_norm[i])
            unslab = lambda a: a.swapaxes(0, 1).reshape(a.shape[1], -1)
            z, dt = _norm_proj_resident(xp, norm_mix[layer], w_z, [d_inner], tm=tmo, extra_w=w_dt, slab=True)
            xbc, = _norm_proj_resident(xp, norm_mix[layer], w_x, [xbc_w], tm=tmr, slab=True)
            yn, s_last = _ssd_prompt(xbc, z, dt, *ssd_args, batch=Bp, seq=Tp)
            mixed_p = _out_proj([yn], w_out, xp, tm=tmo)
            tail = xbc.reshape(-1, Bp, Tp, LANES)[:, :, Tp - (CONV_K - 1):]
            outs["scp"].append(tail.transpose(1, 2, 0, 3).reshape(Bp, CONV_K - 1, xbc_w))
            outs["shp"].append(s_last.reshape(Bp, n_ssd_heads, SSD_HEAD_DIM, SSD_STATE))
            z, dt = _norm_proj_resident(xs, norm_mix[layer], w_z, [d_inner], tm=tms, extra_w=w_dt, slab=True)
            xbc, = _norm_proj_resident(xs, norm_mix[layer], w_x, [xbc_w], tm=tms, slab=True)
            z, xbc = unslab(z), unslab(xbc)
            buf = state_ssd_conv[i]
            yn, s_new = _ssd_sample(xbc, buf.swapaxes(0, 1), z, dt,
                                    state_ssd_h.reshape((-1,) + state_ssd_h.shape[2:]), *ssd_args, s_off=i * Bs)
            mixed_s = _out_proj([yn], w_out, xs, tm=tms)
            outs["scs"].append(jnp.concatenate([buf[:, 1:], xbc[:, None, :]], axis=1))
            outs["shs"].append(s_new)
        wu, wd = w_up[layer].astype(BF16), w_down[layer].astype(BF16)
        xp = _mlp(mixed_p, norm_mlp[layer], wu, wd, tm=tmp)
        xs = _mlp(mixed_s, norm_mlp[layer], wu, wd, tm=tms)

    st = lambda key: jnp.stack(outs[key])
    return (xp.reshape(Bp, Tp, D), xs.reshape(Bs, Ts, D), st("kp"), st("vp"), st("ks"), st("vs"),
            st("lcp"), st("lcs"), st("lhp"), st("lhs"), st("scp"), st("scs"), st("shp"), st("shs"))
```

```python
import functools
import math

import jax
import jax.numpy as jnp
from jax import lax
from jax.experimental import pallas as pl
from jax.experimental.pallas import tpu as pltpu

F32 = jnp.float32
BF16 = jnp.bfloat16
HIGHEST = lax.Precision.HIGHEST

A_HEADS = 8
HEAD_DIM = 128
A_W = A_HEADS * HEAD_DIM
ROT_DIM = HEAD_DIM // 4
ROT_HALF = ROT_DIM // 2
ROPE_THETA = 500000.0
MOBA_BLOCK = 256
MOBA_TOPK = 3
ATTN_SCALE = HEAD_DIM ** -0.5
LRU_HEADS = 8
LRU_C = 8.0
CONV_K = 4
SSD_HEAD_DIM = 64
SSD_STATE = 128
SSD_GROUPS = 8
SSD_CHUNK = 128
PAGE_SIZE = 128
EPS = 1e-6

LANES = 128
SUBLANES = 8
NEG_BIG = -1e30
LOG2E = math.log2(math.e)
VMEM_BYTES_V7X = 64 * 1024 * 1024
VMEM_LIMIT = VMEM_BYTES_V7X - 8 * 1024 * 1024
VMEM_LIMIT_MLP = VMEM_BYTES_V7X - 4 * 1024 * 1024

NT_DIMS = (((1,), (1,)), ((), ()))


def _cparams(sem, vmem=VMEM_LIMIT):
    return pltpu.CompilerParams(dimension_semantics=sem, vmem_limit_bytes=vmem)


def _sigmoid(x):
    return 0.5 * (jnp.tanh(0.5 * x) + 1.0)


def _silu(x):
    h = 0.5 * x
    return h * (jnp.tanh(h) + 1.0)


def _rms_rows(x, g):
    return x * lax.rsqrt(jnp.mean(x * x, axis=-1, keepdims=True) + EPS) * g


def _head_norm_rope(acc, g, cos, sin):
    y = _rms_rows(acc, g)
    lane = lax.broadcasted_iota(jnp.int32, y.shape, 1)
    partner = jnp.where(lane < ROT_HALF, pltpu.roll(y, LANES - ROT_HALF, 1), pltpu.roll(y, ROT_HALF, 1))
    return y * cos + partner * sin


RESIDENT_DOT_COLS = 1024


def _norm_proj_resident_kernel(*refs, splits, n_rope, has_extra, slab):
    it = iter(refs)
    x_ref, g_ref, w_ref = next(it), next(it), next(it)
    rope_g = [next(it) for _ in range(n_rope)]
    if n_rope:
        cos_ref, sin_ref = next(it), next(it)
    if has_extra:
        we_ref = next(it)
    outs = [next(it) for _ in splits]
    xn = _rms_rows(x_ref[...], g_ref[...]).astype(BF16)
    if has_extra:
        next(it)[...] = jnp.dot(xn, we_ref[...], preferred_element_type=F32)
    col = 0
    for s, width in enumerate(splits):
        for c0 in range(0, width, RESIDENT_DOT_COLS):
            acc = jnp.dot(xn, w_ref[:, col + c0:col + c0 + RESIDENT_DOT_COLS], preferred_element_type=F32)
            for hh in range(RESIDENT_DOT_COLS // LANES):
                sl = slice(hh * LANES, (hh + 1) * LANES)
                val = acc[:, sl]
                if s < n_rope:
                    val = _head_norm_rope(val, rope_g[s][...], cos_ref[...], sin_ref[...])
                if slab[s]:
                    outs[s][c0 // LANES + hh] = val
                else:
                    outs[s][:, c0 + hh * LANES:c0 + (hh + 1) * LANES] = val
        col += width


def _norm_proj_resident(x, g, w, splits, *, tm, rope=None, extra_w=None, slab=False):
    M, K = x.shape
    N = sum(splits)
    assert w.shape == (K, N) and M % tm == 0 and all(s % RESIDENT_DOT_COLS == 0 for s in splits)
    assert HEAD_DIM == LANES
    n_rope = 0 if rope is None else len(rope[0])
    in_specs = [pl.BlockSpec((tm, K), lambda i: (i, 0)),
                pl.BlockSpec((1, K), lambda i: (0, 0)),
                pl.BlockSpec((K, N), lambda i: (0, 0), pipeline_mode=pl.Buffered(1))]
    args = [x, g.reshape(1, K), w]
    if rope is not None:
        gains, cos, sin, period = rope
        for gg in gains:
            in_specs.append(pl.BlockSpec((1, HEAD_DIM), lambda i: (0, 0)))
            args.append(gg.reshape(1, HEAD_DIM))
        for tab in (cos, sin):
            in_specs.append(pl.BlockSpec((tm, HEAD_DIM), lambda i, period=period: (i % period, 0)))
            args.append(tab)
    if extra_w is not None:
        in_specs.append(pl.BlockSpec((K, LANES), lambda i: (0, 0)))
        args.append(extra_w)
    slab = tuple(slab) if isinstance(slab, (tuple, list)) else (bool(slab),) * len(splits)
    out_specs, out_shape = [], []
    for s, as_slab in zip(splits, slab):
        if as_slab:
            out_specs.append(pl.BlockSpec((s // LANES, tm, LANES), lambda i: (0, i, 0)))
            out_shape.append(jax.ShapeDtypeStruct((s // LANES, M, LANES), F32))
        else:
            out_specs.append(pl.BlockSpec((tm, s), lambda i: (i, 0)))
            out_shape.append(jax.ShapeDtypeStruct((M, s), F32))
    if extra_w is not None:
        out_specs.append(pl.BlockSpec((tm, LANES), lambda i: (i, 0)))
        out_shape.append(jax.ShapeDtypeStruct((M, LANES), F32))
    return pl.pallas_call(
        functools.partial(_norm_proj_resident_kernel, splits=tuple(splits), n_rope=n_rope,
                          has_extra=extra_w is not None, slab=slab),
        grid=(M // tm,), in_specs=in_specs, out_specs=out_specs, out_shape=out_shape,
        compiler_params=_cparams(("parallel",)), name="norm_proj_resident")(*args)


def _out_proj_kernel(*refs, n_parts, kp):
    xs = refs[:n_parts]
    w_ref, res_ref, o_ref = refs[n_parts:]
    acc = res_ref[...]
    for p, x_ref in enumerate(xs):
        acc = acc + jnp.dot(x_ref[...], w_ref[p * kp:(p + 1) * kp, :], preferred_element_type=F32)
    o_ref[...] = acc


def _out_proj(parts, w, res, *, tm):
    M, N = res.shape
    kp = parts[0].shape[1]
    assert all(p.shape == (M, kp) for p in parts) and w.shape == (kp * len(parts), N) and M % tm == 0
    in_specs = [pl.BlockSpec((tm, kp), lambda i: (i, 0)) for _ in parts]
    in_specs.append(pl.BlockSpec(w.shape, lambda i: (0, 0), pipeline_mode=pl.Buffered(1)))
    in_specs.append(pl.BlockSpec((tm, N), lambda i: (i, 0)))
    return pl.pallas_call(
        functools.partial(_out_proj_kernel, n_parts=len(parts), kp=kp),
        grid=(M // tm,), in_specs=in_specs,
        out_specs=pl.BlockSpec((tm, N), lambda i: (i, 0)),
        out_shape=jax.ShapeDtypeStruct((M, N), F32),
        compiler_params=_cparams(("parallel",)), name="out_proj")(*parts, w, res)


def _mlp_kernel(x_ref, g_ref, wu_ref, wd_ref, o_ref, xn_ref):
    f = pl.program_id(1)

    @pl.when(f == 0)
    def _():
        x = x_ref[...]
        xn_ref[...] = _rms_rows(x, g_ref[...]).astype(BF16)
        o_ref[...] = x

    h = jnp.maximum(jnp.dot(xn_ref[...], wu_ref[...], preferred_element_type=F32), 0.0)
    o_ref[...] += jnp.dot((h * h).astype(BF16), wd_ref[...], preferred_element_type=F32)


def _mlp(x, g, w_up, w_down, layer, *, tm, tf=1024):
    M, D = x.shape
    FF = w_up.shape[2]
    return pl.pallas_call(
        _mlp_kernel, grid=(M // tm, FF // tf),
        in_specs=[pl.BlockSpec((tm, D), lambda i, f: (i, 0)),
                  pl.BlockSpec((1, D), lambda i, f: (0, 0)),
                  pl.BlockSpec((None, D, tf), lambda i, f: (layer, 0, f)),
                  pl.BlockSpec((None, tf, D), lambda i, f: (layer, f, 0))],
        out_specs=pl.BlockSpec((tm, D), lambda i, f: (i, 0)),
        out_shape=jax.ShapeDtypeStruct((M, D), F32),
        scratch_shapes=[pltpu.VMEM((tm, D), BF16)],
        compiler_params=_cparams(("parallel", "arbitrary"), VMEM_LIMIT_MLP), name="mlp")(
            x, g.reshape(1, D), w_up, w_down)


def _moba_prompt_kernel(q_ref, k_ref, v_ref, hot_ref, o_ref, kaug_ref, vb_ref, kmean_ref, *, nb):
    blk, D = MOBA_BLOCK, HEAD_DIM
    kaug_ref[:, 0:D] = k_ref[...].astype(BF16)
    kaug_ref[:, D:2 * D] = hot_ref[...]
    vb_ref[...] = v_ref[...].astype(BF16)
    kmean_ref[...] = jnp.zeros(kmean_ref.shape, F32)
    for n in range(nb):
        kmean_ref[n:n + 1, :] = jnp.mean(k_ref[n * blk:(n + 1) * blk, :], axis=0, keepdims=True)

    lane = lax.broadcasted_iota(jnp.int32, (blk, LANES), 1)
    row = lax.broadcasted_iota(jnp.int32, (blk, blk), 0)
    col = lax.broadcasted_iota(jnp.int32, (blk, blk), 1)
    nbp = kmean_ref.shape[0]
    blk_id = lax.broadcasted_iota(jnp.int32, (nbp, blk), 0)
    eye = (row == col).astype(BF16)
    for qb in range(nb):
        q = q_ref[qb * blk:(qb + 1) * blk, :]
        if qb > MOBA_TOPK:
            gate = lax.dot_general(kmean_ref[...], q, NT_DIMS, precision=HIGHEST, preferred_element_type=F32)
            gate = jnp.where(blk_id < qb, gate, -jnp.inf)
            bias_t = jnp.where(blk_id == qb, 0.0, NEG_BIG)
            for _ in range(MOBA_TOPK):
                m = jnp.max(gate, axis=0, keepdims=True)
                idx = jnp.min(jnp.where(gate == m, blk_id, nbp), axis=0, keepdims=True)
                hit = blk_id == idx
                bias_t = jnp.where(hit & (m > -jnp.inf), 0.0, bias_t)
                gate = jnp.where(hit, -jnp.inf, gate)
            pad = jnp.zeros((LANES - nbp, blk), F32)
            bias = lax.dot_general(eye, jnp.concatenate([bias_t, pad], axis=0).astype(BF16), NT_DIMS,
                                   preferred_element_type=F32).astype(BF16)
        else:
            bias = jnp.where(lane <= qb, 0.0, NEG_BIG).astype(BF16)
        lhs = jnp.concatenate([(q * (ATTN_SCALE * LOG2E)).astype(BF16), bias], axis=1)
        own = slice(qb * blk, (qb + 1) * blk)
        s_own = lax.dot_general(lhs, kaug_ref[own, :], NT_DIMS, preferred_element_type=F32)
        s_own = jnp.where(col <= row, s_own, NEG_BIG)
        m = jnp.max(s_own, axis=-1, keepdims=True)
        if qb > 0:
            s_past = lax.dot_general(lhs, kaug_ref[0:qb * blk, :], NT_DIMS, preferred_element_type=F32)
            m = jnp.maximum(m, jnp.max(s_past, axis=-1, keepdims=True))
        p = jnp.exp2(s_own - m)
        l = jnp.sum(p, axis=-1, keepdims=True)
        acc = jnp.dot(p.astype(BF16), vb_ref[own, :], preferred_element_type=F32)
        if qb > 0:
            p = jnp.exp2(s_past - m)
            l = l + jnp.sum(p, axis=-1, keepdims=True)
            acc = acc + jnp.dot(p.astype(BF16), vb_ref[0:qb * blk, :], preferred_element_type=F32)
        o_ref[own, :] = (acc / l).astype(o_ref.dtype)


def _moba_prompt(q, k, v, *, batch, seq):
    M = q.shape[0]
    nb = seq // MOBA_BLOCK
    assert nb <= LANES and seq % MOBA_BLOCK == 0 and q.shape == (M, A_W)
    hot = (lax.broadcasted_iota(jnp.int32, (seq, LANES), 0) // MOBA_BLOCK
           == lax.broadcasted_iota(jnp.int32, (seq, LANES), 1)).astype(BF16)
    seq_head = pl.BlockSpec((seq, HEAD_DIM), lambda b, h: (b, h))
    return pl.pallas_call(
        functools.partial(_moba_prompt_kernel, nb=nb),
        grid=(batch, A_HEADS),
        in_specs=[seq_head, seq_head, seq_head, pl.BlockSpec((seq, LANES), lambda b, h: (0, 0))],
        out_specs=seq_head,
        out_shape=jax.ShapeDtypeStruct((M, A_W), BF16),
        scratch_shapes=[pltpu.VMEM((seq, 2 * HEAD_DIM), BF16), pltpu.VMEM((seq, HEAD_DIM), BF16),
                        pltpu.VMEM((-(-nb // SUBLANES) * SUBLANES, HEAD_DIM), F32)],
        compiler_params=_cparams(("parallel", "parallel")), name="moba_prompt")(q, k, v, hot)


def _lru_gates(xc, wa_ref, ba, wx_ref, bx, lam):
    bw = xc.shape[1] // LRU_HEADS
    rs, is_ = [], []
    for n in range(LRU_HEADS):
        xb = xc[:, n * bw:(n + 1) * bw].astype(BF16)
        rs.append(jnp.dot(xb, wa_ref[n], preferred_element_type=F32))
        is_.append(jnp.dot(xb, wx_ref[n], preferred_element_type=F32))
    r = _sigmoid(jnp.concatenate(rs, axis=-1) + ba)
    i = _sigmoid(jnp.concatenate(is_, axis=-1) + bx)
    log_a = -LRU_C * r * jax.nn.softplus(-lam)
    a = jnp.exp(log_a)
    t = jnp.tanh(log_a)
    u = jnp.sqrt(-2.0 * t / (1.0 - t)) * (i * xc)
    return a, u


def _lru_prompt_kernel(xr_ref, yg_ref, cw_ref, cb_ref, wa_ref, ba_ref, wx_ref, bx_ref, lam_ref,
                       rec_ref, hl_ref, xx_ref, a_ref, u_ref, h_ref, *, tt):
    ti = pl.program_id(1)
    pad = SUBLANES

    @pl.when(ti == 0)
    def _():
        xx_ref[0:pad, :] = jnp.zeros((pad, xx_ref.shape[1]), F32)
        h_ref[...] = jnp.zeros(h_ref.shape, F32)

    @pl.when(ti > 0)
    def _():
        xx_ref[0:pad, :] = xx_ref[tt:tt + pad, :]

    xx_ref[pad:pad + tt, :] = xr_ref[...]
    xc = cb_ref[...]
    for j in range(CONV_K):
        off = pad - (CONV_K - 1) + j
        xc = xc + xx_ref[off:off + tt, :] * cw_ref[j:j + 1, :]
    a, u = _lru_gates(xc, wa_ref, ba_ref[...], wx_ref, bx_ref[...], lam_ref[...])
    a_ref[...] = a
    u_ref[...] = u

    sub = lax.broadcasted_iota(jnp.int32, (SUBLANES, a.shape[1]), 0)

    def group(gi, h):
        r0 = pl.multiple_of(gi * SUBLANES, SUBLANES)
        a8 = a_ref[pl.ds(r0, SUBLANES), :]
        u8 = u_ref[pl.ds(r0, SUBLANES), :]
        for s in (1, 2, 4):
            keep = sub >= s
            a_sh = pltpu.roll(a8, s, 0)
            u_sh = pltpu.roll(u8, s, 0)
            u8 = jnp.where(keep, a8 * u_sh + u8, u8)
            a8 = jnp.where(keep, a8 * a_sh, a8)
        hs = a8 * h + u8
        yg = yg_ref[pl.ds(r0, SUBLANES), :]
        rec_ref[pl.ds(r0, SUBLANES), :] = (hs * jax.nn.gelu(yg)).astype(rec_ref.dtype)
        return hs[SUBLANES - 1:SUBLANES, :]

    h = lax.fori_loop(0, tt // SUBLANES, group, h_ref[...])
    h_ref[...] = h

    @pl.when(ti == pl.num_programs(1) - 1)
    def _():
        hl_ref[0] = h


def _lru_prompt(xr, yg, cw, cb, wa, ba, wx, bx, lam, *, batch, seq, tt=256):
    M, W = xr.shape
    nt = seq // tt
    row = lambda a: a.reshape(1, W)
    full2 = lambda shp: pl.BlockSpec(shp, lambda b, t: (0, 0))
    full3 = lambda shp: pl.BlockSpec(shp, lambda b, t: (0, 0, 0))
    return pl.pallas_call(
        functools.partial(_lru_prompt_kernel, tt=tt),
        grid=(batch, nt),
        in_specs=[pl.BlockSpec((tt, W), lambda b, t: (b * nt + t, 0)),
                  pl.BlockSpec((tt, W), lambda b, t: (b * nt + t, 0)),
                  full2((CONV_K, W)), full2((1, W)), full3(wa.shape), full2((1, W)),
                  full3(wx.shape), full2((1, W)), full2((1, W))],
        out_specs=[pl.BlockSpec((tt, W), lambda b, t: (b * nt + t, 0)),
                   pl.BlockSpec((1, 1, W), lambda b, t: (b, 0, 0))],
        out_shape=[jax.ShapeDtypeStruct((M, W), BF16), jax.ShapeDtypeStruct((batch, 1, W), F32)],
        scratch_shapes=[pltpu.VMEM((tt + SUBLANES, W), F32), pltpu.VMEM((tt, W), F32),
                        pltpu.VMEM((tt, W), F32), pltpu.VMEM((1, W), F32)],
        compiler_params=_cparams(("parallel", "arbitrary")), name="lru_prompt")(
            xr, yg, cw, row(cb), wa.astype(BF16), row(ba), wx.astype(BF16), row(bx), row(lam))


def _lru_sample_kernel(xr_ref, yg_ref, buf_ref, h0_ref, cw_ref, cb_ref, wa_ref, ba_ref, wx_ref, bx_ref,
                       lam_ref, rec_ref, hn_ref):
    xc = cb_ref[...]
    for j in range(CONV_K - 1):
        xc = xc + buf_ref[j] * cw_ref[j:j + 1, :]
    xc = xc + xr_ref[...] * cw_ref[CONV_K - 1:CONV_K, :]
    a, u = _lru_gates(xc, wa_ref, ba_ref[...], wx_ref, bx_ref[...], lam_ref[...])
    h = a * h0_ref[...] + u
    hn_ref[...] = h
    rec_ref[...] = (h * jax.nn.gelu(yg_ref[...])).astype(rec_ref.dtype)


def _lru_sample(xr, yg, buf_t, h0, cw, cb, wa, ba, wx, bx, lam):
    Bs, W = xr.shape
    row = lambda a: a.reshape(1, W)
    return pl.pallas_call(
        _lru_sample_kernel,
        out_shape=[jax.ShapeDtypeStruct((Bs, W), BF16), jax.ShapeDtypeStruct((Bs, W), F32)],
        compiler_params=pltpu.CompilerParams(vmem_limit_bytes=VMEM_LIMIT), name="lru_sample")(
            xr, yg, buf_t, h0, cw, row(cb), wa.astype(BF16), row(ba), wx.astype(BF16), row(bx), row(lam))


def _ssd_prompt_kernel(xbc_ref, prev_ref, z_ref, dt_ref, cw_ref, cb_ref, dtb_ref, alog_ref, dsk_ref, ng_ref,
                       ex_ref, yn_ref, sl_ref, xx_ref, xcv_ref, y_ref, st_ref, *, d_inner):
    c = pl.program_id(1)
    L = SSD_CHUNK
    pad = SUBLANES
    gw = d_inner // SSD_GROUPS
    spg = gw // LANES
    hps = LANES // SSD_HEAD_DIM
    head_of_lane = lax.broadcasted_iota(jnp.int32, (L, LANES), 1) // SSD_HEAD_DIM
    assert SSD_STATE == LANES

    @pl.when(c == 0)
    def _():
        xx_ref[:, 0:pad, :] = jnp.zeros((xx_ref.shape[0], pad, LANES), F32)
        st_ref[...] = jnp.zeros(st_ref.shape, F32)

    @pl.when(c > 0)
    def _():
        xx_ref[:, 0:pad, :] = prev_ref[...]

    xx_ref[:, pad:pad + L, :] = xbc_ref[...]

    def conv_slab(k, carry):
        xs = xx_ref[k]
        xc = cb_ref[k]
        for j in range(CONV_K):
            back = CONV_K - 1 - j
            tap = xs if back == 0 else pltpu.roll(xs, back, 0)
            xc = xc + tap[pad:] * cw_ref[k, j:j + 1, :]
        xcv_ref[k] = _silu(xc)
        return carry

    lax.fori_loop(0, xx_ref.shape[0], conv_slab, 0)

    def slabs(ref, first, count):
        return jnp.concatenate([ref[first + i] for i in range(count)], axis=1)

    def split2(a):
        hi = a.astype(BF16)
        return hi, (a - hi.astype(F32)).astype(BF16)

    dt = jax.nn.softplus(dt_ref[...] + dtb_ref[...])
    dta = dt * (-jnp.exp(alog_ref[...]))
    tril = lax.broadcasted_iota(jnp.int32, (L, L), 1) <= lax.broadcasted_iota(jnp.int32, (L, L), 0)
    cs = jnp.dot(tril.astype(F32), dta, precision=HIGHEST, preferred_element_type=F32)
    cs_l2 = cs * LOG2E
    adj_t = (cs_l2 - jnp.log2(dt)).T
    last = cs[L - 1:L, :]
    w_hi, w_lo = split2(dt * jnp.exp(last - cs))
    e_hi, e_lo = split2(jnp.exp(cs))
    cdec = jnp.broadcast_to(jnp.exp(last), (SUBLANES, LANES))
    cd_hi, cd_rest = split2(cdec)
    cd_mid, cd_lo = split2(cdec - cd_hi.astype(F32))

    def expand(terms, exg):
        out = jnp.dot(terms[0], exg, preferred_element_type=F32)
        for t in terms[1:]:
            out = out + jnp.dot(t, exg, preferred_element_type=F32)
        return out

    for g in range(SSD_GROUPS):
        lo = g * gw
        exg = ex_ref[:, lo:lo + gw]
        w_x = expand((w_hi, w_lo), exg)
        ecs_x = expand((e_hi, e_lo), exg)
        cd_x = expand((cd_hi, cd_mid, cd_lo), exg)[0:1]
        xg = slabs(xcv_ref, g * spg, spg)
        bg = xcv_ref[d_inner // LANES + g]
        cg = xcv_ref[d_inner // LANES + SSD_GROUPS + g].astype(BF16)
        bg_t = bg.T.astype(BF16)
        cb = jnp.dot(cg, bg_t, preferred_element_type=F32)
        st_g = st_ref[:, lo:lo + gw]
        y_off = jnp.dot(cg, st_g.astype(BF16), preferred_element_type=F32) * ecs_x
        st_ref[:, lo:lo + gw] = st_g * cd_x + jnp.dot(bg_t, (xg * w_x).astype(BF16), preferred_element_type=F32)
        for i in range(spg):
            xs_h = xcv_ref[g * spg + i].astype(BF16)
            lhs, rhs = [], []
            for k in range(hps):
                h = (g * spg + i) * hps + k
                dec = jnp.exp2(jnp.where(tril, cs_l2[:, h:h + 1] - adj_t[h:h + 1, :], NEG_BIG))
                lhs.append((cb * dec).astype(BF16))
                rhs.append(jnp.where(head_of_lane == k, xs_h, jnp.zeros_like(xs_h)))
            yd = jnp.dot(jnp.concatenate(lhs, axis=1), jnp.concatenate(rhs, axis=0), preferred_element_type=F32)
            cols = slice(lo + i * LANES, lo + (i + 1) * LANES)
            y_ref[:, cols] = yd + y_off[:, i * LANES:(i + 1) * LANES]

    for g in range(SSD_GROUPS):
        lo = g * gw
        cols = slice(lo, lo + gw)
        y = y_ref[:, cols] + slabs(xcv_ref, g * spg, spg) * dsk_ref[:, cols]
        zz = slabs(z_ref, g * spg, spg)
        u = y * _silu(zz)
        yn_ref[:, cols] = _rms_rows(u, ng_ref[:, cols]).astype(yn_ref.dtype)

    @pl.when(c == pl.num_programs(1) - 1)
    def _():
        for k in range(d_inner // LANES):
            sl_ref[0, k * LANES:(k + 1) * LANES, :] = st_ref[:, k * LANES:(k + 1) * LANES].T


def _ssd_expand_matrix(n_heads, d_inner):
    h = lax.broadcasted_iota(jnp.int32, (LANES, d_inner), 0)
    col = lax.broadcasted_iota(jnp.int32, (LANES, d_inner), 1)
    return (col // SSD_HEAD_DIM == h).astype(F32) * (h < n_heads).astype(F32)


def _pad_lanes(v):
    return jnp.pad(v, (0, LANES - v.shape[0])).reshape(1, LANES)


def _ssd_prompt(xbc, z, dt, cw, cb, dt_bias, a_log, d_skip, norm_g, *, batch, seq):
    nxs, M, _ = xbc.shape
    nzs = z.shape[0]
    XW, d_inner = nxs * LANES, nzs * LANES
    n_heads = d_inner // SSD_HEAD_DIM
    L = SSD_CHUNK
    nc = seq // L
    rpb = L // SUBLANES
    const2 = lambda shp: pl.BlockSpec(shp, lambda b, c: (0, 0))
    const3 = lambda shp: pl.BlockSpec(shp, lambda b, c: (0, 0, 0))
    return pl.pallas_call(
        functools.partial(_ssd_prompt_kernel, d_inner=d_inner),
        grid=(batch, nc),
        in_specs=[pl.BlockSpec((nxs, L, LANES), lambda b, c: (0, b * nc + c, 0)),
                  pl.BlockSpec((nxs, SUBLANES, LANES), lambda b, c: (0, jnp.maximum((b * nc + c) * rpb - 1, 0), 0)),
                  pl.BlockSpec((nzs, L, LANES), lambda b, c: (0, b * nc + c, 0)),
                  pl.BlockSpec((L, LANES), lambda b, c: (b * nc + c, 0)),
                  const3((nxs, CONV_K, LANES)), const3((nxs, 1, LANES)), const2((1, LANES)), const2((1, LANES)),
                  const2((1, d_inner)), const2((1, d_inner)), const2((LANES, d_inner))],
        out_specs=[pl.BlockSpec((L, d_inner), lambda b, c: (b * nc + c, 0)),
                   pl.BlockSpec((1, d_inner, SSD_STATE), lambda b, c: (b, 0, 0))],
        out_shape=[jax.ShapeDtypeStruct((M, d_inner), BF16),
                   jax.ShapeDtypeStruct((batch, d_inner, SSD_STATE), F32)],
        scratch_shapes=[pltpu.VMEM((nxs, L + SUBLANES, LANES), F32), pltpu.VMEM((nxs, L, LANES), F32),
                        pltpu.VMEM((L, d_inner), F32), pltpu.VMEM((SSD_STATE, d_inner), F32)],
        compiler_params=_cparams(("parallel", "arbitrary")), name="ssd_prompt")(
            xbc, xbc, z, dt, cw.reshape(CONV_K, nxs, LANES).swapaxes(0, 1), cb.reshape(nxs, 1, LANES),
            _pad_lanes(dt_bias), _pad_lanes(a_log),
            jnp.repeat(d_skip, SSD_HEAD_DIM).reshape(1, d_inner), norm_g.reshape(1, d_inner),
            _ssd_expand_matrix(n_heads, d_inner).astype(BF16))


def _ssd_sample_pre_kernel(x_ref, buf_ref, dt_ref, cw_ref, cb_ref, dtb_ref, alog_ref, ex_ref,
                           xcs_ref, dtx_ref, dax_ref):
    xc = cb_ref[...]
    for j in range(CONV_K - 1):
        xc = xc + buf_ref[j] * cw_ref[j:j + 1, :]
    xc = xc + x_ref[...] * cw_ref[CONV_K - 1:CONV_K, :]
    xcs_ref[...] = _silu(xc)
    dt = jax.nn.softplus(dt_ref[...] + dtb_ref[...])
    da = jnp.exp(dt * (-jnp.exp(alog_ref[...])))
    ex = ex_ref[...]
    dtx_ref[...] = jnp.dot(dt, ex, precision=HIGHEST, preferred_element_type=F32)
    dax_ref[...] = jnp.dot(da, ex, precision=HIGHEST, preferred_element_type=F32)


TN_DIMS = (((0,), (0,)), ((), ()))


def _ssd_sample_state_kernel(s0_ref, xcs_ref, dtx_ref, dax_ref, z_ref, dsk_ref, ng_ref, sn_ref, yn_ref, *, d_inner):
    gw = d_inner // SSD_GROUPS
    hpg = gw // SSD_HEAD_DIM
    nst = SSD_STATE
    row_id = lax.broadcasted_iota(jnp.int32, (SUBLANES, gw), 0)
    row_id_n = lax.broadcasted_iota(jnp.int32, (SUBLANES, nst), 0)
    ones3 = jnp.where(row_id_n < 3, 1.0, 0.0).astype(BF16)
    xcs = xcs_ref[0]
    for g in range(SSD_GROUPS):
        cols = slice(g * gw, (g + 1) * gw)
        x = xcs[:, cols]
        b_row = xcs[:, d_inner + g * nst:d_inner + (g + 1) * nst]
        c_row = xcs[:, d_inner + SSD_GROUPS * nst + g * nst:d_inner + SSD_GROUPS * nst + (g + 1) * nst]
        xdt8 = jnp.where(row_id == 0, x * dtx_ref[0][:, cols], 0.0).astype(BF16)
        b8 = jnp.where(row_id_n == 0, b_row, 0.0).astype(BF16)
        outer = lax.dot_general(xdt8, b8, TN_DIMS, preferred_element_type=F32)
        da = dax_ref[0][:, cols]
        da_hi = da.astype(BF16).astype(F32)
        da_mid = (da - da_hi).astype(BF16).astype(F32)
        da_lo = da - da_hi - da_mid
        da8 = jnp.where(row_id == 0, da_hi, jnp.where(row_id == 1, da_mid, jnp.where(row_id == 2, da_lo, 0.0)))
        da_col = lax.dot_general(da8.astype(BF16), ones3, TN_DIMS, preferred_element_type=F32)
        heads = slice(g * hpg, (g + 1) * hpg)
        s_new = s0_ref[0, heads].reshape(gw, nst) * da_col + outer
        sn_ref[0, heads] = s_new.reshape(hpg, SSD_HEAD_DIM, nst)
        c8 = jnp.broadcast_to(c_row, (SUBLANES, nst)).astype(BF16)
        y = lax.dot_general(c8, s_new.astype(BF16), NT_DIMS, preferred_element_type=F32)[0:1]
        u = (y + x * dsk_ref[:, cols]) * _silu(z_ref[0][:, cols])
        yn_ref[0, :, cols] = _rms_rows(u, ng_ref[:, cols]).astype(yn_ref.dtype)


def _ssd_sample(xbc, buf_t, z, dt, s0, cw, cb, dt_bias, a_log, d_skip, norm_g, s_off=0):
    Bs, XW = xbc.shape
    d_inner = z.shape[1]
    n_heads = d_inner // SSD_HEAD_DIM
    gw = d_inner // SSD_GROUPS
    hpg = gw // SSD_HEAD_DIM
    ex = _ssd_expand_matrix(n_heads, d_inner)
    xcs, dtx, dax = pl.pallas_call(
        _ssd_sample_pre_kernel,
        out_shape=[jax.ShapeDtypeStruct((Bs, XW), F32), jax.ShapeDtypeStruct((Bs, d_inner), F32),
                   jax.ShapeDtypeStruct((Bs, d_inner), F32)],
        compiler_params=pltpu.CompilerParams(vmem_limit_bytes=VMEM_LIMIT), name="ssd_sample_pre")(
            xbc, buf_t, dt, cw, cb.reshape(1, XW), _pad_lanes(dt_bias), _pad_lanes(a_log), ex)
    r3 = lambda a: a.reshape(Bs, 1, a.shape[1])
    seq_row = lambda w: pl.BlockSpec((1, 1, w), lambda b: (b, 0, 0))
    state_blk = (1, n_heads, SSD_HEAD_DIM, SSD_STATE)
    s_new, yn = pl.pallas_call(
        functools.partial(_ssd_sample_state_kernel, d_inner=d_inner),
        grid=(Bs,),
        in_specs=[pl.BlockSpec(state_blk, lambda b: (b + s_off, 0, 0, 0)),
                  seq_row(XW), seq_row(d_inner), seq_row(d_inner), seq_row(d_inner),
                  pl.BlockSpec((1, d_inner), lambda b: (0, 0)), pl.BlockSpec((1, d_inner), lambda b: (0, 0))],
        out_specs=[pl.BlockSpec(state_blk, lambda b: (b, 0, 0, 0)), seq_row(d_inner)],
        out_shape=[jax.ShapeDtypeStruct((Bs,) + s0.shape[1:], F32), jax.ShapeDtypeStruct((Bs, 1, d_inner), BF16)],
        compiler_params=_cparams(("parallel",)), name="ssd_sample_state")(
            s0, r3(xcs), r3(dtx), r3(dax), r3(z),
            jnp.repeat(d_skip, SSD_HEAD_DIM).reshape(1, d_inner), norm_g.reshape(1, d_inner))
    return yn.reshape(Bs, d_inner), s_new


def _paged_topk_kernel(pt_ref, *refs, nblk, bps, ppb):
    pages = refs[:bps * ppb]
    q_ref, idx_ref, kmean_ref = refs[bps * ppb:]
    n = pl.program_id(1)
    for j in range(bps):
        tot = jnp.sum(pages[j * ppb][0, 0], axis=0)
        for r in range(1, ppb):
            tot = tot + jnp.sum(pages[j * ppb + r][0, 0], axis=0)
        kmean_ref[n * bps + j] = tot / float(MOBA_BLOCK)

    @pl.when(n == pl.num_programs(1) - 1)
    def _():
        gate = jnp.sum(kmean_ref[...] * q_ref[...], axis=-1, keepdims=True)
        blk_id = lax.broadcasted_iota(jnp.int32, gate.shape, 0)
        lane = lax.broadcasted_iota(jnp.int32, (A_HEADS, LANES), 1)
        out = jnp.zeros((A_HEADS, LANES), jnp.int32)
        for t in range(MOBA_TOPK):
            m = jnp.max(gate, axis=0, keepdims=True)
            idx = jnp.min(jnp.where(gate == m, blk_id, nblk), axis=0, keepdims=True)
            out = jnp.where(lane == t, idx[0], out)
            gate = jnp.where(blk_id == idx, -jnp.inf, gate)
        idx_ref[0] = out


def _paged_attn_kernel(pt_ref, top_ref, q_ref, kn_ref, vn_ref, k_hbm, v_hbm, o_ref, kbuf, vbuf, sem, *, layer, ppb):
    b = pl.program_id(0)
    n_pages = MOBA_TOPK * ppb

    def page_copies(seq, slot):
        cps = []
        for h in range(A_HEADS):
            for sel in range(MOBA_TOPK):
                blk = top_ref[(seq * A_HEADS + h) * MOBA_TOPK + sel]
                for r in range(ppb):
                    page = pt_ref[seq, blk * ppb + r]
                    i = h * n_pages + sel * ppb + r
                    cps.append(pltpu.make_async_copy(k_hbm.at[layer, page, :, h, :], kbuf.at[slot, i], sem.at[0, slot]))
                    cps.append(pltpu.make_async_copy(v_hbm.at[layer, page, :, h, :], vbuf.at[slot, i], sem.at[1, slot]))
        return cps

    @pl.when(b == 0)
    def _():
        for cp in page_copies(0, 0):
            cp.start()

    @pl.when(b + 1 < pl.num_programs(0))
    def _():
        for cp in page_copies(b + 1, (b + 1) % 2):
            cp.start()

    slot = b % 2
    for cp in page_copies(b, slot):
        cp.wait()

    rows = []
    for h in range(A_HEADS):
        q = q_ref[0, h:h + 1, :]
        qh = jnp.broadcast_to(q, (SUBLANES, HEAD_DIM)).astype(BF16)
        s_own = jnp.sum(q * kn_ref[0, h:h + 1, :], axis=-1, keepdims=True) * ATTN_SCALE
        ss = [lax.dot_general(qh, kbuf[slot, h * n_pages + i].astype(BF16), NT_DIMS,
                              preferred_element_type=F32) * ATTN_SCALE for i in range(n_pages)]
        m = s_own
        for sc in ss:
            m = jnp.maximum(m, jnp.max(sc[0:1], axis=-1, keepdims=True))
        p_own = jnp.exp(s_own - m)
        l = p_own
        acc = p_own * vn_ref[0, h:h + 1, :]
        for i, sc in enumerate(ss):
            p = jnp.exp(sc - m)
            l = l + jnp.sum(p[0:1], axis=-1, keepdims=True)
            acc = acc + jnp.dot(p.astype(BF16), vbuf[slot, h * n_pages + i].astype(BF16),
                                preferred_element_type=F32)[0:1]
        rows.append(acc / l)
    o_ref[0] = jnp.concatenate(rows, axis=0)


def _paged_moba(q, k_new, v_new, k_pool, v_pool, page_table, layer):
    Bs = q.shape[0]
    assert q.shape == (Bs, A_HEADS, HEAD_DIM)
    n_pages = page_table.shape[1]
    ppb = MOBA_BLOCK // PAGE_SIZE
    assert n_pages % ppb == 0
    nblk = n_pages // ppb
    assert nblk >= MOBA_TOPK
    bps = math.gcd(nblk, 4)
    q3, k3, v3 = q, k_new, v_new
    page_blk = (1, 1, PAGE_SIZE, A_HEADS, HEAD_DIM)

    def mean_spec(p):
        return pl.BlockSpec(page_blk, lambda b, n, pt, p=p: (layer, pt[b, n * bps * ppb + p], 0, 0, 0))

    idx = pl.pallas_call(
        functools.partial(_paged_topk_kernel, nblk=nblk, bps=bps, ppb=ppb),
        grid_spec=pltpu.PrefetchScalarGridSpec(
            num_scalar_prefetch=1, grid=(Bs, nblk // bps),
            in_specs=[mean_spec(p) for p in range(bps * ppb)]
            + [pl.BlockSpec((1, A_HEADS, HEAD_DIM), lambda b, n, pt: (b, 0, 0))],
            out_specs=pl.BlockSpec((1, A_HEADS, LANES), lambda b, n, pt: (b, 0, 0)),
            scratch_shapes=[pltpu.VMEM((nblk, A_HEADS, HEAD_DIM), F32)]),
        out_shape=jax.ShapeDtypeStruct((Bs, A_HEADS, LANES), jnp.int32),
        compiler_params=_cparams(("parallel", "arbitrary")), name="paged_topk")(
            page_table, *([k_pool] * (bps * ppb)), q3)
    top = idx[:, :, :MOBA_TOPK].reshape(-1)

    n_sel_pages = MOBA_TOPK * ppb
    seq_spec = pl.BlockSpec((1, A_HEADS, HEAD_DIM), lambda b, pt, tp: (b, 0, 0))
    hbm_spec = pl.BlockSpec(memory_space=pl.ANY)
    att = pl.pallas_call(
        functools.partial(_paged_attn_kernel, layer=layer, ppb=ppb),
        grid_spec=pltpu.PrefetchScalarGridSpec(
            num_scalar_prefetch=2, grid=(Bs,),
            in_specs=[seq_spec, seq_spec, seq_spec, hbm_spec, hbm_spec],
            out_specs=seq_spec,
            scratch_shapes=[pltpu.VMEM((2, A_HEADS * n_sel_pages, PAGE_SIZE, HEAD_DIM), F32),
                            pltpu.VMEM((2, A_HEADS * n_sel_pages, PAGE_SIZE, HEAD_DIM), F32),
                            pltpu.SemaphoreType.DMA((2, 2))]),
        out_shape=jax.ShapeDtypeStruct((Bs, A_HEADS, HEAD_DIM), F32),
        compiler_params=_cparams(("arbitrary",)), name="paged_attn")(
            page_table, top, q3, k3, v3, k_pool, v_pool)
    return att


def _rope_tables(pos):
    inv = ROPE_THETA ** (-jnp.arange(ROT_HALF, dtype=F32) / ROT_HALF)
    ang = pos.astype(F32)[:, None] * inv
    cos, sin = jnp.cos(ang), jnp.sin(ang)
    T = pos.shape[0]
    rest = HEAD_DIM - ROT_DIM
    cos_t = jnp.concatenate([cos, cos, jnp.ones((T, rest), F32)], axis=-1)
    sin_t = jnp.concatenate([-sin, sin, jnp.zeros((T, rest), F32)], axis=-1)
    return cos_t, sin_t


ROW_TILE = 1024


def _row_tile(m):
    return ROW_TILE if m % ROW_TILE == 0 else m


def kernel(x_prompt, x_sample, cache_k, cache_v, state_lru_conv, state_lru_h, state_ssd_conv, state_ssd_h, page_table, norm_mix, norm_mlp, w_in_a, q_norm, k_norm, lru_conv_w, lru_conv_b, lru_wa, lru_ba, lru_wx, lru_bx, lru_lambda, w_out_a, w_in_s, ssd_conv_w, ssd_conv_b, ssd_dt_bias, ssd_a_log, ssd_d, ssd_norm, w_out_s, w_up, w_down):
    Bp, Tp, D = x_prompt.shape
    Bs, Ts, _ = x_sample.shape
    assert Ts == 1
    depth = norm_mix.shape[0]
    past_len = page_table.shape[1] * PAGE_SIZE
    Mp, Ms = Bp * Tp, Bs * Ts
    tmp, tms = _row_tile(Mp), _row_tile(Ms)
    tmo = tmp // 2 if tmp == ROW_TILE else tmp
    tmr = tmp // 4 if tmp == ROW_TILE else tmp
    assert Tp % tmr == 0
    assert Tp % tmp == 0
    xp = x_prompt.reshape(Mp, D)
    xs = x_sample.reshape(Ms, D)
    lru_w = lru_lambda.shape[1]
    d_inner = ssd_norm.shape[1]
    xbc_w = ssd_conv_w.shape[2]
    n_ssd_heads = ssd_d.shape[1]

    cos_p, sin_p = _rope_tables(jnp.arange(Tp, dtype=jnp.int32))
    cos_s, sin_s = _rope_tables(jnp.full((Ms,), past_len, dtype=jnp.int32))
    w_up_h, w_down_h = w_up.astype(BF16), w_down.astype(BF16)

    outs = {k: [] for k in ("kp", "vp", "ks", "vs", "lcp", "lcs", "lhp", "lhs", "scp", "scs", "shp", "shs")}
    for layer in range(depth):
        i = layer // 2
        if layer % 2 == 0:
            w_in = w_in_a[i].astype(BF16)
            w_out = w_out_a[i].astype(BF16)
            splits = [A_W, A_W, A_W, lru_w, lru_w]
            lru_args = (lru_conv_w[i], lru_conv_b[i], lru_wa[i], lru_ba[i], lru_wx[i], lru_bx[i], lru_lambda[i])
            heads = lambda a: a.reshape(a.shape[0], A_HEADS, HEAD_DIM)
            q, k, v, xr, yg = _norm_proj_resident(xp, norm_mix[layer], w_in, splits, tm=tmr,
                                                  rope=((q_norm[i], k_norm[i]), cos_p, sin_p, Tp // tmr))
            att = _moba_prompt(q, k, v, batch=Bp, seq=Tp)
            rec, h_last = _lru_prompt(xr, yg, *lru_args, batch=Bp, seq=Tp)
            mixed_p = _out_proj([att, rec], w_out, xp, tm=tmo)
            outs["kp"].append(k.reshape(Bp, Tp, A_HEADS, HEAD_DIM))
            outs["vp"].append(v.reshape(Bp, Tp, A_HEADS, HEAD_DIM))
            outs["lcp"].append(xr.reshape(Bp, Tp, lru_w)[:, Tp - (CONV_K - 1):])
            outs["lhp"].append(h_last.reshape(Bp, lru_w))
            q, k, v, xr, yg = _norm_proj_resident(xs, norm_mix[layer], w_in, splits, tm=tms,
                                                  rope=((q_norm[i], k_norm[i]), cos_s, sin_s, 1))
            att = _paged_moba(heads(q), heads(k), heads(v), cache_k, cache_v, page_table, layer=i)
            att = att.reshape(Bs, A_W).astype(BF16)
            buf = state_lru_conv[i]
            rec, h_new = _lru_sample(xr, yg, buf.swapaxes(0, 1), state_lru_h[i], *lru_args)
            mixed_s = _out_proj([att, rec], w_out, xs, tm=tms)
            outs["ks"].append(k.reshape(Bs, Ts, A_HEADS, HEAD_DIM))
            outs["vs"].append(v.reshape(Bs, Ts, A_HEADS, HEAD_DIM))
            outs["lcs"].append(jnp.concatenate([buf[:, 1:], xr[:, None, :]], axis=1))
            outs["lhs"].append(h_new)
        else:
            main_w = d_inner + xbc_w
            w_z = w_in_s[i][:, :d_inner].astype(BF16)
            w_x = w_in_s[i][:, d_inner:main_w].astype(BF16)
            w_dt = jnp.pad(w_in_s[i][:, main_w:], ((0, 0), (0, LANES - n_ssd_heads))).astype(BF16)
            w_out = w_out_s[i].astype(BF16)
            ssd_args = (ssd_conv_w[i], ssd_conv_b[i], ssd_dt_bias[i], ssd_a_log[i], ssd_d[i], ssd_norm[i])
            unslab = lambda a: a.swapaxes(0, 1).reshape(a.shape[1], -1)
            z, dt = _norm_proj_resident(xp, norm_mix[layer], w_z, [d_inner], tm=tmo, extra_w=w_dt, slab=True)
            xbc, = _norm_proj_resident(xp, norm_mix[layer], w_x, [xbc_w], tm=tmr, slab=True)
            yn, s_last = _ssd_prompt(xbc, z, dt, *ssd_args, batch=Bp, seq=Tp)
            mixed_p = _out_proj([yn], w_out, xp, tm=tmo)
            tail = xbc.reshape(-1, Bp, Tp, LANES)[:, :, Tp - (CONV_K - 1):]
            outs["scp"].append(tail.transpose(1, 2, 0, 3).reshape(Bp, CONV_K - 1, xbc_w))
            outs["shp"].append(s_last.reshape(Bp, n_ssd_heads, SSD_HEAD_DIM, SSD_STATE))
            z, dt = _norm_proj_resident(xs, norm_mix[layer], w_z, [d_inner], tm=tms, extra_w=w_dt, slab=True)
            xbc, = _norm_proj_resident(xs, norm_mix[layer], w_x, [xbc_w], tm=tms, slab=True)
            z, xbc = unslab(z), unslab(xbc)
            buf = state_ssd_conv[i]
            yn, s_new = _ssd_sample(xbc, buf.swapaxes(0, 1), z, dt,
                                    state_ssd_h.reshape((-1,) + state_ssd_h.shape[2:]), *ssd_args, s_off=i * Bs)
            mixed_s = _out_proj([yn], w_out, xs, tm=tms)
            outs["scs"].append(jnp.concatenate([buf[:, 1:], xbc[:, None, :]], axis=1))
            outs["shs"].append(s_new)
        xp = _mlp(mixed_p, norm_mlp[layer], w_up_h, w_down_h, layer, tm=tmp)
        xs = _mlp(mixed_s, norm_mlp[layer], w_up_h, w_down_h, layer, tm=tms)

    st = lambda key: jnp.stack(outs[key])
    return (xp.reshape(Bp, Tp, D), xs.reshape(Bs, Ts, D), st("kp"), st("vp"), st("ks"), st("vs"),
            st("lcp"), st("lcs"), st("lhp"), st("lhs"), st("scp"), st("scs"), st("shp"), st("shs"))
```

```python
import functools
import math

import jax
import jax.numpy as jnp
from jax import lax
from jax.experimental import pallas as pl
from jax.experimental.pallas import tpu as pltpu

F32 = jnp.float32
BF16 = jnp.bfloat16
HIGHEST = lax.Precision.HIGHEST

A_HEADS = 8
HEAD_DIM = 128
A_W = A_HEADS * HEAD_DIM
ROT_DIM = HEAD_DIM // 4
ROT_HALF = ROT_DIM // 2
ROPE_THETA = 500000.0
MOBA_BLOCK = 256
MOBA_TOPK = 3
ATTN_SCALE = HEAD_DIM ** -0.5
LRU_HEADS = 8
LRU_C = 8.0
CONV_K = 4
SSD_HEAD_DIM = 64
SSD_STATE = 128
SSD_GROUPS = 8
SSD_CHUNK = 128
PAGE_SIZE = 128
EPS = 1e-6

LANES = 128
SUBLANES = 8
NEG_BIG = -1e30
LOG2E = math.log2(math.e)
VMEM_BYTES_V7X = 64 * 1024 * 1024
VMEM_LIMIT = VMEM_BYTES_V7X - 8 * 1024 * 1024
VMEM_LIMIT_MLP = VMEM_BYTES_V7X - 4 * 1024 * 1024

NT_DIMS = (((1,), (1,)), ((), ()))


def _cparams(sem, vmem=VMEM_LIMIT):
    return pltpu.CompilerParams(dimension_semantics=sem, vmem_limit_bytes=vmem)


def _sigmoid(x):
    return 0.5 * (jnp.tanh(0.5 * x) + 1.0)


def _silu(x):
    h = 0.5 * x
    return h * (jnp.tanh(h) + 1.0)


def _rms_rows(x, g):
    return x * lax.rsqrt(jnp.mean(x * x, axis=-1, keepdims=True) + EPS) * g


def _head_norm_rope(acc, g, cos, sin):
    y = _rms_rows(acc, g)
    lane = lax.broadcasted_iota(jnp.int32, y.shape, 1)
    partner = jnp.where(lane < ROT_HALF, pltpu.roll(y, LANES - ROT_HALF, 1), pltpu.roll(y, ROT_HALF, 1))
    return y * cos + partner * sin


RESIDENT_DOT_COLS = 1024


def _norm_proj_resident_kernel(*refs, splits, n_rope, has_extra, slab):
    it = iter(refs)
    x_ref, g_ref = next(it), next(it)
    w_refs = [next(it) for _ in range(sum(splits) // RESIDENT_DOT_COLS)]
    rope_g = [next(it) for _ in range(n_rope)]
    if n_rope:
        cos_ref, sin_ref = next(it), next(it)
    if has_extra:
        we_ref = next(it)
    outs = [next(it) for _ in splits]
    xn = _rms_rows(x_ref[...], g_ref[...]).astype(BF16)
    if has_extra:
        next(it)[...] = jnp.dot(xn, we_ref[...], preferred_element_type=F32)
    col = 0
    for s, width in enumerate(splits):
        for c0 in range(0, width, RESIDENT_DOT_COLS):
            acc = jnp.dot(xn, w_refs[(col + c0) // RESIDENT_DOT_COLS][...], preferred_element_type=F32)
            for hh in range(RESIDENT_DOT_COLS // LANES):
                sl = slice(hh * LANES, (hh + 1) * LANES)
                val = acc[:, sl]
                if s < n_rope:
                    val = _head_norm_rope(val, rope_g[s][...], cos_ref[...], sin_ref[...])
                if slab[s]:
                    outs[s][c0 // LANES + hh] = val
                else:
                    outs[s][:, c0 + hh * LANES:c0 + (hh + 1) * LANES] = val
        col += width


def _norm_proj_resident(x, g, w, w_layer, w_col0, splits, *, tm, rope=None, extra_w=None, slab=False):
    M, K = x.shape
    N = sum(splits)
    assert w.ndim == 3 and w.shape[1] == K and w_col0 % RESIDENT_DOT_COLS == 0 and w_col0 + N <= w.shape[2]
    assert M % tm == 0 and all(s % RESIDENT_DOT_COLS == 0 for s in splits)
    assert HEAD_DIM == LANES
    n_rope = 0 if rope is None else len(rope[0])
    in_specs = [pl.BlockSpec((tm, K), lambda i: (i, 0)), pl.BlockSpec((1, K), lambda i: (0, 0))]
    blk0 = w_col0 // RESIDENT_DOT_COLS
    for j in range(N // RESIDENT_DOT_COLS):
        in_specs.append(pl.BlockSpec((None, K, RESIDENT_DOT_COLS), lambda i, j=j: (w_layer, 0, blk0 + j),
                                     pipeline_mode=pl.Buffered(1)))
    args = [x, g.reshape(1, K)] + [w] * (N // RESIDENT_DOT_COLS)
    if rope is not None:
        gains, cos, sin, period = rope
        for gg in gains:
            in_specs.append(pl.BlockSpec((1, HEAD_DIM), lambda i: (0, 0)))
            args.append(gg.reshape(1, HEAD_DIM))
        for tab in (cos, sin):
            in_specs.append(pl.BlockSpec((tm, HEAD_DIM), lambda i, period=period: (i % period, 0)))
            args.append(tab)
    if extra_w is not None:
        in_specs.append(pl.BlockSpec((K, LANES), lambda i: (0, 0)))
        args.append(extra_w)
    slab = tuple(slab) if isinstance(slab, (tuple, list)) else (bool(slab),) * len(splits)
    out_specs, out_shape = [], []
    for s, as_slab in zip(splits, slab):
        if as_slab:
            out_specs.append(pl.BlockSpec((s // LANES, tm, LANES), lambda i: (0, i, 0)))
            out_shape.append(jax.ShapeDtypeStruct((s // LANES, M, LANES), F32))
        else:
            out_specs.append(pl.BlockSpec((tm, s), lambda i: (i, 0)))
            out_shape.append(jax.ShapeDtypeStruct((M, s), F32))
    if extra_w is not None:
        out_specs.append(pl.BlockSpec((tm, LANES), lambda i: (i, 0)))
        out_shape.append(jax.ShapeDtypeStruct((M, LANES), F32))
    return pl.pallas_call(
        functools.partial(_norm_proj_resident_kernel, splits=tuple(splits), n_rope=n_rope,
                          has_extra=extra_w is not None, slab=slab),
        grid=(M // tm,), in_specs=in_specs, out_specs=out_specs, out_shape=out_shape,
        compiler_params=_cparams(("parallel",)), name="norm_proj_resident")(*args)


def _out_proj_kernel(*refs, n_parts, kp):
    xs = refs[:n_parts]
    w_ref, res_ref, o_ref = refs[n_parts:]
    acc = res_ref[...]
    for p, x_ref in enumerate(xs):
        acc = acc + jnp.dot(x_ref[...], w_ref[p * kp:(p + 1) * kp, :], preferred_element_type=F32)
    o_ref[...] = acc


def _out_proj(parts, w, res, *, tm):
    M, N = res.shape
    kp = parts[0].shape[1]
    assert all(p.shape == (M, kp) for p in parts) and w.shape == (kp * len(parts), N) and M % tm == 0
    in_specs = [pl.BlockSpec((tm, kp), lambda i: (i, 0)) for _ in parts]
    in_specs.append(pl.BlockSpec(w.shape, lambda i: (0, 0), pipeline_mode=pl.Buffered(1)))
    in_specs.append(pl.BlockSpec((tm, N), lambda i: (i, 0)))
    return pl.pallas_call(
        functools.partial(_out_proj_kernel, n_parts=len(parts), kp=kp),
        grid=(M // tm,), in_specs=in_specs,
        out_specs=pl.BlockSpec((tm, N), lambda i: (i, 0)),
        out_shape=jax.ShapeDtypeStruct((M, N), F32),
        compiler_params=_cparams(("parallel",)), name="out_proj")(*parts, w, res)


def _mlp_kernel(x_ref, g_ref, wu_ref, wd_ref, o_ref, xn_ref):
    f = pl.program_id(1)

    @pl.when(f == 0)
    def _():
        x = x_ref[...]
        xn_ref[...] = _rms_rows(x, g_ref[...]).astype(BF16)
        o_ref[...] = x

    h = jnp.maximum(jnp.dot(xn_ref[...], wu_ref[...], preferred_element_type=F32), 0.0)
    o_ref[...] += jnp.dot((h * h).astype(BF16), wd_ref[...], preferred_element_type=F32)


def _mlp(x, g, w_up, w_down, layer, *, tm, tf=1024):
    M, D = x.shape
    FF = w_up.shape[2]
    return pl.pallas_call(
        _mlp_kernel, grid=(M // tm, FF // tf),
        in_specs=[pl.BlockSpec((tm, D), lambda i, f: (i, 0)),
                  pl.BlockSpec((1, D), lambda i, f: (0, 0)),
                  pl.BlockSpec((None, D, tf), lambda i, f: (layer, 0, f)),
                  pl.BlockSpec((None, tf, D), lambda i, f: (layer, f, 0))],
        out_specs=pl.BlockSpec((tm, D), lambda i, f: (i, 0)),
        out_shape=jax.ShapeDtypeStruct((M, D), F32),
        scratch_shapes=[pltpu.VMEM((tm, D), BF16)],
        compiler_params=_cparams(("parallel", "arbitrary"), VMEM_LIMIT_MLP), name="mlp")(
            x, g.reshape(1, D), w_up, w_down)


def _moba_prompt_kernel(q_ref, k_ref, v_ref, hot_ref, o_ref, kaug_ref, vb_ref, kmean_ref, *, nb):
    blk, D = MOBA_BLOCK, HEAD_DIM
    kaug_ref[:, 0:D] = k_ref[...].astype(BF16)
    kaug_ref[:, D:2 * D] = hot_ref[...]
    vb_ref[...] = v_ref[...].astype(BF16)
    kmean_ref[...] = jnp.zeros(kmean_ref.shape, F32)
    for n in range(nb):
        kmean_ref[n:n + 1, :] = jnp.mean(k_ref[n * blk:(n + 1) * blk, :], axis=0, keepdims=True)

    lane = lax.broadcasted_iota(jnp.int32, (blk, LANES), 1)
    row = lax.broadcasted_iota(jnp.int32, (blk, blk), 0)
    col = lax.broadcasted_iota(jnp.int32, (blk, blk), 1)
    nbp = kmean_ref.shape[0]
    blk_id = lax.broadcasted_iota(jnp.int32, (nbp, blk), 0)
    eye = (row == col).astype(BF16)
    for qb in range(nb):
        q = q_ref[qb * blk:(qb + 1) * blk, :]
        if qb > MOBA_TOPK:
            gate = lax.dot_general(kmean_ref[...], q, NT_DIMS, precision=HIGHEST, preferred_element_type=F32)
            gate = jnp.where(blk_id < qb, gate, -jnp.inf)
            bias_t = jnp.where(blk_id == qb, 0.0, NEG_BIG)
            for _ in range(MOBA_TOPK):
                m = jnp.max(gate, axis=0, keepdims=True)
                idx = jnp.min(jnp.where(gate == m, blk_id, nbp), axis=0, keepdims=True)
                hit = blk_id == idx
                bias_t = jnp.where(hit & (m > -jnp.inf), 0.0, bias_t)
                gate = jnp.where(hit, -jnp.inf, gate)
            pad = jnp.zeros((LANES - nbp, blk), F32)
            bias = lax.dot_general(eye, jnp.concatenate([bias_t, pad], axis=0).astype(BF16), NT_DIMS,
                                   preferred_element_type=F32).astype(BF16)
        else:
            bias = jnp.where(lane <= qb, 0.0, NEG_BIG).astype(BF16)
        lhs = jnp.concatenate([(q * (ATTN_SCALE * LOG2E)).astype(BF16), bias], axis=1)
        own = slice(qb * blk, (qb + 1) * blk)
        s_own = lax.dot_general(lhs, kaug_ref[own, :], NT_DIMS, preferred_element_type=F32)
        s_own = jnp.where(col <= row, s_own, NEG_BIG)
        m = jnp.max(s_own, axis=-1, keepdims=True)
        if qb > 0:
            s_past = lax.dot_general(lhs, kaug_ref[0:qb * blk, :], NT_DIMS, preferred_element_type=F32)
            m = jnp.maximum(m, jnp.max(s_past, axis=-1, keepdims=True))
        p = jnp.exp2(s_own - m)
        l = jnp.sum(p, axis=-1, keepdims=True)
        acc = jnp.dot(p.astype(BF16), vb_ref[own, :], preferred_element_type=F32)
        if qb > 0:
            p = jnp.exp2(s_past - m)
            l = l + jnp.sum(p, axis=-1, keepdims=True)
            acc = acc + jnp.dot(p.astype(BF16), vb_ref[0:qb * blk, :], preferred_element_type=F32)
        o_ref[own, :] = (acc / l).astype(o_ref.dtype)


def _moba_prompt(q, k, v, *, batch, seq):
    M = q.shape[0]
    nb = seq // MOBA_BLOCK
    assert nb <= LANES and seq % MOBA_BLOCK == 0 and q.shape == (M, A_W)
    hot = (lax.broadcasted_iota(jnp.int32, (seq, LANES), 0) // MOBA_BLOCK
           == lax.broadcasted_iota(jnp.int32, (seq, LANES), 1)).astype(BF16)
    seq_head = pl.BlockSpec((seq, HEAD_DIM), lambda b, h: (b, h))
    return pl.pallas_call(
        functools.partial(_moba_prompt_kernel, nb=nb),
        grid=(batch, A_HEADS),
        in_specs=[seq_head, seq_head, seq_head, pl.BlockSpec((seq, LANES), lambda b, h: (0, 0))],
        out_specs=seq_head,
        out_shape=jax.ShapeDtypeStruct((M, A_W), BF16),
        scratch_shapes=[pltpu.VMEM((seq, 2 * HEAD_DIM), BF16), pltpu.VMEM((seq, HEAD_DIM), BF16),
                        pltpu.VMEM((-(-nb // SUBLANES) * SUBLANES, HEAD_DIM), F32)],
        compiler_params=_cparams(("parallel", "parallel")), name="moba_prompt")(q, k, v, hot)


def _lru_gates(xc, wa_ref, ba, wx_ref, bx, lam):
    bw = xc.shape[1] // LRU_HEADS
    rs, is_ = [], []
    for n in range(LRU_HEADS):
        xb = xc[:, n * bw:(n + 1) * bw].astype(BF16)
        rs.append(jnp.dot(xb, wa_ref[n], preferred_element_type=F32))
        is_.append(jnp.dot(xb, wx_ref[n], preferred_element_type=F32))
    r = _sigmoid(jnp.concatenate(rs, axis=-1) + ba)
    i = _sigmoid(jnp.concatenate(is_, axis=-1) + bx)
    log_a = -LRU_C * r * jax.nn.softplus(-lam)
    a = jnp.exp(log_a)
    t = jnp.tanh(log_a)
    u = jnp.sqrt(-2.0 * t / (1.0 - t)) * (i * xc)
    return a, u


def _lru_prompt_kernel(xr_ref, yg_ref, cw_ref, cb_ref, wa_ref, ba_ref, wx_ref, bx_ref, lam_ref,
                       rec_ref, hl_ref, xx_ref, a_ref, u_ref, h_ref, *, tt):
    ti = pl.program_id(1)
    pad = SUBLANES

    @pl.when(ti == 0)
    def _():
        xx_ref[0:pad, :] = jnp.zeros((pad, xx_ref.shape[1]), F32)
        h_ref[...] = jnp.zeros(h_ref.shape, F32)

    @pl.when(ti > 0)
    def _():
        xx_ref[0:pad, :] = xx_ref[tt:tt + pad, :]

    xx_ref[pad:pad + tt, :] = xr_ref[...]
    xc = cb_ref[...]
    for j in range(CONV_K):
        off = pad - (CONV_K - 1) + j
        xc = xc + xx_ref[off:off + tt, :] * cw_ref[j:j + 1, :]
    a, u = _lru_gates(xc, wa_ref, ba_ref[...], wx_ref, bx_ref[...], lam_ref[...])
    a_ref[...] = a
    u_ref[...] = u

    sub = lax.broadcasted_iota(jnp.int32, (SUBLANES, a.shape[1]), 0)

    def group(gi, h):
        r0 = pl.multiple_of(gi * SUBLANES, SUBLANES)
        a8 = a_ref[pl.ds(r0, SUBLANES), :]
        u8 = u_ref[pl.ds(r0, SUBLANES), :]
        for s in (1, 2, 4):
            keep = sub >= s
            a_sh = pltpu.roll(a8, s, 0)
            u_sh = pltpu.roll(u8, s, 0)
            u8 = jnp.where(keep, a8 * u_sh + u8, u8)
            a8 = jnp.where(keep, a8 * a_sh, a8)
        hs = a8 * h + u8
        yg = yg_ref[pl.ds(r0, SUBLANES), :]
        rec_ref[pl.ds(r0, SUBLANES), :] = (hs * jax.nn.gelu(yg)).astype(rec_ref.dtype)
        return hs[SUBLANES - 1:SUBLANES, :]

    h = lax.fori_loop(0, tt // SUBLANES, group, h_ref[...])
    h_ref[...] = h

    @pl.when(ti == pl.num_programs(1) - 1)
    def _():
        hl_ref[0] = h


def _lru_prompt(xr, yg, cw, cb, wa, ba, wx, bx, lam, *, batch, seq, tt=256):
    M, W = xr.shape
    nt = seq // tt
    row = lambda a: a.reshape(1, W)
    full2 = lambda shp: pl.BlockSpec(shp, lambda b, t: (0, 0))
    full3 = lambda shp: pl.BlockSpec(shp, lambda b, t: (0, 0, 0))
    return pl.pallas_call(
        functools.partial(_lru_prompt_kernel, tt=tt),
        grid=(batch, nt),
        in_specs=[pl.BlockSpec((tt, W), lambda b, t: (b * nt + t, 0)),
                  pl.BlockSpec((tt, W), lambda b, t: (b * nt + t, 0)),
                  full2((CONV_K, W)), full2((1, W)), full3(wa.shape), full2((1, W)),
                  full3(wx.shape), full2((1, W)), full2((1, W))],
        out_specs=[pl.BlockSpec((tt, W), lambda b, t: (b * nt + t, 0)),
                   pl.BlockSpec((1, 1, W), lambda b, t: (b, 0, 0))],
        out_shape=[jax.ShapeDtypeStruct((M, W), BF16), jax.ShapeDtypeStruct((batch, 1, W), F32)],
        scratch_shapes=[pltpu.VMEM((tt + SUBLANES, W), F32), pltpu.VMEM((tt, W), F32),
                        pltpu.VMEM((tt, W), F32), pltpu.VMEM((1, W), F32)],
        compiler_params=_cparams(("parallel", "arbitrary")), name="lru_prompt")(
            xr, yg, cw, row(cb), wa.astype(BF16), row(ba), wx.astype(BF16), row(bx), row(lam))


def _lru_sample_kernel(xr_ref, yg_ref, buf_ref, h0_ref, cw_ref, cb_ref, wa_ref, ba_ref, wx_ref, bx_ref,
                       lam_ref, rec_ref, hn_ref):
    xc = cb_ref[...]
    for j in range(CONV_K - 1):
        xc = xc + buf_ref[j] * cw_ref[j:j + 1, :]
    xc = xc + xr_ref[...] * cw_ref[CONV_K - 1:CONV_K, :]
    a, u = _lru_gates(xc, wa_ref, ba_ref[...], wx_ref, bx_ref[...], lam_ref[...])
    h = a * h0_ref[...] + u
    hn_ref[...] = h
    rec_ref[...] = (h * jax.nn.gelu(yg_ref[...])).astype(rec_ref.dtype)


def _lru_sample(xr, yg, buf_t, h0, cw, cb, wa, ba, wx, bx, lam):
    Bs, W = xr.shape
    row = lambda a: a.reshape(1, W)
    return pl.pallas_call(
        _lru_sample_kernel,
        out_shape=[jax.ShapeDtypeStruct((Bs, W), BF16), jax.ShapeDtypeStruct((Bs, W), F32)],
        compiler_params=pltpu.CompilerParams(vmem_limit_bytes=VMEM_LIMIT), name="lru_sample")(
            xr, yg, buf_t, h0, cw, row(cb), wa.astype(BF16), row(ba), wx.astype(BF16), row(bx), row(lam))


def _ssd_prompt_kernel(xbc_ref, prev_ref, z_ref, dt_ref, cw_ref, cb_ref, dtb_ref, alog_ref, dsk_ref, ng_ref,
                       ex_ref, yn_ref, sl_ref, xx_ref, xcv_ref, y_ref, st_ref, *, d_inner):
    c = pl.program_id(1)
    L = SSD_CHUNK
    pad = SUBLANES
    gw = d_inner // SSD_GROUPS
    spg = gw // LANES
    hps = LANES // SSD_HEAD_DIM
    head_of_lane = lax.broadcasted_iota(jnp.int32, (L, LANES), 1) // SSD_HEAD_DIM
    assert SSD_STATE == LANES

    @pl.when(c == 0)
    def _():
        xx_ref[:, 0:pad, :] = jnp.zeros((xx_ref.shape[0], pad, LANES), F32)
        st_ref[...] = jnp.zeros(st_ref.shape, F32)

    @pl.when(c > 0)
    def _():
        xx_ref[:, 0:pad, :] = prev_ref[...]

    xx_ref[:, pad:pad + L, :] = xbc_ref[...]

    def conv_slab(k, carry):
        xs = xx_ref[k]
        xc = cb_ref[k]
        for j in range(CONV_K):
            back = CONV_K - 1 - j
            tap = xs if back == 0 else pltpu.roll(xs, back, 0)
            xc = xc + tap[pad:] * cw_ref[k, j:j + 1, :]
        xcv_ref[k] = _silu(xc)
        return carry

    lax.fori_loop(0, xx_ref.shape[0], conv_slab, 0)

    def slabs(ref, first, count):
        return jnp.concatenate([ref[first + i] for i in range(count)], axis=1)

    def split2(a):
        hi = a.astype(BF16)
        return hi, (a - hi.astype(F32)).astype(BF16)

    dt = jax.nn.softplus(dt_ref[...] + dtb_ref[...])
    dta = dt * (-jnp.exp(alog_ref[...]))
    tril = lax.broadcasted_iota(jnp.int32, (L, L), 1) <= lax.broadcasted_iota(jnp.int32, (L, L), 0)
    cs = jnp.dot(tril.astype(F32), dta, precision=HIGHEST, preferred_element_type=F32)
    cs_l2 = cs * LOG2E
    adj_t = (cs_l2 - jnp.log2(dt)).T
    last = cs[L - 1:L, :]
    w_hi, w_lo = split2(dt * jnp.exp(last - cs))
    e_hi, e_lo = split2(jnp.exp(cs))
    cdec = jnp.broadcast_to(jnp.exp(last), (SUBLANES, LANES))
    cd_hi, cd_rest = split2(cdec)
    cd_mid, cd_lo = split2(cdec - cd_hi.astype(F32))

    def expand(terms, exg):
        out = jnp.dot(terms[0], exg, preferred_element_type=F32)
        for t in terms[1:]:
            out = out + jnp.dot(t, exg, preferred_element_type=F32)
        return out

    for g in range(SSD_GROUPS):
        lo = g * gw
        exg = ex_ref[:, lo:lo + gw]
        w_x = expand((w_hi, w_lo), exg)
        ecs_x = expand((e_hi, e_lo), exg)
        cd_x = expand((cd_hi, cd_mid, cd_lo), exg)[0:1]
        xg = slabs(xcv_ref, g * spg, spg)
        bg = xcv_ref[d_inner // LANES + g]
        cg = xcv_ref[d_inner // LANES + SSD_GROUPS + g].astype(BF16)
        bg_t = bg.T.astype(BF16)
        cb = jnp.dot(cg, bg_t, preferred_element_type=F32)
        st_g = st_ref[:, lo:lo + gw]
        y_off = jnp.dot(cg, st_g.astype(BF16), preferred_element_type=F32) * ecs_x
        st_ref[:, lo:lo + gw] = st_g * cd_x + jnp.dot(bg_t, (xg * w_x).astype(BF16), preferred_element_type=F32)
        for i in range(spg):
            xs_h = xcv_ref[g * spg + i].astype(BF16)
            lhs, rhs = [], []
            for k in range(hps):
                h = (g * spg + i) * hps + k
                dec = jnp.exp2(jnp.where(tril, cs_l2[:, h:h + 1] - adj_t[h:h + 1, :], NEG_BIG))
                lhs.append((cb * dec).astype(BF16))
                rhs.append(jnp.where(head_of_lane == k, xs_h, jnp.zeros_like(xs_h)))
            yd = jnp.dot(jnp.concatenate(lhs, axis=1), jnp.concatenate(rhs, axis=0), preferred_element_type=F32)
            cols = slice(lo + i * LANES, lo + (i + 1) * LANES)
            y_ref[:, cols] = yd + y_off[:, i * LANES:(i + 1) * LANES]

    for g in range(SSD_GROUPS):
        lo = g * gw
        cols = slice(lo, lo + gw)
        y = y_ref[:, cols] + slabs(xcv_ref, g * spg, spg) * dsk_ref[:, cols]
        zz = slabs(z_ref, g * spg, spg)
        u = y * _silu(zz)
        yn_ref[:, cols] = _rms_rows(u, ng_ref[:, cols]).astype(yn_ref.dtype)

    @pl.when(c == pl.num_programs(1) - 1)
    def _():
        for k in range(d_inner // LANES):
            sl_ref[0, k * LANES:(k + 1) * LANES, :] = st_ref[:, k * LANES:(k + 1) * LANES].T


def _ssd_expand_matrix(n_heads, d_inner):
    h = lax.broadcasted_iota(jnp.int32, (LANES, d_inner), 0)
    col = lax.broadcasted_iota(jnp.int32, (LANES, d_inner), 1)
    return (col // SSD_HEAD_DIM == h).astype(F32) * (h < n_heads).astype(F32)


def _pad_lanes(v):
    return jnp.pad(v, (0, LANES - v.shape[0])).reshape(1, LANES)


def _ssd_prompt(xbc, z, dt, cw, cb, dt_bias, a_log, d_skip, norm_g, *, batch, seq):
    nxs, M, _ = xbc.shape
    nzs = z.shape[0]
    XW, d_inner = nxs * LANES, nzs * LANES
    n_heads = d_inner // SSD_HEAD_DIM
    L = SSD_CHUNK
    nc = seq // L
    rpb = L // SUBLANES
    const2 = lambda shp: pl.BlockSpec(shp, lambda b, c: (0, 0))
    const3 = lambda shp: pl.BlockSpec(shp, lambda b, c: (0, 0, 0))
    return pl.pallas_call(
        functools.partial(_ssd_prompt_kernel, d_inner=d_inner),
        grid=(batch, nc),
        in_specs=[pl.BlockSpec((nxs, L, LANES), lambda b, c: (0, b * nc + c, 0)),
                  pl.BlockSpec((nxs, SUBLANES, LANES), lambda b, c: (0, jnp.maximum((b * nc + c) * rpb - 1, 0), 0)),
                  pl.BlockSpec((nzs, L, LANES), lambda b, c: (0, b * nc + c, 0)),
                  pl.BlockSpec((L, LANES), lambda b, c: (b * nc + c, 0)),
                  const3((nxs, CONV_K, LANES)), const3((nxs, 1, LANES)), const2((1, LANES)), const2((1, LANES)),
                  const2((1, d_inner)), const2((1, d_inner)), const2((LANES, d_inner))],
        out_specs=[pl.BlockSpec((L, d_inner), lambda b, c: (b * nc + c, 0)),
                   pl.BlockSpec((1, d_inner, SSD_STATE), lambda b, c: (b, 0, 0))],
        out_shape=[jax.ShapeDtypeStruct((M, d_inner), BF16),
                   jax.ShapeDtypeStruct((batch, d_inner, SSD_STATE), F32)],
        scratch_shapes=[pltpu.VMEM((nxs, L + SUBLANES, LANES), F32), pltpu.VMEM((nxs, L, LANES), F32),
                        pltpu.VMEM((L, d_inner), F32), pltpu.VMEM((SSD_STATE, d_inner), F32)],
        compiler_params=_cparams(("parallel", "arbitrary")), name="ssd_prompt")(
            xbc, xbc, z, dt, cw.reshape(CONV_K, nxs, LANES).swapaxes(0, 1), cb.reshape(nxs, 1, LANES),
            _pad_lanes(dt_bias), _pad_lanes(a_log),
            jnp.repeat(d_skip, SSD_HEAD_DIM).reshape(1, d_inner), norm_g.reshape(1, d_inner),
            _ssd_expand_matrix(n_heads, d_inner).astype(BF16))


def _ssd_sample_pre_kernel(x_ref, buf_ref, dt_ref, cw_ref, cb_ref, dtb_ref, alog_ref, ex_ref,
                           xcs_ref, dtx_ref, dax_ref):
    xc = cb_ref[...]
    for j in range(CONV_K - 1):
        xc = xc + buf_ref[j] * cw_ref[j:j + 1, :]
    xc = xc + x_ref[...] * cw_ref[CONV_K - 1:CONV_K, :]
    xcs_ref[...] = _silu(xc)
    dt = jax.nn.softplus(dt_ref[...] + dtb_ref[...])
    da = jnp.exp(dt * (-jnp.exp(alog_ref[...])))
    ex = ex_ref[...]
    dtx_ref[...] = jnp.dot(dt, ex, precision=HIGHEST, preferred_element_type=F32)
    dax_ref[...] = jnp.dot(da, ex, precision=HIGHEST, preferred_element_type=F32)


TN_DIMS = (((0,), (0,)), ((), ()))


def _ssd_sample_state_kernel(s0_ref, xcs_ref, dtx_ref, dax_ref, z_ref, dsk_ref, ng_ref, sn_ref, yn_ref, *, d_inner):
    gw = d_inner // SSD_GROUPS
    hpg = gw // SSD_HEAD_DIM
    nst = SSD_STATE
    row_id = lax.broadcasted_iota(jnp.int32, (SUBLANES, gw), 0)
    row_id_n = lax.broadcasted_iota(jnp.int32, (SUBLANES, nst), 0)
    ones3 = jnp.where(row_id_n < 3, 1.0, 0.0).astype(BF16)
    xcs = xcs_ref[0]
    for g in range(SSD_GROUPS):
        cols = slice(g * gw, (g + 1) * gw)
        x = xcs[:, cols]
        b_row = xcs[:, d_inner + g * nst:d_inner + (g + 1) * nst]
        c_row = xcs[:, d_inner + SSD_GROUPS * nst + g * nst:d_inner + SSD_GROUPS * nst + (g + 1) * nst]
        xdt8 = jnp.where(row_id == 0, x * dtx_ref[0][:, cols], 0.0).astype(BF16)
        b8 = jnp.where(row_id_n == 0, b_row, 0.0).astype(BF16)
        outer = lax.dot_general(xdt8, b8, TN_DIMS, preferred_element_type=F32)
        da = dax_ref[0][:, cols]
        da_hi = da.astype(BF16).astype(F32)
        da_mid = (da - da_hi).astype(BF16).astype(F32)
        da_lo = da - da_hi - da_mid
        da8 = jnp.where(row_id == 0, da_hi, jnp.where(row_id == 1, da_mid, jnp.where(row_id == 2, da_lo, 0.0)))
        da_col = lax.dot_general(da8.astype(BF16), ones3, TN_DIMS, preferred_element_type=F32)
        heads = slice(g * hpg, (g + 1) * hpg)
        s_new = s0_ref[0, heads].reshape(gw, nst) * da_col + outer
        sn_ref[0, heads] = s_new.reshape(hpg, SSD_HEAD_DIM, nst)
        c8 = jnp.broadcast_to(c_row, (SUBLANES, nst)).astype(BF16)
        y = lax.dot_general(c8, s_new.astype(BF16), NT_DIMS, preferred_element_type=F32)[0:1]
        u = (y + x * dsk_ref[:, cols]) * _silu(z_ref[0][:, cols])
        yn_ref[0, :, cols] = _rms_rows(u, ng_ref[:, cols]).astype(yn_ref.dtype)


def _ssd_sample(xbc, buf_t, z, dt, s0, cw, cb, dt_bias, a_log, d_skip, norm_g, s_off=0):
    Bs, XW = xbc.shape
    d_inner = z.shape[1]
    n_heads = d_inner // SSD_HEAD_DIM
    gw = d_inner // SSD_GROUPS
    hpg = gw // SSD_HEAD_DIM
    ex = _ssd_expand_matrix(n_heads, d_inner)
    xcs, dtx, dax = pl.pallas_call(
        _ssd_sample_pre_kernel,
        out_shape=[jax.ShapeDtypeStruct((Bs, XW), F32), jax.ShapeDtypeStruct((Bs, d_inner), F32),
                   jax.ShapeDtypeStruct((Bs, d_inner), F32)],
        compiler_params=pltpu.CompilerParams(vmem_limit_bytes=VMEM_LIMIT), name="ssd_sample_pre")(
            xbc, buf_t, dt, cw, cb.reshape(1, XW), _pad_lanes(dt_bias), _pad_lanes(a_log), ex)
    r3 = lambda a: a.reshape(Bs, 1, a.shape[1])
    seq_row = lambda w: pl.BlockSpec((1, 1, w), lambda b: (b, 0, 0))
    state_blk = (1, n_heads, SSD_HEAD_DIM, SSD_STATE)
    s_new, yn = pl.pallas_call(
        functools.partial(_ssd_sample_state_kernel, d_inner=d_inner),
        grid=(Bs,),
        in_specs=[pl.BlockSpec(state_blk, lambda b: (b + s_off, 0, 0, 0)),
                  seq_row(XW), seq_row(d_inner), seq_row(d_inner), seq_row(d_inner),
                  pl.BlockSpec((1, d_inner), lambda b: (0, 0)), pl.BlockSpec((1, d_inner), lambda b: (0, 0))],
        out_specs=[pl.BlockSpec(state_blk, lambda b: (b, 0, 0, 0)), seq_row(d_inner)],
        out_shape=[jax.ShapeDtypeStruct((Bs,) + s0.shape[1:], F32), jax.ShapeDtypeStruct((Bs, 1, d_inner), BF16)],
        compiler_params=_cparams(("parallel",)), name="ssd_sample_state")(
            s0, r3(xcs), r3(dtx), r3(dax), r3(z),
            jnp.repeat(d_skip, SSD_HEAD_DIM).reshape(1, d_inner), norm_g.reshape(1, d_inner))
    return yn.reshape(Bs, d_inner), s_new


def _paged_topk_kernel(pt_ref, *refs, nblk, bps, ppb):
    pages = refs[:bps * ppb]
    q_ref, idx_ref, kmean_ref = refs[bps * ppb:]
    n = pl.program_id(1)
    for j in range(bps):
        tot = jnp.sum(pages[j * ppb][0, 0], axis=0)
        for r in range(1, ppb):
            tot = tot + jnp.sum(pages[j * ppb + r][0, 0], axis=0)
        kmean_ref[n * bps + j] = tot / float(MOBA_BLOCK)

    @pl.when(n == pl.num_programs(1) - 1)
    def _():
        gate = jnp.sum(kmean_ref[...] * q_ref[...], axis=-1, keepdims=True)
        blk_id = lax.broadcasted_iota(jnp.int32, gate.shape, 0)
        lane = lax.broadcasted_iota(jnp.int32, (A_HEADS, LANES), 1)
        out = jnp.zeros((A_HEADS, LANES), jnp.int32)
        for t in range(MOBA_TOPK):
            m = jnp.max(gate, axis=0, keepdims=True)
            idx = jnp.min(jnp.where(gate == m, blk_id, nblk), axis=0, keepdims=True)
            out = jnp.where(lane == t, idx[0], out)
            gate = jnp.where(blk_id == idx, -jnp.inf, gate)
        idx_ref[0] = out


def _paged_attn_kernel(pt_ref, top_ref, q_ref, kn_ref, vn_ref, k_hbm, v_hbm, o_ref, kbuf, vbuf, sem, *, layer, ppb):
    b = pl.program_id(0)
    n_pages = MOBA_TOPK * ppb

    def page_copies(seq, slot):
        cps = []
        for h in range(A_HEADS):
            for sel in range(MOBA_TOPK):
                blk = top_ref[(seq * A_HEADS + h) * MOBA_TOPK + sel]
                for r in range(ppb):
                    page = pt_ref[seq, blk * ppb + r]
                    i = h * n_pages + sel * ppb + r
                    cps.append(pltpu.make_async_copy(k_hbm.at[layer, page, :, h, :], kbuf.at[slot, i], sem.at[0, slot]))
                    cps.append(pltpu.make_async_copy(v_hbm.at[layer, page, :, h, :], vbuf.at[slot, i], sem.at[1, slot]))
        return cps

    @pl.when(b == 0)
    def _():
        for cp in page_copies(0, 0):
            cp.start()

    @pl.when(b + 1 < pl.num_programs(0))
    def _():
        for cp in page_copies(b + 1, (b + 1) % 2):
            cp.start()

    slot = b % 2
    for cp in page_copies(b, slot):
        cp.wait()

    rows = []
    for h in range(A_HEADS):
        q = q_ref[0, h:h + 1, :]
        qh = jnp.broadcast_to(q, (SUBLANES, HEAD_DIM)).astype(BF16)
        s_own = jnp.sum(q * kn_ref[0, h:h + 1, :], axis=-1, keepdims=True) * ATTN_SCALE
        ss = [lax.dot_general(qh, kbuf[slot, h * n_pages + i].astype(BF16), NT_DIMS,
                              preferred_element_type=F32) * ATTN_SCALE for i in range(n_pages)]
        m = s_own
        for sc in ss:
            m = jnp.maximum(m, jnp.max(sc[0:1], axis=-1, keepdims=True))
        p_own = jnp.exp(s_own - m)
        l = p_own
        acc = p_own * vn_ref[0, h:h + 1, :]
        for i, sc in enumerate(ss):
            p = jnp.exp(sc - m)
            l = l + jnp.sum(p[0:1], axis=-1, keepdims=True)
            acc = acc + jnp.dot(p.astype(BF16), vbuf[slot, h * n_pages + i].astype(BF16),
                                preferred_element_type=F32)[0:1]
        rows.append(acc / l)
    o_ref[0] = jnp.concatenate(rows, axis=0)


def _paged_moba(q, k_new, v_new, k_pool, v_pool, page_table, layer):
    Bs = q.shape[0]
    assert q.shape == (Bs, A_HEADS, HEAD_DIM)
    n_pages = page_table.shape[1]
    ppb = MOBA_BLOCK // PAGE_SIZE
    assert n_pages % ppb == 0
    nblk = n_pages // ppb
    assert nblk >= MOBA_TOPK
    bps = math.gcd(nblk, 4)
    q3, k3, v3 = q, k_new, v_new
    page_blk = (1, 1, PAGE_SIZE, A_HEADS, HEAD_DIM)

    def mean_spec(p):
        return pl.BlockSpec(page_blk, lambda b, n, pt, p=p: (layer, pt[b, n * bps * ppb + p], 0, 0, 0))

    idx = pl.pallas_call(
        functools.partial(_paged_topk_kernel, nblk=nblk, bps=bps, ppb=ppb),
        grid_spec=pltpu.PrefetchScalarGridSpec(
            num_scalar_prefetch=1, grid=(Bs, nblk // bps),
            in_specs=[mean_spec(p) for p in range(bps * ppb)]
            + [pl.BlockSpec((1, A_HEADS, HEAD_DIM), lambda b, n, pt: (b, 0, 0))],
            out_specs=pl.BlockSpec((1, A_HEADS, LANES), lambda b, n, pt: (b, 0, 0)),
            scratch_shapes=[pltpu.VMEM((nblk, A_HEADS, HEAD_DIM), F32)]),
        out_shape=jax.ShapeDtypeStruct((Bs, A_HEADS, LANES), jnp.int32),
        compiler_params=_cparams(("parallel", "arbitrary")), name="paged_topk")(
            page_table, *([k_pool] * (bps * ppb)), q3)
    top = idx[:, :, :MOBA_TOPK].reshape(-1)

    n_sel_pages = MOBA_TOPK * ppb
    seq_spec = pl.BlockSpec((1, A_HEADS, HEAD_DIM), lambda b, pt, tp: (b, 0, 0))
    hbm_spec = pl.BlockSpec(memory_space=pl.ANY)
    att = pl.pallas_call(
        functools.partial(_paged_attn_kernel, layer=layer, ppb=ppb),
        grid_spec=pltpu.PrefetchScalarGridSpec(
            num_scalar_prefetch=2, grid=(Bs,),
            in_specs=[seq_spec, seq_spec, seq_spec, hbm_spec, hbm_spec],
            out_specs=seq_spec,
            scratch_shapes=[pltpu.VMEM((2, A_HEADS * n_sel_pages, PAGE_SIZE, HEAD_DIM), F32),
                            pltpu.VMEM((2, A_HEADS * n_sel_pages, PAGE_SIZE, HEAD_DIM), F32),
                            pltpu.SemaphoreType.DMA((2, 2))]),
        out_shape=jax.ShapeDtypeStruct((Bs, A_HEADS, HEAD_DIM), F32),
        compiler_params=_cparams(("arbitrary",)), name="paged_attn")(
            page_table, top, q3, k3, v3, k_pool, v_pool)
    return att


def _rope_tables(pos):
    inv = ROPE_THETA ** (-jnp.arange(ROT_HALF, dtype=F32) / ROT_HALF)
    ang = pos.astype(F32)[:, None] * inv
    cos, sin = jnp.cos(ang), jnp.sin(ang)
    T = pos.shape[0]
    rest = HEAD_DIM - ROT_DIM
    cos_t = jnp.concatenate([cos, cos, jnp.ones((T, rest), F32)], axis=-1)
    sin_t = jnp.concatenate([-sin, sin, jnp.zeros((T, rest), F32)], axis=-1)
    return cos_t, sin_t


ROW_TILE = 1024


def _row_tile(m):
    return ROW_TILE if m % ROW_TILE == 0 else m


def kernel(x_prompt, x_sample, cache_k, cache_v, state_lru_conv, state_lru_h, state_ssd_conv, state_ssd_h, page_table, norm_mix, norm_mlp, w_in_a, q_norm, k_norm, lru_conv_w, lru_conv_b, lru_wa, lru_ba, lru_wx, lru_bx, lru_lambda, w_out_a, w_in_s, ssd_conv_w, ssd_conv_b, ssd_dt_bias, ssd_a_log, ssd_d, ssd_norm, w_out_s, w_up, w_down):
    Bp, Tp, D = x_prompt.shape
    Bs, Ts, _ = x_sample.shape
    assert Ts == 1
    depth = norm_mix.shape[0]
    past_len = page_table.shape[1] * PAGE_SIZE
    Mp, Ms = Bp * Tp, Bs * Ts
    tmp, tms = _row_tile(Mp), _row_tile(Ms)
    tmo = tmp // 2 if tmp == ROW_TILE else tmp
    tmr = tmp // 4 if tmp == ROW_TILE else tmp
    assert Tp % tmr == 0
    assert Tp % tmp == 0
    xp = x_prompt.reshape(Mp, D)
    xs = x_sample.reshape(Ms, D)
    lru_w = lru_lambda.shape[1]
    d_inner = ssd_norm.shape[1]
    xbc_w = ssd_conv_w.shape[2]
    n_ssd_heads = ssd_d.shape[1]

    cos_p, sin_p = _rope_tables(jnp.arange(Tp, dtype=jnp.int32))
    cos_s, sin_s = _rope_tables(jnp.full((Ms,), past_len, dtype=jnp.int32))
    w_up_h, w_down_h = w_up.astype(BF16), w_down.astype(BF16)
    w_in_a_h, w_in_s_h = w_in_a.astype(BF16), w_in_s.astype(BF16)

    outs = {k: [] for k in ("kp", "vp", "ks", "vs", "lcp", "lcs", "lhp", "lhs", "scp", "scs", "shp", "shs")}
    for layer in range(depth):
        i = layer // 2
        if layer % 2 == 0:
            w_in = w_in_a_h
            w_out = w_out_a[i].astype(BF16)
            splits = [A_W, A_W, A_W, lru_w, lru_w]
            lru_args = (lru_conv_w[i], lru_conv_b[i], lru_wa[i], lru_ba[i], lru_wx[i], lru_bx[i], lru_lambda[i])
            heads = lambda a: a.reshape(a.shape[0], A_HEADS, HEAD_DIM)
            q, k, v, xr, yg = _norm_proj_resident(xp, norm_mix[layer], w_in, i, 0, splits, tm=tmr,
                                                  rope=((q_norm[i], k_norm[i]), cos_p, sin_p, Tp // tmr))
            att = _moba_prompt(q, k, v, batch=Bp, seq=Tp)
            rec, h_last = _lru_prompt(xr, yg, *lru_args, batch=Bp, seq=Tp)
            mixed_p = _out_proj([att, rec], w_out, xp, tm=tmo)
            outs["kp"].append(k.reshape(Bp, Tp, A_HEADS, HEAD_DIM))
            outs["vp"].append(v.reshape(Bp, Tp, A_HEADS, HEAD_DIM))
            outs["lcp"].append(xr.reshape(Bp, Tp, lru_w)[:, Tp - (CONV_K - 1):])
            outs["lhp"].append(h_last.reshape(Bp, lru_w))
            q, k, v, xr, yg = _norm_proj_resident(xs, norm_mix[layer], w_in, i, 0, splits, tm=tms,
                                                  rope=((q_norm[i], k_norm[i]), cos_s, sin_s, 1))
            att = _paged_moba(heads(q), heads(k), heads(v), cache_k, cache_v, page_table, layer=i)
            att = att.reshape(Bs, A_W).astype(BF16)
            buf = state_lru_conv[i]
            rec, h_new = _lru_sample(xr, yg, buf.swapaxes(0, 1), state_lru_h[i], *lru_args)
            mixed_s = _out_proj([att, rec], w_out, xs, tm=tms)
            outs["ks"].append(k.reshape(Bs, Ts, A_HEADS, HEAD_DIM))
            outs["vs"].append(v.reshape(Bs, Ts, A_HEADS, HEAD_DIM))
            outs["lcs"].append(jnp.concatenate([buf[:, 1:], xr[:, None, :]], axis=1))
            outs["lhs"].append(h_new)
        else:
            main_w = d_inner + xbc_w
            w_dt = jnp.pad(w_in_s[i][:, main_w:], ((0, 0), (0, LANES - n_ssd_heads))).astype(BF16)
            w_out = w_out_s[i].astype(BF16)
            ssd_args = (ssd_conv_w[i], ssd_conv_b[i], ssd_dt_bias[i], ssd_a_log[i], ssd_d[i], ssd_norm[i])
            unslab = lambda a: a.swapaxes(0, 1).reshape(a.shape[1], -1)
            z, dt = _norm_proj_resident(xp, norm_mix[layer], w_in_s_h, i, 0, [d_inner], tm=tmo, extra_w=w_dt, slab=True)
            xbc, = _norm_proj_resident(xp, norm_mix[layer], w_in_s_h, i, d_inner, [xbc_w], tm=tmr, slab=True)
            yn, s_last = _ssd_prompt(xbc, z, dt, *ssd_args, batch=Bp, seq=Tp)
            mixed_p = _out_proj([yn], w_out, xp, tm=tmo)
            tail = xbc.reshape(-1, Bp, Tp, LANES)[:, :, Tp - (CONV_K - 1):]
            outs["scp"].append(tail.transpose(1, 2, 0, 3).reshape(Bp, CONV_K - 1, xbc_w))
            outs["shp"].append(s_last.reshape(Bp, n_ssd_heads, SSD_HEAD_DIM, SSD_STATE))
            z, dt = _norm_proj_resident(xs, norm_mix[layer], w_in_s_h, i, 0, [d_inner], tm=tms, extra_w=w_dt, slab=True)
            xbc, = _norm_proj_resident(xs, norm_mix[layer], w_in_s_h, i, d_inner, [xbc_w], tm=tms, slab=True)
            z, xbc = unslab(z), unslab(xbc)
            buf = state_ssd_conv[i]
            yn, s_new = _ssd_sample(xbc, buf.swapaxes(0, 1), z, dt,
                                    state_ssd_h.reshape((-1,) + state_ssd_h.shape[2:]), *ssd_args, s_off=i * Bs)
            mixed_s = _out_proj([yn], w_out, xs, tm=tms)
            outs["scs"].append(jnp.concatenate([buf[:, 1:], xbc[:, None, :]], axis=1))
            outs["shs"].append(s_new)
        xp = _mlp(mixed_p, norm_mlp[layer], w_up_h, w_down_h, layer, tm=tmp)
        xs = _mlp(mixed_s, norm_mlp[layer], w_up_h, w_down_h, layer, tm=tms)

    st = lambda key: jnp.stack(outs[key])
    return (xp.reshape(Bp, Tp, D), xs.reshape(Bs, Ts, D), st("kp"), st("vp"), st("ks"), st("vs"),
            st("lcp"), st("lcs"), st("lhp"), st("lhs"), st("scp"), st("scs"), st("shp"), st("shs"))
```

```python
import functools
import math

import jax
import jax.numpy as jnp
from jax import lax
from jax.experimental import pallas as pl
from jax.experimental.pallas import tpu as pltpu

F32 = jnp.float32
BF16 = jnp.bfloat16
HIGHEST = lax.Precision.HIGHEST

A_HEADS = 8
HEAD_DIM = 128
A_W = A_HEADS * HEAD_DIM
ROT_DIM = HEAD_DIM // 4
ROT_HALF = ROT_DIM // 2
ROPE_THETA = 500000.0
MOBA_BLOCK = 256
MOBA_TOPK = 3
ATTN_SCALE = HEAD_DIM ** -0.5
LRU_HEADS = 8
LRU_C = 8.0
CONV_K = 4
SSD_HEAD_DIM = 64
SSD_STATE = 128
SSD_GROUPS = 8
SSD_CHUNK = 128
PAGE_SIZE = 128
EPS = 1e-6

LANES = 128
SUBLANES = 8
NEG_BIG = -1e30
LOG2E = math.log2(math.e)
VMEM_BYTES_V7X = 64 * 1024 * 1024
VMEM_LIMIT = VMEM_BYTES_V7X - 8 * 1024 * 1024
VMEM_LIMIT_MLP = VMEM_BYTES_V7X - 4 * 1024 * 1024

NT_DIMS = (((1,), (1,)), ((), ()))


def _cparams(sem, vmem=VMEM_LIMIT):
    return pltpu.CompilerParams(dimension_semantics=sem, vmem_limit_bytes=vmem)


def _sigmoid(x):
    return 0.5 * (jnp.tanh(0.5 * x) + 1.0)


def _silu(x):
    h = 0.5 * x
    return h * (jnp.tanh(h) + 1.0)


def _rms_rows(x, g):
    return x * lax.rsqrt(jnp.mean(x * x, axis=-1, keepdims=True) + EPS) * g


def _head_norm_rope(acc, g, cos, sin):
    y = _rms_rows(acc, g)
    lane = lax.broadcasted_iota(jnp.int32, y.shape, 1)
    partner = jnp.where(lane < ROT_HALF, pltpu.roll(y, LANES - ROT_HALF, 1), pltpu.roll(y, ROT_HALF, 1))
    return y * cos + partner * sin


RESIDENT_DOT_COLS = 1024


def _norm_proj_resident_kernel(*refs, splits, n_rope, has_extra, slab):
    it = iter(refs)
    x_ref, g_ref = next(it), next(it)
    w_refs = [next(it) for _ in range(sum(splits) // RESIDENT_DOT_COLS)]
    rope_g = [next(it) for _ in range(n_rope)]
    if n_rope:
        cos_ref, sin_ref = next(it), next(it)
    if has_extra:
        we_ref = next(it)
    outs = [next(it) for _ in splits]
    xn = _rms_rows(x_ref[...], g_ref[...]).astype(BF16)
    if has_extra:
        next(it)[...] = jnp.dot(xn, we_ref[...], preferred_element_type=F32)
    col = 0
    for s, width in enumerate(splits):
        for c0 in range(0, width, RESIDENT_DOT_COLS):
            acc = jnp.dot(xn, w_refs[(col + c0) // RESIDENT_DOT_COLS][...], preferred_element_type=F32)
            for hh in range(RESIDENT_DOT_COLS // LANES):
                sl = slice(hh * LANES, (hh + 1) * LANES)
                val = acc[:, sl]
                if s < n_rope:
                    val = _head_norm_rope(val, rope_g[s][...], cos_ref[...], sin_ref[...])
                if slab[s]:
                    outs[s][c0 // LANES + hh] = val
                else:
                    outs[s][:, c0 + hh * LANES:c0 + (hh + 1) * LANES] = val
        col += width


def _norm_proj_resident(x, g, w, w_layer, w_col0, splits, *, tm, rope=None, extra_w=None, slab=False):
    M, K = x.shape
    N = sum(splits)
    assert w.ndim == 3 and w.shape[1] == K and w_col0 % RESIDENT_DOT_COLS == 0 and w_col0 + N <= w.shape[2]
    assert M % tm == 0 and all(s % RESIDENT_DOT_COLS == 0 for s in splits)
    assert HEAD_DIM == LANES
    n_rope = 0 if rope is None else len(rope[0])
    in_specs = [pl.BlockSpec((tm, K), lambda i: (i, 0)), pl.BlockSpec((1, K), lambda i: (0, 0))]
    blk0 = w_col0 // RESIDENT_DOT_COLS
    for j in range(N // RESIDENT_DOT_COLS):
        in_specs.append(pl.BlockSpec((None, K, RESIDENT_DOT_COLS), lambda i, j=j: (w_layer, 0, blk0 + j),
                                     pipeline_mode=pl.Buffered(1)))
    args = [x, g.reshape(1, K)] + [w] * (N // RESIDENT_DOT_COLS)
    if rope is not None:
        gains, cos, sin, period = rope
        for gg in gains:
            in_specs.append(pl.BlockSpec((1, HEAD_DIM), lambda i: (0, 0)))
            args.append(gg.reshape(1, HEAD_DIM))
        for tab in (cos, sin):
            in_specs.append(pl.BlockSpec((tm, HEAD_DIM), lambda i, period=period: (i % period, 0)))
            args.append(tab)
    if extra_w is not None:
        in_specs.append(pl.BlockSpec((K, LANES), lambda i: (0, 0)))
        args.append(extra_w)
    slab = tuple(slab) if isinstance(slab, (tuple, list)) else (bool(slab),) * len(splits)
    out_specs, out_shape = [], []
    for s, as_slab in zip(splits, slab):
        if as_slab:
            out_specs.append(pl.BlockSpec((s // LANES, tm, LANES), lambda i: (0, i, 0)))
            out_shape.append(jax.ShapeDtypeStruct((s // LANES, M, LANES), F32))
        else:
            out_specs.append(pl.BlockSpec((tm, s), lambda i: (i, 0)))
            out_shape.append(jax.ShapeDtypeStruct((M, s), F32))
    if extra_w is not None:
        out_specs.append(pl.BlockSpec((tm, LANES), lambda i: (i, 0)))
        out_shape.append(jax.ShapeDtypeStruct((M, LANES), F32))
    return pl.pallas_call(
        functools.partial(_norm_proj_resident_kernel, splits=tuple(splits), n_rope=n_rope,
                          has_extra=extra_w is not None, slab=slab),
        grid=(M // tm,), in_specs=in_specs, out_specs=out_specs, out_shape=out_shape,
        compiler_params=_cparams(("parallel",)), name="norm_proj_resident")(*args)


def _out_proj_kernel(*refs, n_parts, kp):
    xs = refs[:n_parts]
    w_ref, res_ref, o_ref = refs[n_parts:]
    acc = res_ref[...]
    for p, x_ref in enumerate(xs):
        acc = acc + jnp.dot(x_ref[...], w_ref[p * kp:(p + 1) * kp, :], preferred_element_type=F32)
    o_ref[...] = acc


def _out_proj(parts, w, res, *, tm):
    M, N = res.shape
    kp = parts[0].shape[1]
    assert all(p.shape == (M, kp) for p in parts) and w.shape == (kp * len(parts), N) and M % tm == 0
    in_specs = [pl.BlockSpec((tm, kp), lambda i: (i, 0)) for _ in parts]
    in_specs.append(pl.BlockSpec(w.shape, lambda i: (0, 0), pipeline_mode=pl.Buffered(1)))
    in_specs.append(pl.BlockSpec((tm, N), lambda i: (i, 0)))
    return pl.pallas_call(
        functools.partial(_out_proj_kernel, n_parts=len(parts), kp=kp),
        grid=(M // tm,), in_specs=in_specs,
        out_specs=pl.BlockSpec((tm, N), lambda i: (i, 0)),
        out_shape=jax.ShapeDtypeStruct((M, N), F32),
        compiler_params=_cparams(("parallel",)), name="out_proj")(*parts, w, res)


def _mlp_kernel(x_ref, g_ref, wu_ref, wd_ref, o_ref, xn_ref):
    f = pl.program_id(1)

    @pl.when(f == 0)
    def _():
        x = x_ref[...]
        xn_ref[...] = _rms_rows(x, g_ref[...]).astype(BF16)
        o_ref[...] = x

    h = jnp.maximum(jnp.dot(xn_ref[...], wu_ref[...], preferred_element_type=F32), 0.0)
    o_ref[...] += jnp.dot((h * h).astype(BF16), wd_ref[...], preferred_element_type=F32)


def _mlp(x, g, w_up, w_down, layer, *, tm, tf=1024):
    M, D = x.shape
    FF = w_up.shape[2]
    return pl.pallas_call(
        _mlp_kernel, grid=(M // tm, FF // tf),
        in_specs=[pl.BlockSpec((tm, D), lambda i, f: (i, 0)),
                  pl.BlockSpec((1, D), lambda i, f: (0, 0)),
                  pl.BlockSpec((None, D, tf), lambda i, f: (layer, 0, f)),
                  pl.BlockSpec((None, tf, D), lambda i, f: (layer, f, 0))],
        out_specs=pl.BlockSpec((tm, D), lambda i, f: (i, 0)),
        out_shape=jax.ShapeDtypeStruct((M, D), F32),
        scratch_shapes=[pltpu.VMEM((tm, D), BF16)],
        compiler_params=_cparams(("parallel", "arbitrary"), VMEM_LIMIT_MLP), name="mlp")(
            x, g.reshape(1, D), w_up, w_down)


def _moba_prompt_kernel(q_ref, k_ref, v_ref, hot_ref, o_ref, kaug_ref, vb_ref, kmean_ref, *, nb):
    blk, D = MOBA_BLOCK, HEAD_DIM
    kaug_ref[:, 0:D] = k_ref[...].astype(BF16)
    kaug_ref[:, D:2 * D] = hot_ref[...]
    vb_ref[...] = v_ref[...].astype(BF16)
    kmean_ref[...] = jnp.zeros(kmean_ref.shape, F32)
    for n in range(nb):
        kmean_ref[n:n + 1, :] = jnp.mean(k_ref[n * blk:(n + 1) * blk, :], axis=0, keepdims=True)

    lane = lax.broadcasted_iota(jnp.int32, (blk, LANES), 1)
    row = lax.broadcasted_iota(jnp.int32, (blk, blk), 0)
    col = lax.broadcasted_iota(jnp.int32, (blk, blk), 1)
    nbp = kmean_ref.shape[0]
    blk_id = lax.broadcasted_iota(jnp.int32, (nbp, blk), 0)
    eye = (row == col).astype(BF16)
    for qb in range(nb):
        q = q_ref[qb * blk:(qb + 1) * blk, :]
        if qb > MOBA_TOPK:
            gate = lax.dot_general(kmean_ref[...], q, NT_DIMS, precision=HIGHEST, preferred_element_type=F32)
            gate = jnp.where(blk_id < qb, gate, -jnp.inf)
            bias_t = jnp.where(blk_id == qb, 0.0, NEG_BIG)
            for _ in range(MOBA_TOPK):
                m = jnp.max(gate, axis=0, keepdims=True)
                idx = jnp.min(jnp.where(gate == m, blk_id, nbp), axis=0, keepdims=True)
                hit = blk_id == idx
                bias_t = jnp.where(hit & (m > -jnp.inf), 0.0, bias_t)
                gate = jnp.where(hit, -jnp.inf, gate)
            pad = jnp.zeros((LANES - nbp, blk), F32)
            bias = lax.dot_general(eye, jnp.concatenate([bias_t, pad], axis=0).astype(BF16), NT_DIMS,
                                   preferred_element_type=F32).astype(BF16)
        else:
            bias = jnp.where(lane <= qb, 0.0, NEG_BIG).astype(BF16)
        lhs = jnp.concatenate([(q * (ATTN_SCALE * LOG2E)).astype(BF16), bias], axis=1)
        own = slice(qb * blk, (qb + 1) * blk)
        s_own = lax.dot_general(lhs, kaug_ref[own, :], NT_DIMS, preferred_element_type=F32)
        s_own = jnp.where(col <= row, s_own, NEG_BIG)
        m = jnp.max(s_own, axis=-1, keepdims=True)
        if qb > 0:
            s_past = lax.dot_general(lhs, kaug_ref[0:qb * blk, :], NT_DIMS, preferred_element_type=F32)
            m = jnp.maximum(m, jnp.max(s_past, axis=-1, keepdims=True))
        p = jnp.exp2(s_own - m)
        l = jnp.sum(p, axis=-1, keepdims=True)
        acc = jnp.dot(p.astype(BF16), vb_ref[own, :], preferred_element_type=F32)
        if qb > 0:
            p = jnp.exp2(s_past - m)
            l = l + jnp.sum(p, axis=-1, keepdims=True)
            acc = acc + jnp.dot(p.astype(BF16), vb_ref[0:qb * blk, :], preferred_element_type=F32)
        o_ref[own, :] = (acc / l).astype(o_ref.dtype)


def _moba_prompt(q, k, v, *, batch, seq):
    M = q.shape[0]
    nb = seq // MOBA_BLOCK
    assert nb <= LANES and seq % MOBA_BLOCK == 0 and q.shape == (M, A_W)
    hot = (lax.broadcasted_iota(jnp.int32, (seq, LANES), 0) // MOBA_BLOCK
           == lax.broadcasted_iota(jnp.int32, (seq, LANES), 1)).astype(BF16)
    seq_head = pl.BlockSpec((seq, HEAD_DIM), lambda b, h: (b, h))
    return pl.pallas_call(
        functools.partial(_moba_prompt_kernel, nb=nb),
        grid=(batch, A_HEADS),
        in_specs=[seq_head, seq_head, seq_head, pl.BlockSpec((seq, LANES), lambda b, h: (0, 0))],
        out_specs=seq_head,
        out_shape=jax.ShapeDtypeStruct((M, A_W), BF16),
        scratch_shapes=[pltpu.VMEM((seq, 2 * HEAD_DIM), BF16), pltpu.VMEM((seq, HEAD_DIM), BF16),
                        pltpu.VMEM((-(-nb // SUBLANES) * SUBLANES, HEAD_DIM), F32)],
        compiler_params=_cparams(("parallel", "parallel")), name="moba_prompt")(q, k, v, hot)


def _lru_gates(xc, wa_ref, ba, wx_ref, bx, lam):
    bw = xc.shape[1] // LRU_HEADS
    rs, is_ = [], []
    for n in range(LRU_HEADS):
        xb = xc[:, n * bw:(n + 1) * bw].astype(BF16)
        rs.append(jnp.dot(xb, wa_ref[n], preferred_element_type=F32))
        is_.append(jnp.dot(xb, wx_ref[n], preferred_element_type=F32))
    r = _sigmoid(jnp.concatenate(rs, axis=-1) + ba)
    i = _sigmoid(jnp.concatenate(is_, axis=-1) + bx)
    log_a = -LRU_C * r * jax.nn.softplus(-lam)
    a = jnp.exp(log_a)
    t = jnp.tanh(log_a)
    u = jnp.sqrt(-2.0 * t / (1.0 - t)) * (i * xc)
    return a, u


def _lru_prompt_kernel(xr_ref, yg_ref, cw_ref, cb_ref, wa_ref, ba_ref, wx_ref, bx_ref, lam_ref,
                       rec_ref, hl_ref, xx_ref, a_ref, u_ref, h_ref, *, tt):
    ti = pl.program_id(1)
    pad = SUBLANES

    @pl.when(ti == 0)
    def _():
        xx_ref[0:pad, :] = jnp.zeros((pad, xx_ref.shape[1]), F32)
        h_ref[...] = jnp.zeros(h_ref.shape, F32)

    @pl.when(ti > 0)
    def _():
        xx_ref[0:pad, :] = xx_ref[tt:tt + pad, :]

    xx_ref[pad:pad + tt, :] = xr_ref[...]
    xc = cb_ref[...]
    for j in range(CONV_K):
        off = pad - (CONV_K - 1) + j
        xc = xc + xx_ref[off:off + tt, :] * cw_ref[j:j + 1, :]
    a, u = _lru_gates(xc, wa_ref, ba_ref[...], wx_ref, bx_ref[...], lam_ref[...])
    a_ref[...] = a
    u_ref[...] = u

    sub = lax.broadcasted_iota(jnp.int32, (SUBLANES, a.shape[1]), 0)

    def group(gi, h):
        r0 = pl.multiple_of(gi * SUBLANES, SUBLANES)
        a8 = a_ref[pl.ds(r0, SUBLANES), :]
        u8 = u_ref[pl.ds(r0, SUBLANES), :]
        for s in (1, 2, 4):
            keep = sub >= s
            a_sh = pltpu.roll(a8, s, 0)
            u_sh = pltpu.roll(u8, s, 0)
            u8 = jnp.where(keep, a8 * u_sh + u8, u8)
            a8 = jnp.where(keep, a8 * a_sh, a8)
        hs = a8 * h + u8
        yg = yg_ref[pl.ds(r0, SUBLANES), :]
        rec_ref[pl.ds(r0, SUBLANES), :] = (hs * jax.nn.gelu(yg)).astype(rec_ref.dtype)
        return hs[SUBLANES - 1:SUBLANES, :]

    h = lax.fori_loop(0, tt // SUBLANES, group, h_ref[...])
    h_ref[...] = h

    @pl.when(ti == pl.num_programs(1) - 1)
    def _():
        hl_ref[0] = h


def _lru_prompt(xr, yg, cw, cb, wa, ba, wx, bx, lam, *, batch, seq, tt=256):
    M, W = xr.shape
    nt = seq // tt
    row = lambda a: a.reshape(1, W)
    full2 = lambda shp: pl.BlockSpec(shp, lambda b, t: (0, 0))
    full3 = lambda shp: pl.BlockSpec(shp, lambda b, t: (0, 0, 0))
    return pl.pallas_call(
        functools.partial(_lru_prompt_kernel, tt=tt),
        grid=(batch, nt),
        in_specs=[pl.BlockSpec((tt, W), lambda b, t: (b * nt + t, 0)),
                  pl.BlockSpec((tt, W), lambda b, t: (b * nt + t, 0)),
                  full2((CONV_K, W)), full2((1, W)), full3(wa.shape), full2((1, W)),
                  full3(wx.shape), full2((1, W)), full2((1, W))],
        out_specs=[pl.BlockSpec((tt, W), lambda b, t: (b * nt + t, 0)),
                   pl.BlockSpec((1, 1, W), lambda b, t: (b, 0, 0))],
        out_shape=[jax.ShapeDtypeStruct((M, W), BF16), jax.ShapeDtypeStruct((batch, 1, W), F32)],
        scratch_shapes=[pltpu.VMEM((tt + SUBLANES, W), F32), pltpu.VMEM((tt, W), F32),
                        pltpu.VMEM((tt, W), F32), pltpu.VMEM((1, W), F32)],
        compiler_params=_cparams(("parallel", "arbitrary")), name="lru_prompt")(
            xr, yg, cw, row(cb), wa.astype(BF16), row(ba), wx.astype(BF16), row(bx), row(lam))


def _lru_sample_kernel(xr_ref, yg_ref, buf_ref, h0_ref, cw_ref, cb_ref, wa_ref, ba_ref, wx_ref, bx_ref,
                       lam_ref, rec_ref, hn_ref):
    xc = cb_ref[...]
    for j in range(CONV_K - 1):
        xc = xc + buf_ref[j] * cw_ref[j:j + 1, :]
    xc = xc + xr_ref[...] * cw_ref[CONV_K - 1:CONV_K, :]
    a, u = _lru_gates(xc, wa_ref, ba_ref[...], wx_ref, bx_ref[...], lam_ref[...])
    h = a * h0_ref[...] + u
    hn_ref[...] = h
    rec_ref[...] = (h * jax.nn.gelu(yg_ref[...])).astype(rec_ref.dtype)


def _lru_sample(xr, yg, buf_t, h0, cw, cb, wa, ba, wx, bx, lam):
    Bs, W = xr.shape
    row = lambda a: a.reshape(1, W)
    return pl.pallas_call(
        _lru_sample_kernel,
        out_shape=[jax.ShapeDtypeStruct((Bs, W), BF16), jax.ShapeDtypeStruct((Bs, W), F32)],
        compiler_params=pltpu.CompilerParams(vmem_limit_bytes=VMEM_LIMIT), name="lru_sample")(
            xr, yg, buf_t, h0, cw, row(cb), wa.astype(BF16), row(ba), wx.astype(BF16), row(bx), row(lam))


def _ssd_prompt_kernel(xbc_ref, prev_ref, z_ref, dt_ref, cw_ref, cb_ref, dtb_ref, alog_ref, dsk_ref, ng_ref,
                       ex_ref, yn_ref, sl_ref, xx_ref, xcv_ref, y_ref, st_ref, *, d_inner):
    c = pl.program_id(1)
    L = SSD_CHUNK
    pad = SUBLANES
    gw = d_inner // SSD_GROUPS
    spg = gw // LANES
    hps = LANES // SSD_HEAD_DIM
    head_of_lane = lax.broadcasted_iota(jnp.int32, (L, LANES), 1) // SSD_HEAD_DIM
    assert SSD_STATE == LANES

    @pl.when(c == 0)
    def _():
        xx_ref[:, 0:pad, :] = jnp.zeros((xx_ref.shape[0], pad, LANES), F32)
        st_ref[...] = jnp.zeros(st_ref.shape, F32)

    @pl.when(c > 0)
    def _():
        xx_ref[:, 0:pad, :] = prev_ref[...]

    xx_ref[:, pad:pad + L, :] = xbc_ref[...]

    def conv_slab(k, carry):
        xs = xx_ref[k]
        xc = cb_ref[k]
        for j in range(CONV_K):
            back = CONV_K - 1 - j
            tap = xs if back == 0 else pltpu.roll(xs, back, 0)
            xc = xc + tap[pad:] * cw_ref[k, j:j + 1, :]
        xcv_ref[k] = _silu(xc)
        return carry

    lax.fori_loop(0, xx_ref.shape[0], conv_slab, 0)

    def slabs(ref, first, count):
        return jnp.concatenate([ref[first + i] for i in range(count)], axis=1)

    def split2(a):
        hi = a.astype(BF16)
        return hi, (a - hi.astype(F32)).astype(BF16)

    dt = jax.nn.softplus(dt_ref[...] + dtb_ref[...])
    dta = dt * (-jnp.exp(alog_ref[...]))
    tril = lax.broadcasted_iota(jnp.int32, (L, L), 1) <= lax.broadcasted_iota(jnp.int32, (L, L), 0)
    cs = jnp.dot(tril.astype(F32), dta, precision=HIGHEST, preferred_element_type=F32)
    cs_l2 = cs * LOG2E
    adj_t = (cs_l2 - jnp.log2(dt)).T
    last = cs[L - 1:L, :]
    pk = 2 * SUBLANES
    cdec = jnp.broadcast_to(jnp.exp(last), (pk, LANES))
    cd_hi = cdec.astype(BF16)
    cd_mid, cd_lo = split2(cdec - cd_hi.astype(F32))
    spread_in = jnp.concatenate([jnp.concatenate(split2(dt * jnp.exp(last - cs)), axis=1),
                                 jnp.concatenate(split2(jnp.exp(cs)), axis=1),
                                 jnp.concatenate([cd_hi, cd_mid], axis=1),
                                 jnp.concatenate([cd_lo, jnp.zeros_like(cd_lo)], axis=1)], axis=0)

    for g in range(SSD_GROUPS):
        lo = g * gw
        spread = jnp.dot(spread_in, ex_ref[:, lo:lo + gw], preferred_element_type=F32)
        w_x, ecs_x = spread[0:L], spread[L:2 * L]
        cd_x = spread[2 * L:2 * L + 1] + spread[2 * L + pk:2 * L + pk + 1]
        xg = slabs(xcv_ref, g * spg, spg)
        bg = xcv_ref[d_inner // LANES + g]
        cg = xcv_ref[d_inner // LANES + SSD_GROUPS + g].astype(BF16)
        bg_t = bg.T.astype(BF16)
        cb = jnp.dot(cg, bg_t, preferred_element_type=F32)
        st_g = st_ref[:, lo:lo + gw]
        y_off = jnp.dot(cg, st_g.astype(BF16), preferred_element_type=F32) * ecs_x
        st_ref[:, lo:lo + gw] = st_g * cd_x + jnp.dot(bg_t, (xg * w_x).astype(BF16), preferred_element_type=F32)
        for i in range(spg):
            xs_h = xcv_ref[g * spg + i].astype(BF16)
            lhs, rhs = [], []
            for k in range(hps):
                h = (g * spg + i) * hps + k
                dec = jnp.exp2(jnp.where(tril, cs_l2[:, h:h + 1] - adj_t[h:h + 1, :], NEG_BIG))
                lhs.append((cb * dec).astype(BF16))
                rhs.append(jnp.where(head_of_lane == k, xs_h, jnp.zeros_like(xs_h)))
            yd = jnp.dot(jnp.concatenate(lhs, axis=1), jnp.concatenate(rhs, axis=0), preferred_element_type=F32)
            cols = slice(lo + i * LANES, lo + (i + 1) * LANES)
            y_ref[:, cols] = yd + y_off[:, i * LANES:(i + 1) * LANES]

    for g in range(SSD_GROUPS):
        lo = g * gw
        cols = slice(lo, lo + gw)
        y = y_ref[:, cols] + slabs(xcv_ref, g * spg, spg) * dsk_ref[:, cols]
        zz = slabs(z_ref, g * spg, spg)
        u = y * _silu(zz)
        yn_ref[:, cols] = _rms_rows(u, ng_ref[:, cols]).astype(yn_ref.dtype)

    @pl.when(c == pl.num_programs(1) - 1)
    def _():
        for k in range(d_inner // LANES):
            sl_ref[0, k * LANES:(k + 1) * LANES, :] = st_ref[:, k * LANES:(k + 1) * LANES].T


def _ssd_expand_matrix(n_heads, d_inner):
    h = lax.broadcasted_iota(jnp.int32, (LANES, d_inner), 0)
    col = lax.broadcasted_iota(jnp.int32, (LANES, d_inner), 1)
    return (col // SSD_HEAD_DIM == h).astype(F32) * (h < n_heads).astype(F32)


def _pad_lanes(v):
    return jnp.pad(v, (0, LANES - v.shape[0])).reshape(1, LANES)


def _ssd_prompt(xbc, z, dt, cw, cb, dt_bias, a_log, d_skip, norm_g, *, batch, seq):
    nxs, M, _ = xbc.shape
    nzs = z.shape[0]
    XW, d_inner = nxs * LANES, nzs * LANES
    n_heads = d_inner // SSD_HEAD_DIM
    L = SSD_CHUNK
    nc = seq // L
    rpb = L // SUBLANES
    const2 = lambda shp: pl.BlockSpec(shp, lambda b, c: (0, 0))
    const3 = lambda shp: pl.BlockSpec(shp, lambda b, c: (0, 0, 0))
    return pl.pallas_call(
        functools.partial(_ssd_prompt_kernel, d_inner=d_inner),
        grid=(batch, nc),
        in_specs=[pl.BlockSpec((nxs, L, LANES), lambda b, c: (0, b * nc + c, 0)),
                  pl.BlockSpec((nxs, SUBLANES, LANES), lambda b, c: (0, jnp.maximum((b * nc + c) * rpb - 1, 0), 0)),
                  pl.BlockSpec((nzs, L, LANES), lambda b, c: (0, b * nc + c, 0)),
                  pl.BlockSpec((L, LANES), lambda b, c: (b * nc + c, 0)),
                  const3((nxs, CONV_K, LANES)), const3((nxs, 1, LANES)), const2((1, LANES)), const2((1, LANES)),
                  const2((1, d_inner)), const2((1, d_inner)), const2((2 * LANES, d_inner))],
        out_specs=[pl.BlockSpec((L, d_inner), lambda b, c: (b * nc + c, 0)),
                   pl.BlockSpec((1, d_inner, SSD_STATE), lambda b, c: (b, 0, 0))],
        out_shape=[jax.ShapeDtypeStruct((M, d_inner), BF16),
                   jax.ShapeDtypeStruct((batch, d_inner, SSD_STATE), F32)],
        scratch_shapes=[pltpu.VMEM((nxs, L + SUBLANES, LANES), F32), pltpu.VMEM((nxs, L, LANES), F32),
                        pltpu.VMEM((L, d_inner), F32), pltpu.VMEM((SSD_STATE, d_inner), F32)],
        compiler_params=_cparams(("parallel", "arbitrary")), name="ssd_prompt")(
            xbc, xbc, z, dt, cw.reshape(CONV_K, nxs, LANES).swapaxes(0, 1), cb.reshape(nxs, 1, LANES),
            _pad_lanes(dt_bias), _pad_lanes(a_log),
            jnp.repeat(d_skip, SSD_HEAD_DIM).reshape(1, d_inner), norm_g.reshape(1, d_inner),
            jnp.tile(_ssd_expand_matrix(n_heads, d_inner).astype(BF16), (2, 1)))


def _ssd_sample_pre_kernel(x_ref, buf_ref, dt_ref, cw_ref, cb_ref, dtb_ref, alog_ref, ex_ref,
                           xcs_ref, dtx_ref, dax_ref):
    xc = cb_ref[...]
    for j in range(CONV_K - 1):
        xc = xc + buf_ref[j] * cw_ref[j:j + 1, :]
    xc = xc + x_ref[...] * cw_ref[CONV_K - 1:CONV_K, :]
    xcs_ref[...] = _silu(xc)
    dt = jax.nn.softplus(dt_ref[...] + dtb_ref[...])
    da = jnp.exp(dt * (-jnp.exp(alog_ref[...])))
    ex = ex_ref[...]
    dtx_ref[...] = jnp.dot(dt, ex, precision=HIGHEST, preferred_element_type=F32)
    dax_ref[...] = jnp.dot(da, ex, precision=HIGHEST, preferred_element_type=F32)


TN_DIMS = (((0,), (0,)), ((), ()))


def _ssd_sample_state_kernel(s0_ref, xcs_ref, dtx_ref, dax_ref, z_ref, dsk_ref, ng_ref, sn_ref, yn_ref, *, d_inner):
    gw = d_inner // SSD_GROUPS
    hpg = gw // SSD_HEAD_DIM
    nst = SSD_STATE
    row_id = lax.broadcasted_iota(jnp.int32, (SUBLANES, gw), 0)
    row_id_n = lax.broadcasted_iota(jnp.int32, (SUBLANES, nst), 0)
    ones3 = jnp.where(row_id_n < 3, 1.0, 0.0).astype(BF16)
    xcs = xcs_ref[0]
    for g in range(SSD_GROUPS):
        cols = slice(g * gw, (g + 1) * gw)
        x = xcs[:, cols]
        b_row = xcs[:, d_inner + g * nst:d_inner + (g + 1) * nst]
        c_row = xcs[:, d_inner + SSD_GROUPS * nst + g * nst:d_inner + SSD_GROUPS * nst + (g + 1) * nst]
        xdt8 = jnp.where(row_id == 0, x * dtx_ref[0][:, cols], 0.0).astype(BF16)
        b8 = jnp.where(row_id_n == 0, b_row, 0.0).astype(BF16)
        outer = lax.dot_general(xdt8, b8, TN_DIMS, preferred_element_type=F32)
        da = dax_ref[0][:, cols]
        da_hi = da.astype(BF16).astype(F32)
        da_mid = (da - da_hi).astype(BF16).astype(F32)
        da_lo = da - da_hi - da_mid
        da8 = jnp.where(row_id == 0, da_hi, jnp.where(row_id == 1, da_mid, jnp.where(row_id == 2, da_lo, 0.0)))
        da_col = lax.dot_general(da8.astype(BF16), ones3, TN_DIMS, preferred_element_type=F32)
        heads = slice(g * hpg, (g + 1) * hpg)
        s_new = s0_ref[0, heads].reshape(gw, nst) * da_col + outer
        sn_ref[0, heads] = s_new.reshape(hpg, SSD_HEAD_DIM, nst)
        c8 = jnp.broadcast_to(c_row, (SUBLANES, nst)).astype(BF16)
        y = lax.dot_general(c8, s_new.astype(BF16), NT_DIMS, preferred_element_type=F32)[0:1]
        u = (y + x * dsk_ref[:, cols]) * _silu(z_ref[0][:, cols])
        yn_ref[0, :, cols] = _rms_rows(u, ng_ref[:, cols]).astype(yn_ref.dtype)


def _ssd_sample(xbc, buf_t, z, dt, s0, cw, cb, dt_bias, a_log, d_skip, norm_g, s_off=0):
    Bs, XW = xbc.shape
    d_inner = z.shape[1]
    n_heads = d_inner // SSD_HEAD_DIM
    gw = d_inner // SSD_GROUPS
    hpg = gw // SSD_HEAD_DIM
    ex = _ssd_expand_matrix(n_heads, d_inner)
    xcs, dtx, dax = pl.pallas_call(
        _ssd_sample_pre_kernel,
        out_shape=[jax.ShapeDtypeStruct((Bs, XW), F32), jax.ShapeDtypeStruct((Bs, d_inner), F32),
                   jax.ShapeDtypeStruct((Bs, d_inner), F32)],
        compiler_params=pltpu.CompilerParams(vmem_limit_bytes=VMEM_LIMIT), name="ssd_sample_pre")(
            xbc, buf_t, dt, cw, cb.reshape(1, XW), _pad_lanes(dt_bias), _pad_lanes(a_log), ex)
    r3 = lambda a: a.reshape(Bs, 1, a.shape[1])
    seq_row = lambda w: pl.BlockSpec((1, 1, w), lambda b: (b, 0, 0))
    state_blk = (1, n_heads, SSD_HEAD_DIM, SSD_STATE)
    s_new, yn = pl.pallas_call(
        functools.partial(_ssd_sample_state_kernel, d_inner=d_inner),
        grid=(Bs,),
        in_specs=[pl.BlockSpec(state_blk, lambda b: (b + s_off, 0, 0, 0)),
                  seq_row(XW), seq_row(d_inner), seq_row(d_inner), seq_row(d_inner),
                  pl.BlockSpec((1, d_inner), lambda b: (0, 0)), pl.BlockSpec((1, d_inner), lambda b: (0, 0))],
        out_specs=[pl.BlockSpec(state_blk, lambda b: (b, 0, 0, 0)), seq_row(d_inner)],
        out_shape=[jax.ShapeDtypeStruct((Bs,) + s0.shape[1:], F32), jax.ShapeDtypeStruct((Bs, 1, d_inner), BF16)],
        compiler_params=_cparams(("parallel",)), name="ssd_sample_state")(
            s0, r3(xcs), r3(dtx), r3(dax), r3(z),
            jnp.repeat(d_skip, SSD_HEAD_DIM).reshape(1, d_inner), norm_g.reshape(1, d_inner))
    return yn.reshape(Bs, d_inner), s_new


def _paged_topk_kernel(pt_ref, *refs, nblk, bps, ppb):
    pages = refs[:bps * ppb]
    q_ref, idx_ref, kmean_ref = refs[bps * ppb:]
    n = pl.program_id(1)
    for j in range(bps):
        tot = jnp.sum(pages[j * ppb][0, 0], axis=0)
        for r in range(1, ppb):
            tot = tot + jnp.sum(pages[j * ppb + r][0, 0], axis=0)
        kmean_ref[n * bps + j] = tot / float(MOBA_BLOCK)

    @pl.when(n == pl.num_programs(1) - 1)
    def _():
        gate = jnp.sum(kmean_ref[...] * q_ref[...], axis=-1, keepdims=True)
        blk_id = lax.broadcasted_iota(jnp.int32, gate.shape, 0)
        lane = lax.broadcasted_iota(jnp.int32, (A_HEADS, LANES), 1)
        out = jnp.zeros((A_HEADS, LANES), jnp.int32)
        for t in range(MOBA_TOPK):
            m = jnp.max(gate, axis=0, keepdims=True)
            idx = jnp.min(jnp.where(gate == m, blk_id, nblk), axis=0, keepdims=True)
            out = jnp.where(lane == t, idx[0], out)
            gate = jnp.where(blk_id == idx, -jnp.inf, gate)
        idx_ref[0] = out


def _paged_attn_kernel(pt_ref, top_ref, q_ref, kn_ref, vn_ref, k_hbm, v_hbm, o_ref, kbuf, vbuf, sem, *, layer, ppb):
    b = pl.program_id(0)
    n_pages = MOBA_TOPK * ppb

    def page_copies(seq, slot):
        cps = []
        for h in range(A_HEADS):
            for sel in range(MOBA_TOPK):
                blk = top_ref[(seq * A_HEADS + h) * MOBA_TOPK + sel]
                for r in range(ppb):
                    page = pt_ref[seq, blk * ppb + r]
                    i = h * n_pages + sel * ppb + r
                    cps.append(pltpu.make_async_copy(k_hbm.at[layer, page, :, h, :], kbuf.at[slot, i], sem.at[0, slot]))
                    cps.append(pltpu.make_async_copy(v_hbm.at[layer, page, :, h, :], vbuf.at[slot, i], sem.at[1, slot]))
        return cps

    @pl.when(b == 0)
    def _():
        for cp in page_copies(0, 0):
            cp.start()

    @pl.when(b + 1 < pl.num_programs(0))
    def _():
        for cp in page_copies(b + 1, (b + 1) % 2):
            cp.start()

    slot = b % 2
    for cp in page_copies(b, slot):
        cp.wait()

    rows = []
    for h in range(A_HEADS):
        q = q_ref[0, h:h + 1, :]
        qh = jnp.broadcast_to(q, (SUBLANES, HEAD_DIM)).astype(BF16)
        s_own = jnp.sum(q * kn_ref[0, h:h + 1, :], axis=-1, keepdims=True) * ATTN_SCALE
        ss = [lax.dot_general(qh, kbuf[slot, h * n_pages + i].astype(BF16), NT_DIMS,
                              preferred_element_type=F32) * ATTN_SCALE for i in range(n_pages)]
        m = s_own
        for sc in ss:
            m = jnp.maximum(m, jnp.max(sc[0:1], axis=-1, keepdims=True))
        p_own = jnp.exp(s_own - m)
        l = p_own
        acc = p_own * vn_ref[0, h:h + 1, :]
        for i, sc in enumerate(ss):
            p = jnp.exp(sc - m)
            l = l + jnp.sum(p[0:1], axis=-1, keepdims=True)
            acc = acc + jnp.dot(p.astype(BF16), vbuf[slot, h * n_pages + i].astype(BF16),
                                preferred_element_type=F32)[0:1]
        rows.append(acc / l)
    o_ref[0] = jnp.concatenate(rows, axis=0)


def _paged_moba(q, k_new, v_new, k_pool, v_pool, page_table, layer):
    Bs = q.shape[0]
    assert q.shape == (Bs, A_HEADS, HEAD_DIM)
    n_pages = page_table.shape[1]
    ppb = MOBA_BLOCK // PAGE_SIZE
    assert n_pages % ppb == 0
    nblk = n_pages // ppb
    assert nblk >= MOBA_TOPK
    bps = math.gcd(nblk, 4)
    q3, k3, v3 = q, k_new, v_new
    page_blk = (1, 1, PAGE_SIZE, A_HEADS, HEAD_DIM)

    def mean_spec(p):
        return pl.BlockSpec(page_blk, lambda b, n, pt, p=p: (layer, pt[b, n * bps * ppb + p], 0, 0, 0))

    idx = pl.pallas_call(
        functools.partial(_paged_topk_kernel, nblk=nblk, bps=bps, ppb=ppb),
        grid_spec=pltpu.PrefetchScalarGridSpec(
            num_scalar_prefetch=1, grid=(Bs, nblk // bps),
            in_specs=[mean_spec(p) for p in range(bps * ppb)]
            + [pl.BlockSpec((1, A_HEADS, HEAD_DIM), lambda b, n, pt: (b, 0, 0))],
            out_specs=pl.BlockSpec((1, A_HEADS, LANES), lambda b, n, pt: (b, 0, 0)),
            scratch_shapes=[pltpu.VMEM((nblk, A_HEADS, HEAD_DIM), F32)]),
        out_shape=jax.ShapeDtypeStruct((Bs, A_HEADS, LANES), jnp.int32),
        compiler_params=_cparams(("parallel", "arbitrary")), name="paged_topk")(
            page_table, *([k_pool] * (bps * ppb)), q3)
    top = idx[:, :, :MOBA_TOPK].reshape(-1)

    n_sel_pages = MOBA_TOPK * ppb
    seq_spec = pl.BlockSpec((1, A_HEADS, HEAD_DIM), lambda b, pt, tp: (b, 0, 0))
    hbm_spec = pl.BlockSpec(memory_space=pl.ANY)
    att = pl.pallas_call(
        functools.partial(_paged_attn_kernel, layer=layer, ppb=ppb),
        grid_spec=pltpu.PrefetchScalarGridSpec(
            num_scalar_prefetch=2, grid=(Bs,),
            in_specs=[seq_spec, seq_spec, seq_spec, hbm_spec, hbm_spec],
            out_specs=seq_spec,
            scratch_shapes=[pltpu.VMEM((2, A_HEADS * n_sel_pages, PAGE_SIZE, HEAD_DIM), F32),
                            pltpu.VMEM((2, A_HEADS * n_sel_pages, PAGE_SIZE, HEAD_DIM), F32),
                            pltpu.SemaphoreType.DMA((2, 2))]),
        out_shape=jax.ShapeDtypeStruct((Bs, A_HEADS, HEAD_DIM), F32),
        compiler_params=_cparams(("arbitrary",)), name="paged_attn")(
            page_table, top, q3, k3, v3, k_pool, v_pool)
    return att


def _rope_tables(pos):
    inv = ROPE_THETA ** (-jnp.arange(ROT_HALF, dtype=F32) / ROT_HALF)
    ang = pos.astype(F32)[:, None] * inv
    cos, sin = jnp.cos(ang), jnp.sin(ang)
    T = pos.shape[0]
    rest = HEAD_DIM - ROT_DIM
    cos_t = jnp.concatenate([cos, cos, jnp.ones((T, rest), F32)], axis=-1)
    sin_t = jnp.concatenate([-sin, sin, jnp.zeros((T, rest), F32)], axis=-1)
    return cos_t, sin_t


ROW_TILE = 1024


def _row_tile(m):
    return ROW_TILE if m % ROW_TILE == 0 else m


def kernel(x_prompt, x_sample, cache_k, cache_v, state_lru_conv, state_lru_h, state_ssd_conv, state_ssd_h, page_table, norm_mix, norm_mlp, w_in_a, q_norm, k_norm, lru_conv_w, lru_conv_b, lru_wa, lru_ba, lru_wx, lru_bx, lru_lambda, w_out_a, w_in_s, ssd_conv_w, ssd_conv_b, ssd_dt_bias, ssd_a_log, ssd_d, ssd_norm, w_out_s, w_up, w_down):
    Bp, Tp, D = x_prompt.shape
    Bs, Ts, _ = x_sample.shape
    assert Ts == 1
    depth = norm_mix.shape[0]
    past_len = page_table.shape[1] * PAGE_SIZE
    Mp, Ms = Bp * Tp, Bs * Ts
    tmp, tms = _row_tile(Mp), _row_tile(Ms)
    tmo = tmp // 2 if tmp == ROW_TILE else tmp
    tmr = tmp // 4 if tmp == ROW_TILE else tmp
    assert Tp % tmr == 0
    assert Tp % tmp == 0
    xp = x_prompt.reshape(Mp, D)
    xs = x_sample.reshape(Ms, D)
    lru_w = lru_lambda.shape[1]
    d_inner = ssd_norm.shape[1]
    xbc_w = ssd_conv_w.shape[2]
    n_ssd_heads = ssd_d.shape[1]

    cos_p, sin_p = _rope_tables(jnp.arange(Tp, dtype=jnp.int32))
    cos_s, sin_s = _rope_tables(jnp.full((Ms,), past_len, dtype=jnp.int32))
    w_up_h, w_down_h = w_up.astype(BF16), w_down.astype(BF16)
    w_in_a_h, w_in_s_h = w_in_a.astype(BF16), w_in_s.astype(BF16)

    outs = {k: [] for k in ("kp", "vp", "ks", "vs", "lcp", "lcs", "lhp", "lhs", "scp", "scs", "shp", "shs")}
    for layer in range(depth):
        i = layer // 2
        if layer % 2 == 0:
            w_in = w_in_a_h
            w_out = w_out_a[i].astype(BF16)
            splits = [A_W, A_W, A_W, lru_w, lru_w]
            lru_args = (lru_conv_w[i], lru_conv_b[i], lru_wa[i], lru_ba[i], lru_wx[i], lru_bx[i], lru_lambda[i])
            heads = lambda a: a.reshape(a.shape[0], A_HEADS, HEAD_DIM)
            q, k, v, xr, yg = _norm_proj_resident(xp, norm_mix[layer], w_in, i, 0, splits, tm=tmr,
                                                  rope=((q_norm[i], k_norm[i]), cos_p, sin_p, Tp // tmr))
            att = _moba_prompt(q, k, v, batch=Bp, seq=Tp)
            rec, h_last = _lru_prompt(xr, yg, *lru_args, batch=Bp, seq=Tp)
            mixed_p = _out_proj([att, rec], w_out, xp, tm=tmo)
            outs["kp"].append(k.reshape(Bp, Tp, A_HEADS, HEAD_DIM))
            outs["vp"].append(v.reshape(Bp, Tp, A_HEADS, HEAD_DIM))
            outs["lcp"].append(xr.reshape(Bp, Tp, lru_w)[:, Tp - (CONV_K - 1):])
            outs["lhp"].append(h_last.reshape(Bp, lru_w))
            q, k, v, xr, yg = _norm_proj_resident(xs, norm_mix[layer], w_in, i, 0, splits, tm=tms,
                                                  rope=((q_norm[i], k_norm[i]), cos_s, sin_s, 1))
            att = _paged_moba(heads(q), heads(k), heads(v), cache_k, cache_v, page_table, layer=i)
            att = att.reshape(Bs, A_W).astype(BF16)
            buf = state_lru_conv[i]
            rec, h_new = _lru_sample(xr, yg, buf.swapaxes(0, 1), state_lru_h[i], *lru_args)
            mixed_s = _out_proj([att, rec], w_out, xs, tm=tms)
            outs["ks"].append(k.reshape(Bs, Ts, A_HEADS, HEAD_DIM))
            outs["vs"].append(v.reshape(Bs, Ts, A_HEADS, HEAD_DIM))
            outs["lcs"].append(jnp.concatenate([buf[:, 1:], xr[:, None, :]], axis=1))
            outs["lhs"].append(h_new)
        else:
            main_w = d_inner + xbc_w
            w_dt = jnp.pad(w_in_s[i][:, main_w:], ((0, 0), (0, LANES - n_ssd_heads))).astype(BF16)
            w_out = w_out_s[i].astype(BF16)
            ssd_args = (ssd_conv_w[i], ssd_conv_b[i], ssd_dt_bias[i], ssd_a_log[i], ssd_d[i], ssd_norm[i])
            unslab = lambda a: a.swapaxes(0, 1).reshape(a.shape[1], -1)
            z, dt = _norm_proj_resident(xp, norm_mix[layer], w_in_s_h, i, 0, [d_inner], tm=tmo, extra_w=w_dt, slab=True)
            xbc, = _norm_proj_resident(xp, norm_mix[layer], w_in_s_h, i, d_inner, [xbc_w], tm=tmr, slab=True)
            yn, s_last = _ssd_prompt(xbc, z, dt, *ssd_args, batch=Bp, seq=Tp)
            mixed_p = _out_proj([yn], w_out, xp, tm=tmo)
            tail = xbc.reshape(-1, Bp, Tp, LANES)[:, :, Tp - (CONV_K - 1):]
            outs["scp"].append(tail.transpose(1, 2, 0, 3).reshape(Bp, CONV_K - 1, xbc_w))
            outs["shp"].append(s_last.reshape(Bp, n_ssd_heads, SSD_HEAD_DIM, SSD_STATE))
            z, dt = _norm_proj_resident(xs, norm_mix[layer], w_in_s_h, i, 0, [d_inner], tm=tms, extra_w=w_dt, slab=True)
            xbc, = _norm_proj_resident(xs, norm_mix[layer], w_in_s_h, i, d_inner, [xbc_w], tm=tms, slab=True)
            z, xbc = unslab(z), unslab(xbc)
            buf = state_ssd_conv[i]
            yn, s_new = _ssd_sample(xbc, buf.swapaxes(0, 1), z, dt,
                                    state_ssd_h.reshape((-1,) + state_ssd_h.shape[2:]), *ssd_args, s_off=i * Bs)
            mixed_s = _out_proj([yn], w_out, xs, tm=tms)
            outs["scs"].append(jnp.concatenate([buf[:, 1:], xbc[:, None, :]], axis=1))
            outs["shs"].append(s_new)
        xp = _mlp(mixed_p, norm_mlp[layer], w_up_h, w_down_h, layer, tm=tmp)
        xs = _mlp(mixed_s, norm_mlp[layer], w_up_h, w_down_h, layer, tm=tms)

    st = lambda key: jnp.stack(outs[key])
    return (xp.reshape(Bp, Tp, D), xs.reshape(Bs, Ts, D), st("kp"), st("vp"), st("ks"), st("vs"),
            st("lcp"), st("lcs"), st("lhp"), st("lhs"), st("scp"), st("scs"), st("shp"), st("shs"))
```

```python
import functools
import math

import jax
import jax.numpy as jnp
from jax import lax
from jax.experimental import pallas as pl
from jax.experimental.pallas import tpu as pltpu

F32 = jnp.float32
BF16 = jnp.bfloat16
HIGHEST = lax.Precision.HIGHEST

A_HEADS = 8
HEAD_DIM = 128
A_W = A_HEADS * HEAD_DIM
ROT_DIM = HEAD_DIM // 4
ROT_HALF = ROT_DIM // 2
ROPE_THETA = 500000.0
MOBA_BLOCK = 256
MOBA_TOPK = 3
ATTN_SCALE = HEAD_DIM ** -0.5
LRU_HEADS = 8
LRU_C = 8.0
CONV_K = 4
SSD_HEAD_DIM = 64
SSD_STATE = 128
SSD_GROUPS = 8
SSD_CHUNK = 128
PAGE_SIZE = 128
EPS = 1e-6

LANES = 128
SUBLANES = 8
NEG_BIG = -1e30
LOG2E = math.log2(math.e)
VMEM_BYTES_V7X = 64 * 1024 * 1024
VMEM_LIMIT = VMEM_BYTES_V7X - 8 * 1024 * 1024
VMEM_LIMIT_MLP = VMEM_BYTES_V7X - 4 * 1024 * 1024

NT_DIMS = (((1,), (1,)), ((), ()))


def _cparams(sem, vmem=VMEM_LIMIT):
    return pltpu.CompilerParams(dimension_semantics=sem, vmem_limit_bytes=vmem)


def _sigmoid(x):
    return 0.5 * (jnp.tanh(0.5 * x) + 1.0)


def _silu(x):
    h = 0.5 * x
    return h * (jnp.tanh(h) + 1.0)


def _rms_rows(x, g):
    return x * lax.rsqrt(jnp.mean(x * x, axis=-1, keepdims=True) + EPS) * g


def _head_norm_rope(acc, g, cos, sin):
    y = _rms_rows(acc, g)
    lane = lax.broadcasted_iota(jnp.int32, y.shape, 1)
    partner = jnp.where(lane < ROT_HALF, pltpu.roll(y, LANES - ROT_HALF, 1), pltpu.roll(y, ROT_HALF, 1))
    return y * cos + partner * sin


RESIDENT_DOT_COLS = 1024


def _norm_proj_resident_kernel(*refs, splits, n_rope, has_extra, slab):
    it = iter(refs)
    x_ref, g_ref = next(it), next(it)
    w_refs = [next(it) for _ in range(sum(splits) // RESIDENT_DOT_COLS)]
    rope_g = [next(it) for _ in range(n_rope)]
    if n_rope:
        cos_ref, sin_ref = next(it), next(it)
    if has_extra:
        we_ref = next(it)
    outs = [next(it) for _ in splits]
    xn = _rms_rows(x_ref[...], g_ref[...]).astype(BF16)
    if has_extra:
        next(it)[...] = jnp.dot(xn, we_ref[...], preferred_element_type=F32)
    col = 0
    for s, width in enumerate(splits):
        for c0 in range(0, width, RESIDENT_DOT_COLS):
            acc = jnp.dot(xn, w_refs[(col + c0) // RESIDENT_DOT_COLS][...], preferred_element_type=F32)
            for hh in range(RESIDENT_DOT_COLS // LANES):
                sl = slice(hh * LANES, (hh + 1) * LANES)
                val = acc[:, sl]
                if s < n_rope:
                    val = _head_norm_rope(val, rope_g[s][...], cos_ref[...], sin_ref[...])
                if slab[s]:
                    outs[s][c0 // LANES + hh] = val
                else:
                    outs[s][:, c0 + hh * LANES:c0 + (hh + 1) * LANES] = val
        col += width


def _norm_proj_resident(x, g, w, w_layer, w_col0, splits, *, tm, rope=None, extra_w=None, slab=False):
    M, K = x.shape
    N = sum(splits)
    assert w.ndim == 3 and w.shape[1] == K and w_col0 % RESIDENT_DOT_COLS == 0 and w_col0 + N <= w.shape[2]
    assert M % tm == 0 and all(s % RESIDENT_DOT_COLS == 0 for s in splits)
    assert HEAD_DIM == LANES
    n_rope = 0 if rope is None else len(rope[0])
    in_specs = [pl.BlockSpec((tm, K), lambda i: (i, 0)), pl.BlockSpec((1, K), lambda i: (0, 0))]
    blk0 = w_col0 // RESIDENT_DOT_COLS
    for j in range(N // RESIDENT_DOT_COLS):
        in_specs.append(pl.BlockSpec((None, K, RESIDENT_DOT_COLS), lambda i, j=j: (w_layer, 0, blk0 + j),
                                     pipeline_mode=pl.Buffered(1)))
    args = [x, g.reshape(1, K)] + [w] * (N // RESIDENT_DOT_COLS)
    if rope is not None:
        gains, cos, sin, period = rope
        for gg in gains:
            in_specs.append(pl.BlockSpec((1, HEAD_DIM), lambda i: (0, 0)))
            args.append(gg.reshape(1, HEAD_DIM))
        for tab in (cos, sin):
            in_specs.append(pl.BlockSpec((tm, HEAD_DIM), lambda i, period=period: (i % period, 0)))
            args.append(tab)
    if extra_w is not None:
        in_specs.append(pl.BlockSpec((K, LANES), lambda i: (0, 0)))
        args.append(extra_w)
    slab = tuple(slab) if isinstance(slab, (tuple, list)) else (bool(slab),) * len(splits)
    out_specs, out_shape = [], []
    for s, as_slab in zip(splits, slab):
        if as_slab:
            out_specs.append(pl.BlockSpec((s // LANES, tm, LANES), lambda i: (0, i, 0)))
            out_shape.append(jax.ShapeDtypeStruct((s // LANES, M, LANES), F32))
        else:
            out_specs.append(pl.BlockSpec((tm, s), lambda i: (i, 0)))
            out_shape.append(jax.ShapeDtypeStruct((M, s), F32))
    if extra_w is not None:
        out_specs.append(pl.BlockSpec((tm, LANES), lambda i: (i, 0)))
        out_shape.append(jax.ShapeDtypeStruct((M, LANES), F32))
    return pl.pallas_call(
        functools.partial(_norm_proj_resident_kernel, splits=tuple(splits), n_rope=n_rope,
                          has_extra=extra_w is not None, slab=slab),
        grid=(M // tm,), in_specs=in_specs, out_specs=out_specs, out_shape=out_shape,
        compiler_params=_cparams(("parallel",)), name="norm_proj_resident")(*args)


def _out_proj_kernel(*refs, n_parts, kp):
    xs = refs[:n_parts]
    w_ref, res_ref, o_ref = refs[n_parts:]
    acc = res_ref[...]
    for p, x_ref in enumerate(xs):
        acc = acc + jnp.dot(x_ref[...], w_ref[p * kp:(p + 1) * kp, :], preferred_element_type=F32)
    o_ref[...] = acc


def _out_proj(parts, w, res, *, tm):
    M, N = res.shape
    kp = parts[0].shape[1]
    assert all(p.shape == (M, kp) for p in parts) and w.shape == (kp * len(parts), N) and M % tm == 0
    in_specs = [pl.BlockSpec((tm, kp), lambda i: (i, 0)) for _ in parts]
    in_specs.append(pl.BlockSpec(w.shape, lambda i: (0, 0), pipeline_mode=pl.Buffered(1)))
    in_specs.append(pl.BlockSpec((tm, N), lambda i: (i, 0)))
    return pl.pallas_call(
        functools.partial(_out_proj_kernel, n_parts=len(parts), kp=kp),
        grid=(M // tm,), in_specs=in_specs,
        out_specs=pl.BlockSpec((tm, N), lambda i: (i, 0)),
        out_shape=jax.ShapeDtypeStruct((M, N), F32),
        compiler_params=_cparams(("parallel",)), name="out_proj")(*parts, w, res)


def _mlp_kernel(x_ref, g_ref, wu_ref, wd_ref, o_ref, xn_ref):
    f = pl.program_id(1)

    @pl.when(f == 0)
    def _():
        x = x_ref[...]
        xn_ref[...] = _rms_rows(x, g_ref[...]).astype(BF16)
        o_ref[...] = x

    h = jnp.maximum(jnp.dot(xn_ref[...], wu_ref[...], preferred_element_type=F32), 0.0)
    o_ref[...] += jnp.dot((h * h).astype(BF16), wd_ref[...], preferred_element_type=F32)


def _mlp(x, g, w_up, w_down, layer, *, tm, tf=1024):
    M, D = x.shape
    FF = w_up.shape[2]
    return pl.pallas_call(
        _mlp_kernel, grid=(M // tm, FF // tf),
        in_specs=[pl.BlockSpec((tm, D), lambda i, f: (i, 0)),
                  pl.BlockSpec((1, D), lambda i, f: (0, 0)),
                  pl.BlockSpec((None, D, tf), lambda i, f: (layer, 0, f)),
                  pl.BlockSpec((None, tf, D), lambda i, f: (layer, f, 0))],
        out_specs=pl.BlockSpec((tm, D), lambda i, f: (i, 0)),
        out_shape=jax.ShapeDtypeStruct((M, D), F32),
        scratch_shapes=[pltpu.VMEM((tm, D), BF16)],
        compiler_params=_cparams(("parallel", "arbitrary"), VMEM_LIMIT_MLP), name="mlp")(
            x, g.reshape(1, D), w_up, w_down)


def _moba_prompt_kernel(q_ref, k_ref, v_ref, hot_ref, o_ref, kaug_ref, vb_ref, kmean_ref, *, nb):
    blk, D = MOBA_BLOCK, HEAD_DIM
    kaug_ref[:, 0:D] = k_ref[...].astype(BF16)
    kaug_ref[:, D:2 * D] = hot_ref[...]
    vb_ref[...] = v_ref[...].astype(BF16)
    kmean_ref[...] = jnp.zeros(kmean_ref.shape, F32)
    for n in range(nb):
        kmean_ref[n:n + 1, :] = jnp.mean(k_ref[n * blk:(n + 1) * blk, :], axis=0, keepdims=True)

    lane = lax.broadcasted_iota(jnp.int32, (blk, LANES), 1)
    row = lax.broadcasted_iota(jnp.int32, (blk, blk), 0)
    col = lax.broadcasted_iota(jnp.int32, (blk, blk), 1)
    nbp = kmean_ref.shape[0]
    blk_id = lax.broadcasted_iota(jnp.int32, (nbp, blk), 0)
    eye = (row == col).astype(BF16)
    for qb in range(nb):
        q = q_ref[qb * blk:(qb + 1) * blk, :]
        if qb > MOBA_TOPK:
            gate = lax.dot_general(kmean_ref[...], q, NT_DIMS, precision=HIGHEST, preferred_element_type=F32)
            gate = jnp.where(blk_id < qb, gate, -jnp.inf)
            bias_t = jnp.where(blk_id == qb, 0.0, NEG_BIG)
            for _ in range(MOBA_TOPK):
                m = jnp.max(gate, axis=0, keepdims=True)
                idx = jnp.min(jnp.where(gate == m, blk_id, nbp), axis=0, keepdims=True)
                hit = blk_id == idx
                bias_t = jnp.where(hit & (m > -jnp.inf), 0.0, bias_t)
                gate = jnp.where(hit, -jnp.inf, gate)
            pad = jnp.zeros((LANES - nbp, blk), F32)
            bias = lax.dot_general(eye, jnp.concatenate([bias_t, pad], axis=0).astype(BF16), NT_DIMS,
                                   preferred_element_type=F32).astype(BF16)
        else:
            bias = jnp.where(lane <= qb, 0.0, NEG_BIG).astype(BF16)
        lhs = jnp.concatenate([(q * (ATTN_SCALE * LOG2E)).astype(BF16), bias], axis=1)
        own = slice(qb * blk, (qb + 1) * blk)
        s_own = lax.dot_general(lhs, kaug_ref[own, :], NT_DIMS, preferred_element_type=F32)
        s_own = jnp.where(col <= row, s_own, NEG_BIG)
        m = jnp.max(s_own, axis=-1, keepdims=True)
        if qb > 0:
            s_past = lax.dot_general(lhs, kaug_ref[0:qb * blk, :], NT_DIMS, preferred_element_type=F32)
            m = jnp.maximum(m, jnp.max(s_past, axis=-1, keepdims=True))
        p = jnp.exp2(s_own - m)
        l = jnp.sum(p, axis=-1, keepdims=True)
        acc = jnp.dot(p.astype(BF16), vb_ref[own, :], preferred_element_type=F32)
        if qb > 0:
            p = jnp.exp2(s_past - m)
            l = l + jnp.sum(p, axis=-1, keepdims=True)
            acc = acc + jnp.dot(p.astype(BF16), vb_ref[0:qb * blk, :], preferred_element_type=F32)
        o_ref[own, :] = (acc / l).astype(o_ref.dtype)


def _moba_prompt(q, k, v, *, batch, seq):
    M = q.shape[0]
    nb = seq // MOBA_BLOCK
    assert nb <= LANES and seq % MOBA_BLOCK == 0 and q.shape == (M, A_W)
    hot = (lax.broadcasted_iota(jnp.int32, (seq, LANES), 0) // MOBA_BLOCK
           == lax.broadcasted_iota(jnp.int32, (seq, LANES), 1)).astype(BF16)
    seq_head = pl.BlockSpec((seq, HEAD_DIM), lambda b, h: (b, h))
    return pl.pallas_call(
        functools.partial(_moba_prompt_kernel, nb=nb),
        grid=(batch, A_HEADS),
        in_specs=[seq_head, seq_head, seq_head, pl.BlockSpec((seq, LANES), lambda b, h: (0, 0))],
        out_specs=seq_head,
        out_shape=jax.ShapeDtypeStruct((M, A_W), BF16),
        scratch_shapes=[pltpu.VMEM((seq, 2 * HEAD_DIM), BF16), pltpu.VMEM((seq, HEAD_DIM), BF16),
                        pltpu.VMEM((-(-nb // SUBLANES) * SUBLANES, HEAD_DIM), F32)],
        compiler_params=_cparams(("parallel", "parallel")), name="moba_prompt")(q, k, v, hot)


def _lru_gates(xc, wa_ref, ba, wx_ref, bx, lam):
    bw = xc.shape[1] // LRU_HEADS
    rs, is_ = [], []
    for n in range(LRU_HEADS):
        xb = xc[:, n * bw:(n + 1) * bw].astype(BF16)
        rs.append(jnp.dot(xb, wa_ref[n], preferred_element_type=F32))
        is_.append(jnp.dot(xb, wx_ref[n], preferred_element_type=F32))
    r = _sigmoid(jnp.concatenate(rs, axis=-1) + ba)
    i = _sigmoid(jnp.concatenate(is_, axis=-1) + bx)
    log_a = -LRU_C * r * jax.nn.softplus(-lam)
    a = jnp.exp(log_a)
    t = jnp.tanh(log_a)
    u = jnp.sqrt(-2.0 * t / (1.0 - t)) * (i * xc)
    return a, u


def _lru_prompt_kernel(xr_ref, yg_ref, cw_ref, cb_ref, wa_ref, ba_ref, wx_ref, bx_ref, lam_ref,
                       rec_ref, hl_ref, xx_ref, a_ref, u_ref, h_ref, *, tt):
    ti = pl.program_id(1)
    pad = SUBLANES

    @pl.when(ti == 0)
    def _():
        xx_ref[0:pad, :] = jnp.zeros((pad, xx_ref.shape[1]), F32)
        h_ref[...] = jnp.zeros(h_ref.shape, F32)

    @pl.when(ti > 0)
    def _():
        xx_ref[0:pad, :] = xx_ref[tt:tt + pad, :]

    xx_ref[pad:pad + tt, :] = xr_ref[...]
    xc = cb_ref[...]
    for j in range(CONV_K):
        off = pad - (CONV_K - 1) + j
        xc = xc + xx_ref[off:off + tt, :] * cw_ref[j:j + 1, :]
    a, u = _lru_gates(xc, wa_ref, ba_ref[...], wx_ref, bx_ref[...], lam_ref[...])
    a_ref[...] = a
    u_ref[...] = u

    sub = lax.broadcasted_iota(jnp.int32, (SUBLANES, a.shape[1]), 0)

    def group(gi, h):
        r0 = pl.multiple_of(gi * SUBLANES, SUBLANES)
        a8 = a_ref[pl.ds(r0, SUBLANES), :]
        u8 = u_ref[pl.ds(r0, SUBLANES), :]
        for s in (1, 2, 4):
            keep = sub >= s
            a_sh = pltpu.roll(a8, s, 0)
            u_sh = pltpu.roll(u8, s, 0)
            u8 = jnp.where(keep, a8 * u_sh + u8, u8)
            a8 = jnp.where(keep, a8 * a_sh, a8)
        hs = a8 * h + u8
        yg = yg_ref[pl.ds(r0, SUBLANES), :]
        rec_ref[pl.ds(r0, SUBLANES), :] = (hs * jax.nn.gelu(yg)).astype(rec_ref.dtype)
        return hs[SUBLANES - 1:SUBLANES, :]

    h = lax.fori_loop(0, tt // SUBLANES, group, h_ref[...], unroll=4)
    h_ref[...] = h

    @pl.when(ti == pl.num_programs(1) - 1)
    def _():
        hl_ref[0] = h


def _lru_prompt(xr, yg, cw, cb, wa, ba, wx, bx, lam, *, batch, seq, tt=256):
    M, W = xr.shape
    nt = seq // tt
    row = lambda a: a.reshape(1, W)
    full2 = lambda shp: pl.BlockSpec(shp, lambda b, t: (0, 0))
    full3 = lambda shp: pl.BlockSpec(shp, lambda b, t: (0, 0, 0))
    return pl.pallas_call(
        functools.partial(_lru_prompt_kernel, tt=tt),
        grid=(batch, nt),
        in_specs=[pl.BlockSpec((tt, W), lambda b, t: (b * nt + t, 0)),
                  pl.BlockSpec((tt, W), lambda b, t: (b * nt + t, 0)),
                  full2((CONV_K, W)), full2((1, W)), full3(wa.shape), full2((1, W)),
                  full3(wx.shape), full2((1, W)), full2((1, W))],
        out_specs=[pl.BlockSpec((tt, W), lambda b, t: (b * nt + t, 0)),
                   pl.BlockSpec((1, 1, W), lambda b, t: (b, 0, 0))],
        out_shape=[jax.ShapeDtypeStruct((M, W), BF16), jax.ShapeDtypeStruct((batch, 1, W), F32)],
        scratch_shapes=[pltpu.VMEM((tt + SUBLANES, W), F32), pltpu.VMEM((tt, W), F32),
                        pltpu.VMEM((tt, W), F32), pltpu.VMEM((1, W), F32)],
        compiler_params=_cparams(("parallel", "arbitrary")), name="lru_prompt")(
            xr, yg, cw, row(cb), wa.astype(BF16), row(ba), wx.astype(BF16), row(bx), row(lam))


def _lru_sample_kernel(xr_ref, yg_ref, buf_ref, h0_ref, cw_ref, cb_ref, wa_ref, ba_ref, wx_ref, bx_ref,
                       lam_ref, rec_ref, hn_ref):
    xc = cb_ref[...]
    for j in range(CONV_K - 1):
        xc = xc + buf_ref[j] * cw_ref[j:j + 1, :]
    xc = xc + xr_ref[...] * cw_ref[CONV_K - 1:CONV_K, :]
    a, u = _lru_gates(xc, wa_ref, ba_ref[...], wx_ref, bx_ref[...], lam_ref[...])
    h = a * h0_ref[...] + u
    hn_ref[...] = h
    rec_ref[...] = (h * jax.nn.gelu(yg_ref[...])).astype(rec_ref.dtype)


def _lru_sample(xr, yg, buf_t, h0, cw, cb, wa, ba, wx, bx, lam):
    Bs, W = xr.shape
    row = lambda a: a.reshape(1, W)
    return pl.pallas_call(
        _lru_sample_kernel,
        out_shape=[jax.ShapeDtypeStruct((Bs, W), BF16), jax.ShapeDtypeStruct((Bs, W), F32)],
        compiler_params=pltpu.CompilerParams(vmem_limit_bytes=VMEM_LIMIT), name="lru_sample")(
            xr, yg, buf_t, h0, cw, row(cb), wa.astype(BF16), row(ba), wx.astype(BF16), row(bx), row(lam))


def _ssd_prompt_kernel(xbc_ref, prev_ref, z_ref, dt_ref, cw_ref, cb_ref, dtb_ref, alog_ref, dsk_ref, ng_ref,
                       ex_ref, yn_ref, sl_ref, xx_ref, xcv_ref, y_ref, st_ref, *, d_inner):
    c = pl.program_id(1)
    L = SSD_CHUNK
    pad = SUBLANES
    gw = d_inner // SSD_GROUPS
    spg = gw // LANES
    hps = LANES // SSD_HEAD_DIM
    head_of_lane = lax.broadcasted_iota(jnp.int32, (L, LANES), 1) // SSD_HEAD_DIM
    assert SSD_STATE == LANES

    @pl.when(c == 0)
    def _():
        xx_ref[:, 0:pad, :] = jnp.zeros((xx_ref.shape[0], pad, LANES), F32)
        st_ref[...] = jnp.zeros(st_ref.shape, F32)

    @pl.when(c > 0)
    def _():
        xx_ref[:, 0:pad, :] = prev_ref[...]

    xx_ref[:, pad:pad + L, :] = xbc_ref[...]

    def conv_slab(k, carry):
        xs = xx_ref[k]
        xc = cb_ref[k]
        for j in range(CONV_K):
            back = CONV_K - 1 - j
            tap = xs if back == 0 else pltpu.roll(xs, back, 0)
            xc = xc + tap[pad:] * cw_ref[k, j:j + 1, :]
        xcv_ref[k] = xc * (jnp.tanh(xc) + 1.0)
        return carry

    lax.fori_loop(0, xx_ref.shape[0], conv_slab, 0)

    def slabs(ref, first, count):
        return jnp.concatenate([ref[first + i] for i in range(count)], axis=1)

    def split2(a):
        hi = a.astype(BF16)
        return hi, (a - hi.astype(F32)).astype(BF16)

    dt = jax.nn.softplus(dt_ref[...] + dtb_ref[...])
    dta = dt * (-jnp.exp(alog_ref[...]))
    tril = lax.broadcasted_iota(jnp.int32, (L, L), 1) <= lax.broadcasted_iota(jnp.int32, (L, L), 0)
    cs = jnp.dot(tril.astype(F32), dta, precision=HIGHEST, preferred_element_type=F32)
    cs_l2 = cs * LOG2E
    adj_t = (cs_l2 - jnp.log2(dt)).T
    last = cs[L - 1:L, :]
    pk = 2 * SUBLANES
    cdec = jnp.broadcast_to(jnp.exp(last), (pk, LANES))
    cd_hi = cdec.astype(BF16)
    cd_mid, cd_lo = split2(cdec - cd_hi.astype(F32))
    spread_in = jnp.concatenate([jnp.concatenate(split2(dt * jnp.exp(last - cs)), axis=1),
                                 jnp.concatenate(split2(jnp.exp(cs)), axis=1),
                                 jnp.concatenate([cd_hi, cd_mid], axis=1),
                                 jnp.concatenate([cd_lo, jnp.zeros_like(cd_lo)], axis=1)], axis=0)

    for g in range(SSD_GROUPS):
        lo = g * gw
        spread = jnp.dot(spread_in, ex_ref[:, lo:lo + gw], preferred_element_type=F32)
        w_x, ecs_x = spread[0:L], spread[L:2 * L]
        cd_x = spread[2 * L:2 * L + 1] + spread[2 * L + pk:2 * L + pk + 1]
        xg = slabs(xcv_ref, g * spg, spg)
        bg = xcv_ref[d_inner // LANES + g]
        cg = xcv_ref[d_inner // LANES + SSD_GROUPS + g].astype(BF16)
        bg_t = bg.T.astype(BF16)
        cb = jnp.dot(cg, bg_t, preferred_element_type=F32)
        st_g = st_ref[:, lo:lo + gw]
        y_off = jnp.dot(cg, st_g.astype(BF16), preferred_element_type=F32) * ecs_x
        st_ref[:, lo:lo + gw] = st_g * cd_x + jnp.dot(bg_t, (xg * w_x).astype(BF16), preferred_element_type=F32)
        for i in range(spg):
            xs_h = xcv_ref[g * spg + i].astype(BF16)
            lhs, rhs = [], []
            for k in range(hps):
                h = (g * spg + i) * hps + k
                dec = jnp.exp2(jnp.where(tril, cs_l2[:, h:h + 1] - adj_t[h:h + 1, :], NEG_BIG))
                lhs.append((cb * dec).astype(BF16))
                rhs.append(jnp.where(head_of_lane == k, xs_h, jnp.zeros_like(xs_h)))
            yd = jnp.dot(jnp.concatenate(lhs, axis=1), jnp.concatenate(rhs, axis=0), preferred_element_type=F32)
            cols = slice(lo + i * LANES, lo + (i + 1) * LANES)
            y_ref[:, cols] = yd + y_off[:, i * LANES:(i + 1) * LANES]

    for g in range(SSD_GROUPS):
        lo = g * gw
        cols = slice(lo, lo + gw)
        y = y_ref[:, cols] + slabs(xcv_ref, g * spg, spg) * dsk_ref[:, cols]
        zz = slabs(z_ref, g * spg, spg)
        u = y * _silu(zz)
        yn_ref[:, cols] = _rms_rows(u, ng_ref[:, cols]).astype(yn_ref.dtype)

    @pl.when(c == pl.num_programs(1) - 1)
    def _():
        for k in range(d_inner // LANES):
            sl_ref[0, k * LANES:(k + 1) * LANES, :] = st_ref[:, k * LANES:(k + 1) * LANES].T


def _ssd_expand_matrix(n_heads, d_inner):
    h = lax.broadcasted_iota(jnp.int32, (LANES, d_inner), 0)
    col = lax.broadcasted_iota(jnp.int32, (LANES, d_inner), 1)
    return (col // SSD_HEAD_DIM == h).astype(F32) * (h < n_heads).astype(F32)


def _pad_lanes(v):
    return jnp.pad(v, (0, LANES - v.shape[0])).reshape(1, LANES)


def _ssd_prompt(xbc, z, dt, cw, cb, dt_bias, a_log, d_skip, norm_g, *, batch, seq):
    nxs, M, _ = xbc.shape
    nzs = z.shape[0]
    XW, d_inner = nxs * LANES, nzs * LANES
    n_heads = d_inner // SSD_HEAD_DIM
    L = SSD_CHUNK
    nc = seq // L
    rpb = L // SUBLANES
    const2 = lambda shp: pl.BlockSpec(shp, lambda b, c: (0, 0))
    const3 = lambda shp: pl.BlockSpec(shp, lambda b, c: (0, 0, 0))
    return pl.pallas_call(
        functools.partial(_ssd_prompt_kernel, d_inner=d_inner),
        grid=(batch, nc),
        in_specs=[pl.BlockSpec((nxs, L, LANES), lambda b, c: (0, b * nc + c, 0)),
                  pl.BlockSpec((nxs, SUBLANES, LANES), lambda b, c: (0, jnp.maximum((b * nc + c) * rpb - 1, 0), 0)),
                  pl.BlockSpec((nzs, L, LANES), lambda b, c: (0, b * nc + c, 0)),
                  pl.BlockSpec((L, LANES), lambda b, c: (b * nc + c, 0)),
                  const3((nxs, CONV_K, LANES)), const3((nxs, 1, LANES)), const2((1, LANES)), const2((1, LANES)),
                  const2((1, d_inner)), const2((1, d_inner)), const2((2 * LANES, d_inner))],
        out_specs=[pl.BlockSpec((L, d_inner), lambda b, c: (b * nc + c, 0)),
                   pl.BlockSpec((1, d_inner, SSD_STATE), lambda b, c: (b, 0, 0))],
        out_shape=[jax.ShapeDtypeStruct((M, d_inner), BF16),
                   jax.ShapeDtypeStruct((batch, d_inner, SSD_STATE), F32)],
        scratch_shapes=[pltpu.VMEM((nxs, L + SUBLANES, LANES), F32), pltpu.VMEM((nxs, L, LANES), F32),
                        pltpu.VMEM((L, d_inner), F32), pltpu.VMEM((SSD_STATE, d_inner), F32)],
        compiler_params=_cparams(("parallel", "arbitrary")), name="ssd_prompt")(
            xbc, xbc, z, dt, (0.5 * cw).reshape(CONV_K, nxs, LANES).swapaxes(0, 1), (0.5 * cb).reshape(nxs, 1, LANES),
            _pad_lanes(dt_bias), _pad_lanes(a_log),
            jnp.repeat(d_skip, SSD_HEAD_DIM).reshape(1, d_inner), norm_g.reshape(1, d_inner),
            jnp.tile(_ssd_expand_matrix(n_heads, d_inner).astype(BF16), (2, 1)))


def _ssd_sample_pre_kernel(x_ref, buf_ref, dt_ref, cw_ref, cb_ref, dtb_ref, alog_ref, ex_ref,
                           xcs_ref, dtx_ref, dax_ref):
    xc = cb_ref[...]
    for j in range(CONV_K - 1):
        xc = xc + buf_ref[j] * cw_ref[j:j + 1, :]
    xc = xc + x_ref[...] * cw_ref[CONV_K - 1:CONV_K, :]
    xcs_ref[...] = _silu(xc)
    dt = jax.nn.softplus(dt_ref[...] + dtb_ref[...])
    da = jnp.exp(dt * (-jnp.exp(alog_ref[...])))
    ex = ex_ref[...]
    dtx_ref[...] = jnp.dot(dt, ex, precision=HIGHEST, preferred_element_type=F32)
    dax_ref[...] = jnp.dot(da, ex, precision=HIGHEST, preferred_element_type=F32)


TN_DIMS = (((0,), (0,)), ((), ()))


def _ssd_sample_state_kernel(s0_ref, xcs_ref, dtx_ref, dax_ref, z_ref, dsk_ref, ng_ref, sn_ref, yn_ref, *, d_inner):
    gw = d_inner // SSD_GROUPS
    hpg = gw // SSD_HEAD_DIM
    nst = SSD_STATE
    row_id = lax.broadcasted_iota(jnp.int32, (SUBLANES, gw), 0)
    row_id_n = lax.broadcasted_iota(jnp.int32, (SUBLANES, nst), 0)
    ones3 = jnp.where(row_id_n < 3, 1.0, 0.0).astype(BF16)
    xcs = xcs_ref[0]
    for g in range(SSD_GROUPS):
        cols = slice(g * gw, (g + 1) * gw)
        x = xcs[:, cols]
        b_row = xcs[:, d_inner + g * nst:d_inner + (g + 1) * nst]
        c_row = xcs[:, d_inner + SSD_GROUPS * nst + g * nst:d_inner + SSD_GROUPS * nst + (g + 1) * nst]
        xdt8 = jnp.where(row_id == 0, x * dtx_ref[0][:, cols], 0.0).astype(BF16)
        b8 = jnp.where(row_id_n == 0, b_row, 0.0).astype(BF16)
        outer = lax.dot_general(xdt8, b8, TN_DIMS, preferred_element_type=F32)
        da = dax_ref[0][:, cols]
        da_hi = da.astype(BF16).astype(F32)
        da_mid = (da - da_hi).astype(BF16).astype(F32)
        da_lo = da - da_hi - da_mid
        da8 = jnp.where(row_id == 0, da_hi, jnp.where(row_id == 1, da_mid, jnp.where(row_id == 2, da_lo, 0.0)))
        da_col = lax.dot_general(da8.astype(BF16), ones3, TN_DIMS, preferred_element_type=F32)
        heads = slice(g * hpg, (g + 1) * hpg)
        s_new = s0_ref[0, heads].reshape(gw, nst) * da_col + outer
        sn_ref[0, heads] = s_new.reshape(hpg, SSD_HEAD_DIM, nst)
        c8 = jnp.broadcast_to(c_row, (SUBLANES, nst)).astype(BF16)
        y = lax.dot_general(c8, s_new.astype(BF16), NT_DIMS, preferred_element_type=F32)[0:1]
        u = (y + x * dsk_ref[:, cols]) * _silu(z_ref[0][:, cols])
        yn_ref[0, :, cols] = _rms_rows(u, ng_ref[:, cols]).astype(yn_ref.dtype)


def _ssd_sample(xbc, buf_t, z, dt, s0, cw, cb, dt_bias, a_log, d_skip, norm_g, s_off=0):
    Bs, XW = xbc.shape
    d_inner = z.shape[1]
    n_heads = d_inner // SSD_HEAD_DIM
    gw = d_inner // SSD_GROUPS
    hpg = gw // SSD_HEAD_DIM
    ex = _ssd_expand_matrix(n_heads, d_inner)
    xcs, dtx, dax = pl.pallas_call(
        _ssd_sample_pre_kernel,
        out_shape=[jax.ShapeDtypeStruct((Bs, XW), F32), jax.ShapeDtypeStruct((Bs, d_inner), F32),
                   jax.ShapeDtypeStruct((Bs, d_inner), F32)],
        compiler_params=pltpu.CompilerParams(vmem_limit_bytes=VMEM_LIMIT), name="ssd_sample_pre")(
            xbc, buf_t, dt, cw, cb.reshape(1, XW), _pad_lanes(dt_bias), _pad_lanes(a_log), ex)
    r3 = lambda a: a.reshape(Bs, 1, a.shape[1])
    seq_row = lambda w: pl.BlockSpec((1, 1, w), lambda b: (b, 0, 0))
    state_blk = (1, n_heads, SSD_HEAD_DIM, SSD_STATE)
    s_new, yn = pl.pallas_call(
        functools.partial(_ssd_sample_state_kernel, d_inner=d_inner),
        grid=(Bs,),
        in_specs=[pl.BlockSpec(state_blk, lambda b: (b + s_off, 0, 0, 0)),
                  seq_row(XW), seq_row(d_inner), seq_row(d_inner), seq_row(d_inner),
                  pl.BlockSpec((1, d_inner), lambda b: (0, 0)), pl.BlockSpec((1, d_inner), lambda b: (0, 0))],
        out_specs=[pl.BlockSpec(state_blk, lambda b: (b, 0, 0, 0)), seq_row(d_inner)],
        out_shape=[jax.ShapeDtypeStruct((Bs,) + s0.shape[1:], F32), jax.ShapeDtypeStruct((Bs, 1, d_inner), BF16)],
        compiler_params=_cparams(("parallel",)), name="ssd_sample_state")(
            s0, r3(xcs), r3(dtx), r3(dax), r3(z),
            jnp.repeat(d_skip, SSD_HEAD_DIM).reshape(1, d_inner), norm_g.reshape(1, d_inner))
    return yn.reshape(Bs, d_inner), s_new


def _paged_topk_kernel(pt_ref, *refs, nblk, bps, ppb):
    pages = refs[:bps * ppb]
    q_ref, idx_ref, kmean_ref = refs[bps * ppb:]
    n = pl.program_id(1)
    for j in range(bps):
        tot = jnp.sum(pages[j * ppb][0, 0], axis=0)
        for r in range(1, ppb):
            tot = tot + jnp.sum(pages[j * ppb + r][0, 0], axis=0)
        kmean_ref[n * bps + j] = tot / float(MOBA_BLOCK)

    @pl.when(n == pl.num_programs(1) - 1)
    def _():
        gate = jnp.sum(kmean_ref[...] * q_ref[...], axis=-1, keepdims=True)
        blk_id = lax.broadcasted_iota(jnp.int32, gate.shape, 0)
        lane = lax.broadcasted_iota(jnp.int32, (A_HEADS, LANES), 1)
        out = jnp.zeros((A_HEADS, LANES), jnp.int32)
        for t in range(MOBA_TOPK):
            m = jnp.max(gate, axis=0, keepdims=True)
            idx = jnp.min(jnp.where(gate == m, blk_id, nblk), axis=0, keepdims=True)
            out = jnp.where(lane == t, idx[0], out)
            gate = jnp.where(blk_id == idx, -jnp.inf, gate)
        idx_ref[0] = out


def _paged_attn_kernel(pt_ref, top_ref, q_ref, kn_ref, vn_ref, k_hbm, v_hbm, o_ref, kbuf, vbuf, sem, *, layer, ppb):
    b = pl.program_id(0)
    n_pages = MOBA_TOPK * ppb

    def page_copies(seq, slot):
        cps = []
        for h in range(A_HEADS):
            for sel in range(MOBA_TOPK):
                blk = top_ref[(seq * A_HEADS + h) * MOBA_TOPK + sel]
                for r in range(ppb):
                    page = pt_ref[seq, blk * ppb + r]
                    i = h * n_pages + sel * ppb + r
                    cps.append(pltpu.make_async_copy(k_hbm.at[layer, page, :, h, :], kbuf.at[slot, i], sem.at[0, slot]))
                    cps.append(pltpu.make_async_copy(v_hbm.at[layer, page, :, h, :], vbuf.at[slot, i], sem.at[1, slot]))
        return cps

    @pl.when(b == 0)
    def _():
        for cp in page_copies(0, 0):
            cp.start()

    @pl.when(b + 1 < pl.num_programs(0))
    def _():
        for cp in page_copies(b + 1, (b + 1) % 2):
            cp.start()

    slot = b % 2
    for cp in page_copies(b, slot):
        cp.wait()

    rows = []
    for h in range(A_HEADS):
        q = q_ref[0, h:h + 1, :]
        qh = jnp.broadcast_to(q, (SUBLANES, HEAD_DIM)).astype(BF16)
        s_own = jnp.sum(q * kn_ref[0, h:h + 1, :], axis=-1, keepdims=True) * ATTN_SCALE
        ss = [lax.dot_general(qh, kbuf[slot, h * n_pages + i].astype(BF16), NT_DIMS,
                              preferred_element_type=F32) * ATTN_SCALE for i in range(n_pages)]
        m = s_own
        for sc in ss:
            m = jnp.maximum(m, jnp.max(sc[0:1], axis=-1, keepdims=True))
        p_own = jnp.exp(s_own - m)
        l = p_own
        acc = p_own * vn_ref[0, h:h + 1, :]
        for i, sc in enumerate(ss):
            p = jnp.exp(sc - m)
            l = l + jnp.sum(p[0:1], axis=-1, keepdims=True)
            acc = acc + jnp.dot(p.astype(BF16), vbuf[slot, h * n_pages + i].astype(BF16),
                                preferred_element_type=F32)[0:1]
        rows.append(acc / l)
    o_ref[0] = jnp.concatenate(rows, axis=0)


def _paged_moba(q, k_new, v_new, k_pool, v_pool, page_table, layer):
    Bs = q.shape[0]
    assert q.shape == (Bs, A_HEADS, HEAD_DIM)
    n_pages = page_table.shape[1]
    ppb = MOBA_BLOCK // PAGE_SIZE
    assert n_pages % ppb == 0
    nblk = n_pages // ppb
    assert nblk >= MOBA_TOPK
    bps = math.gcd(nblk, 4)
    q3, k3, v3 = q, k_new, v_new
    page_blk = (1, 1, PAGE_SIZE, A_HEADS, HEAD_DIM)

    def mean_spec(p):
        return pl.BlockSpec(page_blk, lambda b, n, pt, p=p: (layer, pt[b, n * bps * ppb + p], 0, 0, 0))

    idx = pl.pallas_call(
        functools.partial(_paged_topk_kernel, nblk=nblk, bps=bps, ppb=ppb),
        grid_spec=pltpu.PrefetchScalarGridSpec(
            num_scalar_prefetch=1, grid=(Bs, nblk // bps),
            in_specs=[mean_spec(p) for p in range(bps * ppb)]
            + [pl.BlockSpec((1, A_HEADS, HEAD_DIM), lambda b, n, pt: (b, 0, 0))],
            out_specs=pl.BlockSpec((1, A_HEADS, LANES), lambda b, n, pt: (b, 0, 0)),
            scratch_shapes=[pltpu.VMEM((nblk, A_HEADS, HEAD_DIM), F32)]),
        out_shape=jax.ShapeDtypeStruct((Bs, A_HEADS, LANES), jnp.int32),
        compiler_params=_cparams(("parallel", "arbitrary")), name="paged_topk")(
            page_table, *([k_pool] * (bps * ppb)), q3)
    top = idx[:, :, :MOBA_TOPK].reshape(-1)

    n_sel_pages = MOBA_TOPK * ppb
    seq_spec = pl.BlockSpec((1, A_HEADS, HEAD_DIM), lambda b, pt, tp: (b, 0, 0))
    hbm_spec = pl.BlockSpec(memory_space=pl.ANY)
    att = pl.pallas_call(
        functools.partial(_paged_attn_kernel, layer=layer, ppb=ppb),
        grid_spec=pltpu.PrefetchScalarGridSpec(
            num_scalar_prefetch=2, grid=(Bs,),
            in_specs=[seq_spec, seq_spec, seq_spec, hbm_spec, hbm_spec],
            out_specs=seq_spec,
            scratch_shapes=[pltpu.VMEM((2, A_HEADS * n_sel_pages, PAGE_SIZE, HEAD_DIM), F32),
                            pltpu.VMEM((2, A_HEADS * n_sel_pages, PAGE_SIZE, HEAD_DIM), F32),
                            pltpu.SemaphoreType.DMA((2, 2))]),
        out_shape=jax.ShapeDtypeStruct((Bs, A_HEADS, HEAD_DIM), F32),
        compiler_params=_cparams(("arbitrary",)), name="paged_attn")(
            page_table, top, q3, k3, v3, k_pool, v_pool)
    return att


def _rope_tables(pos):
    inv = ROPE_THETA ** (-jnp.arange(ROT_HALF, dtype=F32) / ROT_HALF)
    ang = pos.astype(F32)[:, None] * inv
    cos, sin = jnp.cos(ang), jnp.sin(ang)
    T = pos.shape[0]
    rest = HEAD_DIM - ROT_DIM
    cos_t = jnp.concatenate([cos, cos, jnp.ones((T, rest), F32)], axis=-1)
    sin_t = jnp.concatenate([-sin, sin, jnp.zeros((T, rest), F32)], axis=-1)
    return cos_t, sin_t


ROW_TILE = 1024


def _row_tile(m):
    return ROW_TILE if m % ROW_TILE == 0 else m


def kernel(x_prompt, x_sample, cache_k, cache_v, state_lru_conv, state_lru_h, state_ssd_conv, state_ssd_h, page_table, norm_mix, norm_mlp, w_in_a, q_norm, k_norm, lru_conv_w, lru_conv_b, lru_wa, lru_ba, lru_wx, lru_bx, lru_lambda, w_out_a, w_in_s, ssd_conv_w, ssd_conv_b, ssd_dt_bias, ssd_a_log, ssd_d, ssd_norm, w_out_s, w_up, w_down):
    Bp, Tp, D = x_prompt.shape
    Bs, Ts, _ = x_sample.shape
    assert Ts == 1
    depth = norm_mix.shape[0]
    past_len = page_table.shape[1] * PAGE_SIZE
    Mp, Ms = Bp * Tp, Bs * Ts
    tmp, tms = _row_tile(Mp), _row_tile(Ms)
    tmo = tmp // 2 if tmp == ROW_TILE else tmp
    tmr = tmp // 4 if tmp == ROW_TILE else tmp
    assert Tp % tmr == 0
    assert Tp % tmp == 0
    xp = x_prompt.reshape(Mp, D)
    xs = x_sample.reshape(Ms, D)
    lru_w = lru_lambda.shape[1]
    d_inner = ssd_norm.shape[1]
    xbc_w = ssd_conv_w.shape[2]
    n_ssd_heads = ssd_d.shape[1]

    cos_p, sin_p = _rope_tables(jnp.arange(Tp, dtype=jnp.int32))
    cos_s, sin_s = _rope_tables(jnp.full((Ms,), past_len, dtype=jnp.int32))
    w_up_h, w_down_h = w_up.astype(BF16), w_down.astype(BF16)
    w_in_a_h, w_in_s_h = w_in_a.astype(BF16), w_in_s.astype(BF16)

    outs = {k: [] for k in ("kp", "vp", "ks", "vs", "lcp", "lcs", "lhp", "lhs", "scp", "scs", "shp", "shs")}
    for layer in range(depth):
        i = layer // 2
        if layer % 2 == 0:
            w_in = w_in_a_h
            w_out = w_out_a[i].astype(BF16)
            splits = [A_W, A_W, A_W, lru_w, lru_w]
            lru_args = (lru_conv_w[i], lru_conv_b[i], lru_wa[i], lru_ba[i], lru_wx[i], lru_bx[i], lru_lambda[i])
            heads = lambda a: a.reshape(a.shape[0], A_HEADS, HEAD_DIM)
            q, k, v, xr, yg = _norm_proj_resident(xp, norm_mix[layer], w_in, i, 0, splits, tm=tmr,
                                                  rope=((q_norm[i], k_norm[i]), cos_p, sin_p, Tp // tmr))
            att = _moba_prompt(q, k, v, batch=Bp, seq=Tp)
            rec, h_last = _lru_prompt(xr, yg, *lru_args, batch=Bp, seq=Tp)
            mixed_p = _out_proj([att, rec], w_out, xp, tm=tmo)
            outs["kp"].append(k.reshape(Bp, Tp, A_HEADS, HEAD_DIM))
            outs["vp"].append(v.reshape(Bp, Tp, A_HEADS, HEAD_DIM))
            outs["lcp"].append(xr.reshape(Bp, Tp, lru_w)[:, Tp - (CONV_K - 1):])
            outs["lhp"].append(h_last.reshape(Bp, lru_w))
            q, k, v, xr, yg = _norm_proj_resident(xs, norm_mix[layer], w_in, i, 0, splits, tm=tms,
                                                  rope=((q_norm[i], k_norm[i]), cos_s, sin_s, 1))
            att = _paged_moba(heads(q), heads(k), heads(v), cache_k, cache_v, page_table, layer=i)
            att = att.reshape(Bs, A_W).astype(BF16)
            buf = state_lru_conv[i]
            rec, h_new = _lru_sample(xr, yg, buf.swapaxes(0, 1), state_lru_h[i], *lru_args)
            mixed_s = _out_proj([att, rec], w_out, xs, tm=tms)
            outs["ks"].append(k.reshape(Bs, Ts, A_HEADS, HEAD_DIM))
            outs["vs"].append(v.reshape(Bs, Ts, A_HEADS, HEAD_DIM))
            outs["lcs"].append(jnp.concatenate([buf[:, 1:], xr[:, None, :]], axis=1))
            outs["lhs"].append(h_new)
        else:
            main_w = d_inner + xbc_w
            w_dt = jnp.pad(w_in_s[i][:, main_w:], ((0, 0), (0, LANES - n_ssd_heads))).astype(BF16)
            w_out = w_out_s[i].astype(BF16)
            ssd_args = (ssd_conv_w[i], ssd_conv_b[i], ssd_dt_bias[i], ssd_a_log[i], ssd_d[i], ssd_norm[i])
            unslab = lambda a: a.swapaxes(0, 1).reshape(a.shape[1], -1)
            z, dt = _norm_proj_resident(xp, norm_mix[layer], w_in_s_h, i, 0, [d_inner], tm=tmo, extra_w=w_dt, slab=True)
            xbc, = _norm_proj_resident(xp, norm_mix[layer], w_in_s_h, i, d_inner, [xbc_w], tm=tmr, slab=True)
            yn, s_last = _ssd_prompt(xbc, z, dt, *ssd_args, batch=Bp, seq=Tp)
            mixed_p = _out_proj([yn], w_out, xp, tm=tmo)
            tail = xbc.reshape(-1, Bp, Tp, LANES)[:, :, Tp - (CONV_K - 1):]
            outs["scp"].append(tail.transpose(1, 2, 0, 3).reshape(Bp, CONV_K - 1, xbc_w))
            outs["shp"].append(s_last.reshape(Bp, n_ssd_heads, SSD_HEAD_DIM, SSD_STATE))
            z, dt = _norm_proj_resident(xs, norm_mix[layer], w_in_s_h, i, 0, [d_inner], tm=tms, extra_w=w_dt, slab=True)
            xbc, = _norm_proj_resident(xs, norm_mix[layer], w_in_s_h, i, d_inner, [xbc_w], tm=tms, slab=True)
            z, xbc = unslab(z), unslab(xbc)
            buf = state_ssd_conv[i]
            yn, s_new = _ssd_sample(xbc, buf.swapaxes(0, 1), z, dt,
                                    state_ssd_h.reshape((-1,) + state_ssd_h.shape[2:]), *ssd_args, s_off=i * Bs)
            mixed_s = _out_proj([yn], w_out, xs, tm=tms)
            outs["scs"].append(jnp.concatenate([buf[:, 1:], xbc[:, None, :]], axis=1))
            outs["shs"].append(s_new)
        xp = _mlp(mixed_p, norm_mlp[layer], w_up_h, w_down_h, layer, tm=tmp)
        xs = _mlp(mixed_s, norm_mlp[layer], w_up_h, w_down_h, layer, tm=tms)

    st = lambda key: jnp.stack(outs[key])
    return (xp.reshape(Bp, Tp, D), xs.reshape(Bs, Ts, D), st("kp"), st("vp"), st("ks"), st("vs"),
            st("lcp"), st("lcs"), st("lhp"), st("lhs"), st("scp"), st("scs"), st("shp"), st("shs"))
```
